```python
import math
import jax, jax.numpy as jnp
from jax import lax
import numpy as np

D_MODEL = 1024
BATCH = 8
SEQ = 4096
DEPTH = 1

HEAD_DIM = 64
SWA_Q_HEADS = 8
SWA_KV_HEADS = 2
SWA_GROUP = SWA_Q_HEADS // SWA_KV_HEADS
WINDOW = 128
SWA_WIDTH = SWA_Q_HEADS * HEAD_DIM
DIFF_HEADS = 4
DIFF_V_DIM = 2 * HEAD_DIM
DIFF_WIDTH = DIFF_HEADS * DIFF_V_DIM
MIX_WIDTH = SWA_WIDTH + DIFF_WIDTH
Q_BLOCK = 128
RMS_EPS = 1e-6
SUBLN_EPS = 1e-5

A_Q_COLS = SWA_Q_HEADS * HEAD_DIM
A_K_COLS = SWA_KV_HEADS * HEAD_DIM
A_V_COLS = SWA_KV_HEADS * HEAD_DIM
B_Q_COLS = DIFF_HEADS * 2 * HEAD_DIM
B_K_COLS = DIFF_HEADS * 2 * HEAD_DIM
B_V_COLS = DIFF_HEADS * DIFF_V_DIM
GATE_COLS = MIX_WIDTH
IN_COLS = A_Q_COLS + A_K_COLS + A_V_COLS + B_Q_COLS + B_K_COLS + B_V_COLS + GATE_COLS

kernel_name = "hybrid_swa_sink_diffattn_gated_alibi"


def rmsnorm(x, g, eps=RMS_EPS):
    xf = x.astype(jnp.float32)
    y = xf * lax.rsqrt(jnp.mean(xf * xf, axis=-1, keepdims=True) + eps)
    return (y * g.astype(jnp.float32)).astype(x.dtype)


def alibi_slopes(n_heads):
    return 2.0 ** (-8.0 * jnp.arange(1, n_heads + 1, dtype=jnp.float32) / n_heads)


def lambda_init_fn(layer_idx):
    return 0.8 - 0.6 * math.exp(-0.3 * layer_idx)


def sliding_window_gqa_sinks(q, k, v, sinks):
    B, S = q.shape[0], q.shape[1]
    nb = S // WINDOW
    qb = q.reshape(B, nb, WINDOW, SWA_KV_HEADS, SWA_GROUP, HEAD_DIM)
    kb = k.reshape(B, nb, WINDOW, SWA_KV_HEADS, HEAD_DIM)
    vb = v.reshape(B, nb, WINDOW, SWA_KV_HEADS, HEAD_DIM)
    pad = ((0, 0), (1, 0), (0, 0), (0, 0), (0, 0))
    k_band = jnp.concatenate([jnp.pad(kb, pad)[:, :-1], kb], axis=2)
    v_band = jnp.concatenate([jnp.pad(vb, pad)[:, :-1], vb], axis=2)
    scale = HEAD_DIM ** -0.5
    s = jnp.einsum('bnqhgd,bnkhd->bhgnqk', qb, k_band).astype(jnp.float32) * scale
    q_pos = jnp.arange(WINDOW)[:, None] + WINDOW
    k_pos = jnp.arange(2 * WINDOW)[None, :]
    dist = q_pos - k_pos
    valid = (dist >= 0) & (dist < WINDOW)
    not_first = (jnp.arange(nb)[:, None, None] > 0) | (k_pos >= WINDOW)[None]
    valid = valid[None] & not_first
    slopes = alibi_slopes(SWA_Q_HEADS).reshape(SWA_KV_HEADS, SWA_GROUP)
    s = s - slopes[:, :, None, None, None] * dist.astype(jnp.float32)
    s = jnp.where(valid, s, -jnp.inf)
    sink = sinks.astype(jnp.float32).reshape(SWA_KV_HEADS, SWA_GROUP)[:, :, None, None, None]
    m = jnp.maximum(jnp.max(s, axis=-1, keepdims=True), sink)
    p = jnp.exp(s - m)
    probs = p / (jnp.sum(p, axis=-1, keepdims=True) + jnp.exp(sink - m))
    out = jnp.einsum('bhgnqk,bnkhd->bnqhgd', probs.astype(v.dtype), v_band)
    return out.reshape(B, S, SWA_WIDTH)


def differential_attention(q, k, v, lam, subln_g, lambda_init):
    B, S = q.shape[0], q.shape[1]
    nb = S // Q_BLOCK
    qb = q.reshape(B, nb, Q_BLOCK, DIFF_HEADS, 2, HEAD_DIM).transpose(1, 0, 2, 3, 4, 5)
    slopes = alibi_slopes(DIFF_HEADS)
    scale = HEAD_DIM ** -0.5
    k_pos = jnp.arange(S)

    def one_block(args):
        blk, q_blk = args
        q_pos = blk * Q_BLOCK + jnp.arange(Q_BLOCK)
        dist = q_pos[:, None] - k_pos[None, :]
        s = jnp.einsum('bqhcd,bkhcd->bhcqk', q_blk, k).astype(jnp.float32) * scale
        s = s - slopes[:, None, None, None] * dist.astype(jnp.float32)
        s = jnp.where(dist >= 0, s, -jnp.inf)
        p = jax.nn.softmax(s, axis=-1)
        a = p[:, :, 0] - lam * p[:, :, 1]
        return jnp.einsum('bhqk,bkhe->bqhe', a.astype(v.dtype), v)

    out = lax.map(one_block, (jnp.arange(nb), qb))
    out = out.transpose(1, 0, 2, 3, 4).reshape(B, S, DIFF_HEADS, DIFF_V_DIM)
    out = rmsnorm(out, subln_g, SUBLN_EPS) * (1.0 - lambda_init)
    return out.reshape(B, S, DIFF_WIDTH)


def hybrid_layer(x, norm_g, w_in, sinks, lq1, lk1, lq2, lk2, subln_g, w_out, layer_idx):
    B, S, _ = x.shape
    h = rmsnorm(x, norm_g)
    proj = jnp.einsum('bsd,dc->bsc', h, w_in)
    cuts = np.cumsum([A_Q_COLS, A_K_COLS, A_V_COLS, B_Q_COLS, B_K_COLS, B_V_COLS])
    aq, ak, av, bq, bk, bv, gate = jnp.split(proj, cuts, axis=-1)
    aq = aq.reshape(B, S, SWA_Q_HEADS, HEAD_DIM)
    ak = ak.reshape(B, S, SWA_KV_HEADS, HEAD_DIM)
    av = av.reshape(B, S, SWA_KV_HEADS, HEAD_DIM)
    out_a = sliding_window_gqa_sinks(aq, ak, av, sinks)
    lambda_init = lambda_init_fn(layer_idx)
    lam = (jnp.exp(jnp.sum(lq1.astype(jnp.float32) * lk1.astype(jnp.float32)))
           - jnp.exp(jnp.sum(lq2.astype(jnp.float32) * lk2.astype(jnp.float32)))
           + lambda_init)
    bq = bq.reshape(B, S, DIFF_HEADS, 2, HEAD_DIM)
    bk = bk.reshape(B, S, DIFF_HEADS, 2, HEAD_DIM)
    bv = bv.reshape(B, S, DIFF_HEADS, DIFF_V_DIM)
    out_b = differential_attention(bq, bk, bv, lam, subln_g, lambda_init)
    mixed = jnp.concatenate([out_a, out_b], axis=-1) * jax.nn.silu(gate)
    return x + jnp.einsum('bsc,cd->bsd', mixed, w_out)


def setup_inputs(seed: int = 0) -> dict:
    key = jax.random.key(seed)
    ks = jax.random.split(key, 12)
    f32 = jnp.float32
    x = jax.random.normal(ks[0], (BATCH, SEQ, D_MODEL), f32)
    norm_g = 1.0 + 0.02 * jax.random.normal(ks[1], (DEPTH, D_MODEL), f32)
    w_in = jax.random.normal(ks[2], (DEPTH, D_MODEL, IN_COLS), f32) * D_MODEL ** -0.5
    sinks = 0.5 * jax.random.normal(ks[3], (DEPTH, SWA_Q_HEADS), f32)
    lq1 = 0.1 * jax.random.normal(ks[4], (DEPTH, HEAD_DIM), f32)
    lk1 = 0.1 * jax.random.normal(ks[5], (DEPTH, HEAD_DIM), f32)
    lq2 = 0.1 * jax.random.normal(ks[6], (DEPTH, HEAD_DIM), f32)
    lk2 = 0.1 * jax.random.normal(ks[7], (DEPTH, HEAD_DIM), f32)
    subln_g = 1.0 + 0.02 * jax.random.normal(ks[8], (DEPTH, DIFF_V_DIM), f32)
    w_out = jax.random.normal(ks[9], (DEPTH, MIX_WIDTH, D_MODEL), f32) * MIX_WIDTH ** -0.5
    final_g = 1.0 + 0.02 * jax.random.normal(ks[10], (D_MODEL,), f32)
    return {"x": x, "norm_g": norm_g, "w_in": w_in, "sinks": sinks,
            "lambda_q1": lq1, "lambda_k1": lk1, "lambda_q2": lq2, "lambda_k2": lk2,
            "subln_g": subln_g, "w_out": w_out, "final_g": final_g}


def reference(x, norm_g, w_in, sinks, lambda_q1, lambda_k1, lambda_q2, lambda_k2,
              subln_g, w_out, final_g):
    h = x
    for layer in range(DEPTH):
        h = hybrid_layer(h, norm_g[layer], w_in[layer], sinks[layer],
                         lambda_q1[layer], lambda_k1[layer], lambda_q2[layer], lambda_k2[layer],
                         subln_g[layer], w_out[layer], layer)
    return rmsnorm(h, final_g)
```

```python
import functools
import math

import jax
import jax.numpy as jnp
from jax import lax
from jax.experimental import pallas as pl
from jax.experimental.pallas import tpu as pltpu

F32 = jnp.float32
BF16 = jnp.bfloat16

HEAD_DIM = 64
LANES = 128
SWA_Q_HEADS = 8
SWA_KV_HEADS = 2
WINDOW = 128
DIFF_HEADS = 4
RMS_EPS = 1e-6
SUBLN_EPS = 1e-5
NEG_BIG = -1e30

A_Q = SWA_Q_HEADS * HEAD_DIM
A_KV = SWA_KV_HEADS * HEAD_DIM
B_W = DIFF_HEADS * 2 * HEAD_DIM
MIX = A_Q + B_W
OFF_AK = A_Q
OFF_AV = OFF_AK + A_KV
OFF_BQ = OFF_AV + A_KV
OFF_BK = OFF_BQ + B_W
OFF_BV = OFF_BK + B_W
OFF_GATE = OFF_BV + B_W
IN_COLS = OFF_GATE + MIX

TM_PROJ = 512
TQ_SWA = 512
TQ_DIFF = 256
VMEM_LIMIT = 56 * 1024 * 1024


def _lambda_init(layer_idx):
    return 0.8 - 0.6 * math.exp(-0.3 * layer_idx)


def _nt_dot(a, b):
    return lax.dot_general(a, b, (((1,), (1,)), ((), ())), preferred_element_type=F32)


def _inproj_kernel(x_ref, g_ref, w_ref, aq_ref, ak_ref, av_ref, bq_ref, bk_ref, bv_ref, sg_ref):
    x = x_ref[...]
    ms = jnp.mean(x * x, axis=-1, keepdims=True)
    h = (x * lax.rsqrt(ms + RMS_EPS) * g_ref[...]).astype(BF16)
    scale = HEAD_DIM ** -0.5

    def proj(lo, hi):
        return jnp.dot(h, w_ref[:, lo:hi], preferred_element_type=F32)

    def dup_heads(t, out_ref):
        lane = lax.broadcasted_iota(jnp.int32, t.shape, 1)
        lo = lane < HEAD_DIM
        r = pltpu.roll(t, HEAD_DIM, axis=1)
        out_ref[:, 0:LANES] = jnp.where(lo, t, r).astype(BF16)
        out_ref[:, LANES:2 * LANES] = jnp.where(lo, r, t).astype(BF16)

    aq_ref[...] = (proj(0, OFF_AK) * scale).astype(BF16)
    dup_heads(proj(OFF_AK, OFF_AV), ak_ref)
    dup_heads(proj(OFF_AV, OFF_BQ), av_ref)
    bq_ref[...] = (proj(OFF_BQ, OFF_BK) * scale).astype(BF16)
    bk_ref[...] = proj(OFF_BK, OFF_BV).astype(BF16)
    bv_ref[...] = proj(OFF_BV, OFF_GATE).astype(BF16)
    gate = proj(OFF_GATE, IN_COLS)
    sg_ref[...] = (gate * (1.0 / (1.0 + jnp.exp(-gate)))).astype(BF16)


def _inproj(xf, g, w):
    n, d = xf.shape
    tm = TM_PROJ
    row = lambda i: (i, 0)
    whole = lambda i: (0, 0)
    widths = (A_Q, 2 * LANES, 2 * LANES, B_W, B_W, B_W, MIX)
    return pl.pallas_call(
        _inproj_kernel,
        grid=(n // tm,),
        in_specs=[pl.BlockSpec((tm, d), row),
                  pl.BlockSpec((1, d), whole),
                  pl.BlockSpec((d, IN_COLS), whole)],
        out_specs=[pl.BlockSpec((tm, wd), row) for wd in widths],
        out_shape=[jax.ShapeDtypeStruct((n, wd), BF16) for wd in widths],
        compiler_params=pltpu.CompilerParams(
            dimension_semantics=("arbitrary",), vmem_limit_bytes=VMEM_LIMIT),
        name="inproj",
    )(xf, g, w)


def _swa_kernel(sinks_ref, q_ref, kp_ref, kc_ref, vp_ref, vc_ref, sg_ref, o_ref, kcat, vcat):
    i = pl.program_id(1)
    kcat[0:WINDOW, :] = kp_ref[0]
    kcat[WINDOW:, :] = kc_ref[0]
    vcat[0:WINDOW, :] = vp_ref[0]
    vcat[WINDOW:, :] = vc_ref[0]

    r = lax.broadcasted_iota(jnp.int32, (WINDOW, 2 * WINDOW), 0)
    c = lax.broadcasted_iota(jnp.int32, (WINDOW, 2 * WINDOW), 1)
    dist = r + WINDOW - c
    distf = dist.astype(F32)
    band = (dist >= 0) & (dist < WINDOW)
    first_band = band & ((i > 0) | (c >= WINDOW))
    lane_lo = lax.broadcasted_iota(jnp.int32, (WINDOW, LANES), 1) < HEAD_DIM
    group = SWA_Q_HEADS // SWA_KV_HEADS

    for w in range(TQ_SWA // WINDOW):
        rows = slice(w * WINDOW, (w + 1) * WINDOW)
        krows = slice(w * WINDOW, (w + 2) * WINDOW)
        valid = first_band if w == 0 else band
        for cg in range(A_Q // LANES):
            cols = slice(cg * LANES, (cg + 1) * LANES)
            g = (2 * cg) // group
            gcols = slice(g * LANES, (g + 1) * LANES)
            qc = q_ref[0, rows, cols]
            kw = kcat[krows, gcols]
            vw = vcat[krows, gcols]
            outs = []
            for half in range(2):
                hh = 2 * cg + half
                slope = 2.0 ** (-8.0 * (hh + 1) / SWA_Q_HEADS)
                qh = jnp.where(lane_lo if half == 0 else ~lane_lo, qc, jnp.zeros_like(qc))
                s = _nt_dot(qh, kw) - slope * distf
                s = jnp.where(valid, s, NEG_BIG)
                sink = sinks_ref[hh]
                m = jnp.maximum(jnp.max(s, axis=-1, keepdims=True), sink)
                p = jnp.exp(s - m)
                denom = jnp.sum(p, axis=-1, keepdims=True) + jnp.exp(sink - m)
                o = jnp.dot(p.astype(BF16), vw, preferred_element_type=F32)
                outs.append(o / denom)
            o = jnp.where(lane_lo, outs[0], outs[1])
            o_ref[0, rows, cols] = (o * sg_ref[0, rows, cols].astype(F32)).astype(BF16)


def _swa(sinks, aq, akd, avd, sg):
    b, s, _ = aq.shape
    tq = TQ_SWA
    wpt = tq // WINDOW
    cur = lambda bi, i: (bi, i, 0)
    prev = lambda bi, i: (bi, jnp.maximum(i * wpt - 1, 0), 0)
    return pl.pallas_call(
        _swa_kernel,
        grid=(b, s // tq),
        in_specs=[pl.BlockSpec(memory_space=pltpu.SMEM),
                  pl.BlockSpec((1, tq, A_Q), cur),
                  pl.BlockSpec((1, WINDOW, 2 * LANES), prev),
                  pl.BlockSpec((1, tq, 2 * LANES), cur),
                  pl.BlockSpec((1, WINDOW, 2 * LANES), prev),
                  pl.BlockSpec((1, tq, 2 * LANES), cur),
                  pl.BlockSpec((1, tq, A_Q), cur)],
        out_specs=pl.BlockSpec((1, tq, A_Q), cur),
        out_shape=jax.ShapeDtypeStruct((b, s, A_Q), BF16),
        scratch_shapes=[pltpu.VMEM((tq + WINDOW, 2 * LANES), BF16),
                        pltpu.VMEM((tq + WINDOW, 2 * LANES), BF16)],
        compiler_params=pltpu.CompilerParams(
            dimension_semantics=("arbitrary", "arbitrary"), vmem_limit_bytes=VMEM_LIMIT),
        name="swa",
    )(sinks, aq, akd, akd, avd, avd, sg)


def _diff_kernel(slopes_ref, lq1_ref, lk1_ref, lq2_ref, lk2_ref, subg_ref,
                 q_ref, k_ref, v_ref, sg_ref, o_ref, *, lambda_init):
    t = TQ_DIFF
    hd = pl.program_id(1)
    i = pl.program_id(2)
    slope = slopes_ref[hd]

    q = q_ref[0]
    lane = lax.broadcasted_iota(jnp.int32, (t, LANES), 1)
    zero = jnp.zeros_like(q)
    q1 = jnp.where(lane < HEAD_DIM, q, zero)
    q2 = jnp.where(lane >= HEAD_DIM, q, zero)
    colpos = lax.broadcasted_iota(jnp.int32, (1, t), 1).astype(F32)
    row = lax.broadcasted_iota(jnp.int32, (t, t), 0)
    col = lax.broadcasted_iota(jnp.int32, (t, t), 1)
    causal = row >= col

    def softmax_step(s, v, m, l, acc):
        m_new = jnp.maximum(m, jnp.max(s, axis=-1, keepdims=True))
        alpha = jnp.exp(m - m_new)
        p = jnp.exp(s - m_new)
        l = alpha * l + jnp.sum(p, axis=-1, keepdims=True)
        acc = alpha * acc + jnp.dot(p.astype(BF16), v, preferred_element_type=F32)
        return m_new, l, acc

    def step(j, carry, masked):
        m1, l1, a1, m2, l2, a2 = carry
        start = pl.multiple_of(j * t, t)
        k = k_ref[0, pl.ds(start, t), :]
        v = v_ref[0, pl.ds(start, t), :]
        bias = slope * (colpos + ((j - i) * t).astype(F32))
        s1 = _nt_dot(q1, k) + bias
        s2 = _nt_dot(q2, k) + bias
        if masked:
            s1 = jnp.where(causal, s1, NEG_BIG)
            s2 = jnp.where(causal, s2, NEG_BIG)
        m1, l1, a1 = softmax_step(s1, v, m1, l1, a1)
        m2, l2, a2 = softmax_step(s2, v, m2, l2, a2)
        return m1, l1, a1, m2, l2, a2

    mi = jnp.full((t, 1), NEG_BIG, F32)
    li = jnp.zeros((t, 1), F32)
    ai = jnp.zeros((t, LANES), F32)
    carry = lax.fori_loop(0, i, functools.partial(step, masked=False), (mi, li, ai, mi, li, ai))
    m1, l1, a1, m2, l2, a2 = step(i, carry, True)

    lam = (jnp.exp(jnp.sum(lq1_ref[...] * lk1_ref[...], axis=-1, keepdims=True))
           - jnp.exp(jnp.sum(lq2_ref[...] * lk2_ref[...], axis=-1, keepdims=True))
           + lambda_init)
    o = a1 / l1 - lam * (a2 / l2)
    o = o * lax.rsqrt(jnp.mean(o * o, axis=-1, keepdims=True) + SUBLN_EPS) * subg_ref[...]
    o = o * (1.0 - lambda_init)
    o_ref[0] = (o * sg_ref[0].astype(F32)).astype(BF16)


def _diff(slopes, lq1, lk1, lq2, lk2, subg, bq, bk, bv, sg, lambda_init):
    b, s, _ = bq.shape
    t = TQ_DIFF
    smem = pl.BlockSpec(memory_space=pltpu.SMEM)
    small = lambda shape: pl.BlockSpec(shape, lambda bi, h, i: (0, 0))
    return pl.pallas_call(
        functools.partial(_diff_kernel, lambda_init=lambda_init),
        grid=(b, DIFF_HEADS, s // t),
        in_specs=[smem,
                  small((1, HEAD_DIM)), small((1, HEAD_DIM)),
                  small((1, HEAD_DIM)), small((1, HEAD_DIM)),
                  small((1, LANES)),
                  pl.BlockSpec((1, t, LANES), lambda bi, h, i: (bi, i, h)),
                  pl.BlockSpec((1, s, LANES), lambda bi, h, i: (bi, 0, h)),
                  pl.BlockSpec((1, s, LANES), lambda bi, h, i: (bi, 0, h)),
                  pl.BlockSpec((1, t, LANES), lambda bi, h, i: (bi, i, A_Q // LANES + h))],
        out_specs=pl.BlockSpec((1, t, LANES), lambda bi, h, i: (bi, i, h)),
        out_shape=jax.ShapeDtypeStruct((b, s, B_W), BF16),
        compiler_params=pltpu.CompilerParams(
            dimension_semantics=("arbitrary", "arbitrary", "arbitrary"),
            vmem_limit_bytes=VMEM_LIMIT),
        name="diffattn",
    )(slopes, lq1, lk1, lq2, lk2, subg, bq, bk, bv, sg)


def _outproj_kernel(x_ref, ma_ref, mb_ref, w_ref, fg_ref, o_ref, *, final_norm):
    y = (x_ref[...]
         + jnp.dot(ma_ref[...], w_ref[0:A_Q, :], preferred_element_type=F32)
         + jnp.dot(mb_ref[...], w_ref[A_Q:MIX, :], preferred_element_type=F32))
    if final_norm:
        ms = jnp.mean(y * y, axis=-1, keepdims=True)
        y = y * lax.rsqrt(ms + RMS_EPS) * fg_ref[...]
    o_ref[...] = y


def _outproj(xf, ma, mb, w, fg, final_norm):
    n, d = xf.shape
    tm = TM_PROJ
    row = lambda i: (i, 0)
    whole = lambda i: (0, 0)
    return pl.pallas_call(
        functools.partial(_outproj_kernel, final_norm=final_norm),
        grid=(n // tm,),
        in_specs=[pl.BlockSpec((tm, d), row),
                  pl.BlockSpec((tm, A_Q), row),
                  pl.BlockSpec((tm, B_W), row),
                  pl.BlockSpec((MIX, d), whole),
                  pl.BlockSpec((1, d), whole)],
        out_specs=pl.BlockSpec((tm, d), row),
        out_shape=jax.ShapeDtypeStruct((n, d), F32),
        compiler_params=pltpu.CompilerParams(
            dimension_semantics=("arbitrary",), vmem_limit_bytes=VMEM_LIMIT),
        name="outproj",
    )(xf, ma, mb, w, fg)


def kernel(x, norm_g, w_in, sinks, lambda_q1, lambda_k1, lambda_q2, lambda_k2,
           subln_g, w_out, final_g):
    b, s, d = x.shape
    depth = norm_g.shape[0]
    n = b * s
    diff_slopes = jnp.asarray(
        [2.0 ** (-8.0 * (h + 1) / DIFF_HEADS) for h in range(DIFF_HEADS)], F32)
    hf = x.reshape(n, d)
    for layer in range(depth):
        aq, akd, avd, bq, bk, bv, sg = _inproj(
            hf, norm_g[layer].reshape(1, d), w_in[layer].astype(BF16))
        shp = lambda a: a.reshape(b, s, a.shape[-1])
        sg3 = shp(sg)
        mixed_a = _swa(sinks[layer], shp(aq), shp(akd), shp(avd), sg3)
        mixed_b = _diff(diff_slopes,
                        lambda_q1[layer].reshape(1, HEAD_DIM), lambda_k1[layer].reshape(1, HEAD_DIM),
                        lambda_q2[layer].reshape(1, HEAD_DIM), lambda_k2[layer].reshape(1, HEAD_DIM),
                        subln_g[layer].reshape(1, LANES),
                        shp(bq), shp(bk), shp(bv), sg3, _lambda_init(layer))
        hf = _outproj(hf, mixed_a.reshape(n, A_Q), mixed_b.reshape(n, B_W),
                      w_out[layer].astype(BF16), final_g.reshape(1, d),
                      final_norm=(layer == depth - 1))
    return hf.reshape(b, s, d)
```

```python
import functools
import math

import jax
import jax.numpy as jnp
from jax import lax
from jax.experimental import pallas as pl
from jax.experimental.pallas import tpu as pltpu

F32 = jnp.float32
BF16 = jnp.bfloat16

HEAD_DIM = 64
LANES = 128
SWA_Q_HEADS = 8
SWA_KV_HEADS = 2
WINDOW = 128
DIFF_HEADS = 4
RMS_EPS = 1e-6
SUBLN_EPS = 1e-5
NEG_BIG = -1e30
LOG2E = math.log2(math.e)

A_Q = SWA_Q_HEADS * HEAD_DIM
A_KV = SWA_KV_HEADS * HEAD_DIM
B_W = DIFF_HEADS * 2 * HEAD_DIM
MIX = A_Q + B_W
OFF_AK = A_Q
OFF_AV = OFF_AK + A_KV
OFF_BQ = OFF_AV + A_KV
OFF_BK = OFF_BQ + B_W
OFF_BV = OFF_BK + B_W
OFF_GATE = OFF_BV + B_W
IN_COLS = OFF_GATE + MIX

TM_PROJ = 512
TQ_SWA = 512
T_DIFF = 256
NG_DIFF = 4
VMEM_LIMIT = 56 * 1024 * 1024


def _lambda_init(layer_idx):
    return 0.8 - 0.6 * math.exp(-0.3 * layer_idx)


def _nt_dot(a, b):
    return lax.dot_general(a, b, (((1,), (1,)), ((), ())), preferred_element_type=F32)


def _inproj_kernel(x_ref, g_ref, w_ref, wqt_ref, wvt_ref,
                   aq_ref, ak_ref, av_ref, bqt_ref, bk_ref, bvt_ref, sg_ref):
    x = x_ref[0]
    ms = jnp.mean(x * x, axis=-1, keepdims=True)
    h = (x * lax.rsqrt(ms + RMS_EPS) * g_ref[...]).astype(BF16)
    scale = HEAD_DIM ** -0.5

    def proj(lo, hi):
        return jnp.dot(h, w_ref[:, lo:hi], preferred_element_type=F32)

    def dup_heads(t, out_ref):
        lane = lax.broadcasted_iota(jnp.int32, t.shape, 1)
        lo = lane < HEAD_DIM
        r = pltpu.roll(t, HEAD_DIM, axis=1)
        out_ref[0, :, 0:LANES] = jnp.where(lo, t, r).astype(BF16)
        out_ref[0, :, LANES:2 * LANES] = jnp.where(lo, r, t).astype(BF16)

    aq_ref[0] = (proj(0, OFF_AK) * scale).astype(BF16)
    dup_heads(proj(OFF_AK, OFF_AV), ak_ref)
    dup_heads(proj(OFF_AV, OFF_BQ), av_ref)
    bqt_ref[0] = (_nt_dot(wqt_ref[...], h) * (scale * LOG2E)).astype(BF16)
    bk_ref[0] = proj(OFF_BK, OFF_BV).astype(BF16)
    bvt = _nt_dot(wvt_ref[...], h).astype(BF16)
    for c in range(TM_PROJ // T_DIFF):
        bvt_ref[0, c] = bvt[:, c * T_DIFF:(c + 1) * T_DIFF]
    gate = proj(OFF_GATE, IN_COLS)
    sg_ref[0] = (gate * (1.0 / (1.0 + jnp.exp(-gate)))).astype(BF16)


def _inproj(x, g, w, wqt, wvt):
    b, s, d = x.shape
    tm = TM_PROJ
    row = lambda bi, i: (bi, i, 0)
    whole = lambda bi, i: (0, 0)
    bsd = lambda wd: jax.ShapeDtypeStruct((b, s, wd), BF16)
    return pl.pallas_call(
        _inproj_kernel,
        grid=(b, s // tm),
        in_specs=[pl.BlockSpec((1, tm, d), row),
                  pl.BlockSpec((1, d), whole),
                  pl.BlockSpec((d, IN_COLS), whole),
                  pl.BlockSpec((B_W, d), whole),
                  pl.BlockSpec((B_W, d), whole)],
        out_specs=[pl.BlockSpec((1, tm, A_Q), row),
                   pl.BlockSpec((1, tm, 2 * LANES), row),
                   pl.BlockSpec((1, tm, 2 * LANES), row),
                   pl.BlockSpec((1, B_W, tm), lambda bi, i: (bi, 0, i)),
                   pl.BlockSpec((1, tm, B_W), row),
                   pl.BlockSpec((1, tm // T_DIFF, B_W, T_DIFF), lambda bi, i: (bi, i, 0, 0)),
                   pl.BlockSpec((1, tm, MIX), row)],
        out_shape=[bsd(A_Q), bsd(2 * LANES), bsd(2 * LANES),
                   jax.ShapeDtypeStruct((b, B_W, s), BF16),
                   bsd(B_W),
                   jax.ShapeDtypeStruct((b, s // T_DIFF, B_W, T_DIFF), BF16),
                   bsd(MIX)],
        compiler_params=pltpu.CompilerParams(
            dimension_semantics=("arbitrary", "arbitrary"), vmem_limit_bytes=VMEM_LIMIT),
        name="inproj",
    )(x, g, w, wqt, wvt)


def _swa_kernel(sinks_ref, q_ref, kp_ref, kc_ref, vp_ref, vc_ref, sg_ref, o_ref, kcat, vcat):
    i = pl.program_id(1)
    kcat[0:WINDOW, :] = kp_ref[0]
    kcat[WINDOW:, :] = kc_ref[0]
    vcat[0:WINDOW, :] = vp_ref[0]
    vcat[WINDOW:, :] = vc_ref[0]

    r = lax.broadcasted_iota(jnp.int32, (WINDOW, 2 * WINDOW), 0)
    c = lax.broadcasted_iota(jnp.int32, (WINDOW, 2 * WINDOW), 1)
    dist = r + WINDOW - c
    distf = dist.astype(F32)
    band = (dist >= 0) & (dist < WINDOW)
    first_band = band & ((i > 0) | (c >= WINDOW))
    lane_lo = lax.broadcasted_iota(jnp.int32, (WINDOW, LANES), 1) < HEAD_DIM
    group = SWA_Q_HEADS // SWA_KV_HEADS

    for w in range(TQ_SWA // WINDOW):
        rows = slice(w * WINDOW, (w + 1) * WINDOW)
        krows = slice(w * WINDOW, (w + 2) * WINDOW)
        valid = first_band if w == 0 else band
        for cg in range(A_Q // LANES):
            cols = slice(cg * LANES, (cg + 1) * LANES)
            g = (2 * cg) // group
            gcols = slice(g * LANES, (g + 1) * LANES)
            qc = q_ref[0, rows, cols]
            kw = kcat[krows, gcols]
            vw = vcat[krows, gcols]
            outs = []
            for half in range(2):
                hh = 2 * cg + half
                slope = 2.0 ** (-8.0 * (hh + 1) / SWA_Q_HEADS)
                qh = jnp.where(lane_lo if half == 0 else ~lane_lo, qc, jnp.zeros_like(qc))
                s = _nt_dot(qh, kw) - slope * distf
                s = jnp.where(valid, s, NEG_BIG)
                sink = sinks_ref[hh]
                m = jnp.maximum(jnp.max(s, axis=-1, keepdims=True), sink)
                p = jnp.exp(s - m)
                denom = jnp.sum(p, axis=-1, keepdims=True) + jnp.exp(sink - m)
                o = jnp.dot(p.astype(BF16), vw, preferred_element_type=F32)
                outs.append(o / denom)
            o = jnp.where(lane_lo, outs[0], outs[1])
            o_ref[0, rows, cols] = (o * sg_ref[0, rows, cols].astype(F32)).astype(BF16)


def _swa(sinks, aq, akd, avd, sg):
    b, s, _ = aq.shape
    tq = TQ_SWA
    wpt = tq // WINDOW
    cur = lambda bi, i: (bi, i, 0)
    prev = lambda bi, i: (bi, jnp.maximum(i * wpt - 1, 0), 0)
    return pl.pallas_call(
        _swa_kernel,
        grid=(b, s // tq),
        in_specs=[pl.BlockSpec(memory_space=pltpu.SMEM),
                  pl.BlockSpec((1, tq, A_Q), cur),
                  pl.BlockSpec((1, WINDOW, 2 * LANES), prev),
                  pl.BlockSpec((1, tq, 2 * LANES), cur),
                  pl.BlockSpec((1, WINDOW, 2 * LANES), prev),
                  pl.BlockSpec((1, tq, 2 * LANES), cur),
                  pl.BlockSpec((1, tq, A_Q), cur)],
        out_specs=pl.BlockSpec((1, tq, A_Q), cur),
        out_shape=jax.ShapeDtypeStruct((b, s, A_Q), BF16),
        scratch_shapes=[pltpu.VMEM((tq + WINDOW, 2 * LANES), BF16),
                        pltpu.VMEM((tq + WINDOW, 2 * LANES), BF16)],
        compiler_params=pltpu.CompilerParams(
            dimension_semantics=("arbitrary", "arbitrary"), vmem_limit_bytes=VMEM_LIMIT),
        name="swa",
    )(sinks, aq, akd, akd, avd, avd, sg)


def _diff_kernel(slopes_ref, lq1_ref, lk1_ref, lq2_ref, lk2_ref, subg_ref,
                 qt_ref, k_ref, vt_ref, sg_ref, o_ref,
                 qa_ref, s_ref, m_ref, l_ref, acc_ref, *, lambda_init):
    t = T_DIFF
    hd = pl.program_id(1)
    i = pl.program_id(2)
    slope = slopes_ref[hd] * LOG2E

    kk = lax.broadcasted_iota(jnp.int32, (t, LANES), 0).astype(F32)
    lane = lax.broadcasted_iota(jnp.int32, (t, LANES), 1)
    ab = slope * kk
    ab_hi = ab.astype(BF16).astype(F32)
    ab_lo = ab - ab_hi
    zk = jnp.zeros((t, LANES), F32)
    bias1 = jnp.where(lane == HEAD_DIM, ab_hi, jnp.where(lane == HEAD_DIM + 1, ab_lo, zk)).astype(BF16)
    bias2 = jnp.where(lane == 0, ab_hi, jnp.where(lane == 1, ab_lo, zk)).astype(BF16)
    lane_lo = lane < HEAD_DIM

    rowq = lax.broadcasted_iota(jnp.int32, (LANES, t), 0)
    one = jnp.ones((LANES, t), F32)
    zq = jnp.zeros((LANES, t), F32)
    for g in range(NG_DIFF):
        qt = qt_ref[0, :, g * t:(g + 1) * t].astype(F32)
        qa_ref[2 * g] = jnp.where(rowq < HEAD_DIM, qt,
                                  jnp.where(rowq < HEAD_DIM + 2, one, zq)).astype(BF16)
        qa_ref[2 * g + 1] = jnp.where(rowq >= HEAD_DIM, qt,
                                      jnp.where(rowq < 2, one, zq)).astype(BF16)
    m_ref[...] = jnp.full(m_ref.shape, NEG_BIG, F32)
    l_ref[...] = jnp.zeros(l_ref.shape, F32)
    acc_ref[...] = jnp.zeros(acc_ref.shape, F32)

    krow = lax.broadcasted_iota(jnp.int32, (t, t), 0)
    qcol = lax.broadcasted_iota(jnp.int32, (t, t), 1)
    causal = krow <= qcol

    def scores(c, ka):
        s_ref[c] = jnp.dot(ka, qa_ref[c], preferred_element_type=F32)

    def consume(c, vt, cj, masked):
        s = s_ref[c]
        if masked:
            s = jnp.where(causal, s, NEG_BIG)
        m_old = m_ref[c]
        m_new = jnp.maximum(m_old, jnp.max(s, axis=0, keepdims=True) + cj)
        alpha = jnp.exp2(m_old - m_new)
        p = jnp.exp2(s - (m_new - cj))
        l_ref[c] = alpha * l_ref[c] + jnp.sum(p, axis=0, keepdims=True)
        acc_ref[c] = alpha * acc_ref[c] + jnp.dot(vt, p.astype(BF16), preferred_element_type=F32)
        m_ref[c] = m_new

    def keys(j):
        start = pl.multiple_of(j * t, t)
        k = k_ref[0, pl.ds(start, t), :]
        return jnp.where(lane_lo, k, bias1), jnp.where(lane_lo, bias2, k)

    def values(j):
        return vt_ref[0, j], slope * (j * t).astype(F32)

    ka0 = keys(0)
    for c in range(2 * NG_DIFF):
        scores(c, ka0[c % 2])

    def body(j, carry):
        vt, cj = values(j)
        ka = keys(j + 1)
        for c in range(2 * NG_DIFF):
            consume(c, vt, cj, False)
            scores(c, ka[c % 2])
        return carry

    lax.fori_loop(0, i * NG_DIFF, body, 0)
    for jj in range(NG_DIFF):
        blk = i * NG_DIFF + jj
        vt, cj = values(blk)
        ka = keys(blk + 1) if jj + 1 < NG_DIFF else None
        for g in range(jj, NG_DIFF):
            for c in (2 * g, 2 * g + 1):
                consume(c, vt, cj, g == jj)
                if g > jj:
                    scores(c, ka[c % 2])

    lam = (jnp.exp(jnp.sum(lq1_ref[...] * lk1_ref[...], axis=-1, keepdims=True))
           - jnp.exp(jnp.sum(lq2_ref[...] * lk2_ref[...], axis=-1, keepdims=True))
           + lambda_init)
    for g in range(NG_DIFF):
        ot = acc_ref[2 * g] / l_ref[2 * g] - lam * (acc_ref[2 * g + 1] / l_ref[2 * g + 1])
        ot = ot * lax.rsqrt(jnp.mean(ot * ot, axis=0, keepdims=True) + SUBLN_EPS) * subg_ref[...]
        ot = ot * (1.0 - lambda_init)
        rows = slice(g * t, (g + 1) * t)
        o_ref[0, rows, :] = (ot.T * sg_ref[0, rows, :].astype(F32)).astype(BF16)


def _diff(slopes, lq1, lk1, lq2, lk2, subg, bqt, bk, bvt, sg, lambda_init):
    b, s, _ = bk.shape
    t = T_DIFF
    tq = NG_DIFF * t
    nchain = 2 * NG_DIFF
    smem = pl.BlockSpec(memory_space=pltpu.SMEM)
    small = lambda shape: pl.BlockSpec(shape, lambda bi, h, i: (0, 0))
    return pl.pallas_call(
        functools.partial(_diff_kernel, lambda_init=lambda_init),
        grid=(b, DIFF_HEADS, s // tq),
        in_specs=[smem,
                  small((1, HEAD_DIM)), small((1, HEAD_DIM)),
                  small((1, HEAD_DIM)), small((1, HEAD_DIM)),
                  small((LANES, 1)),
                  pl.BlockSpec((1, LANES, tq), lambda bi, h, i: (bi, h, i)),
                  pl.BlockSpec((1, s, LANES), lambda bi, h, i: (bi, 0, h)),
                  pl.BlockSpec((1, s // t, LANES, t), lambda bi, h, i: (bi, 0, h, 0)),
                  pl.BlockSpec((1, tq, LANES), lambda bi, h, i: (bi, i, A_Q // LANES + h))],
        out_specs=pl.BlockSpec((1, tq, LANES), lambda bi, h, i: (bi, i, h)),
        out_shape=jax.ShapeDtypeStruct((b, s, B_W), BF16),
        scratch_shapes=[pltpu.VMEM((nchain, LANES, t), BF16),
                        pltpu.VMEM((nchain, t, t), F32),
                        pltpu.VMEM((nchain, 1, t), F32),
                        pltpu.VMEM((nchain, 1, t), F32),
                        pltpu.VMEM((nchain, LANES, t), F32)],
        compiler_params=pltpu.CompilerParams(
            dimension_semantics=("arbitrary", "arbitrary", "arbitrary"),
            vmem_limit_bytes=VMEM_LIMIT),
        name="diffattn",
    )(slopes, lq1, lk1, lq2, lk2, subg, bqt, bk, bvt, sg)


def _outproj_kernel(x_ref, ma_ref, mb_ref, w_ref, fg_ref, o_ref, *, final_norm):
    y = (x_ref[...]
         + jnp.dot(ma_ref[...], w_ref[0:A_Q, :], preferred_element_type=F32)
         + jnp.dot(mb_ref[...], w_ref[A_Q:MIX, :], preferred_element_type=F32))
    if final_norm:
        ms = jnp.mean(y * y, axis=-1, keepdims=True)
        y = y * lax.rsqrt(ms + RMS_EPS) * fg_ref[...]
    o_ref[...] = y


def _outproj(xf, ma, mb, w, fg, final_norm):
    n, d = xf.shape
    tm = TM_PROJ
    row = lambda i: (i, 0)
    whole = lambda i: (0, 0)
    return pl.pallas_call(
        functools.partial(_outproj_kernel, final_norm=final_norm),
        grid=(n // tm,),
        in_specs=[pl.BlockSpec((tm, d), row),
                  pl.BlockSpec((tm, A_Q), row),
                  pl.BlockSpec((tm, B_W), row),
                  pl.BlockSpec((MIX, d), whole),
                  pl.BlockSpec((1, d), whole)],
        out_specs=pl.BlockSpec((tm, d), row),
        out_shape=jax.ShapeDtypeStruct((n, d), F32),
        compiler_params=pltpu.CompilerParams(
            dimension_semantics=("arbitrary",), vmem_limit_bytes=VMEM_LIMIT),
        name="outproj",
    )(xf, ma, mb, w, fg)


def kernel(x, norm_g, w_in, sinks, lambda_q1, lambda_k1, lambda_q2, lambda_k2,
           subln_g, w_out, final_g):
    b, s, d = x.shape
    depth = norm_g.shape[0]
    n = b * s
    diff_slopes = jnp.asarray(
        [2.0 ** (-8.0 * (h + 1) / DIFF_HEADS) for h in range(DIFF_HEADS)], F32)
    h3 = x
    for layer in range(depth):
        w = w_in[layer]
        aq, akd, avd, bqt, bk, bvt, sg = _inproj(
            h3, norm_g[layer].reshape(1, d), w.astype(BF16),
            w[:, OFF_BQ:OFF_BK].T.astype(BF16), w[:, OFF_BV:OFF_GATE].T.astype(BF16))
        mixed_a = _swa(sinks[layer], aq, akd, avd, sg)
        mixed_b = _diff(diff_slopes,
                        lambda_q1[layer].reshape(1, HEAD_DIM), lambda_k1[layer].reshape(1, HEAD_DIM),
                        lambda_q2[layer].reshape(1, HEAD_DIM), lambda_k2[layer].reshape(1, HEAD_DIM),
                        subln_g[layer].reshape(LANES, 1),
                        bqt, bk, bvt, sg, _lambda_init(layer))
        hf = _outproj(h3.reshape(n, d), mixed_a.reshape(n, A_Q), mixed_b.reshape(n, B_W),
                      w_out[layer].astype(BF16), final_g.reshape(1, d),
                      final_norm=(layer == depth - 1))
        h3 = hf.reshape(b, s, d)
    return h3
```

```python
import functools
import math

import jax
import jax.numpy as jnp
from jax import lax
from jax.experimental import pallas as pl
from jax.experimental.pallas import tpu as pltpu

F32 = jnp.float32
BF16 = jnp.bfloat16

HEAD_DIM = 64
LANES = 128
SWA_Q_HEADS = 8
SWA_KV_HEADS = 2
WINDOW = 128
DIFF_HEADS = 4
RMS_EPS = 1e-6
SUBLN_EPS = 1e-5
NEG_BIG = -1e30
LOG2E = math.log2(math.e)

A_Q = SWA_Q_HEADS * HEAD_DIM
A_KV = SWA_KV_HEADS * HEAD_DIM
B_W = DIFF_HEADS * 2 * HEAD_DIM
MIX = A_Q + B_W
OFF_AK = A_Q
OFF_AV = OFF_AK + A_KV
OFF_BQ = OFF_AV + A_KV
OFF_BK = OFF_BQ + B_W
OFF_BV = OFF_BK + B_W
OFF_GATE = OFF_BV + B_W
IN_COLS = OFF_GATE + MIX

TM_PROJ = 512
TQ_SWA = 512
TG_DIFF = 256
NG_DIFF = 4
TK_DIFF = 512
VROWS = LANES + 16
VMEM_LIMIT = 56 * 1024 * 1024


def _lambda_init(layer_idx):
    return 0.8 - 0.6 * math.exp(-0.3 * layer_idx)


def _nt_dot(a, b):
    return lax.dot_general(a, b, (((1,), (1,)), ((), ())), preferred_element_type=F32)


def _inproj_kernel(x_ref, g_ref, w_ref, wqt_ref, wvt_ref,
                   aq_ref, ak_ref, av_ref, bqt_ref, bk_ref, bvt_ref, sg_ref):
    x = x_ref[0]
    ms = jnp.mean(x * x, axis=-1, keepdims=True)
    h = (x * lax.rsqrt(ms + RMS_EPS) * g_ref[...]).astype(BF16)
    scale = HEAD_DIM ** -0.5

    def proj(lo, hi):
        return jnp.dot(h, w_ref[:, lo:hi], preferred_element_type=F32)

    def dup_heads(t, out_ref):
        lane = lax.broadcasted_iota(jnp.int32, t.shape, 1)
        lo = lane < HEAD_DIM
        r = pltpu.roll(t, HEAD_DIM, axis=1)
        out_ref[0, :, 0:LANES] = jnp.where(lo, t, r).astype(BF16)
        out_ref[0, :, LANES:2 * LANES] = jnp.where(lo, r, t).astype(BF16)

    aq_ref[0] = (proj(0, OFF_AK) * scale).astype(BF16)
    dup_heads(proj(OFF_AK, OFF_AV), ak_ref)
    dup_heads(proj(OFF_AV, OFF_BQ), av_ref)
    bqt_ref[0] = (_nt_dot(wqt_ref[...], h) * (scale * LOG2E)).astype(BF16)
    bk_ref[0] = proj(OFF_BK, OFF_BV).astype(BF16)
    bvt = _nt_dot(wvt_ref[...], h).astype(BF16)
    ones = jnp.ones((VROWS - LANES, TK_DIFF), BF16)
    for c in range(TM_PROJ // TK_DIFF):
        for hd in range(DIFF_HEADS):
            bvt_ref[0, c, hd * VROWS:hd * VROWS + LANES, :] = (
                bvt[hd * LANES:(hd + 1) * LANES, c * TK_DIFF:(c + 1) * TK_DIFF])
            bvt_ref[0, c, hd * VROWS + LANES:(hd + 1) * VROWS, :] = ones
    gate = proj(OFF_GATE, IN_COLS)
    sg_ref[0] = (gate * (1.0 / (1.0 + jnp.exp(-gate)))).astype(BF16)


def _inproj(x, g, w, wqt, wvt):
    b, s, d = x.shape
    tm = TM_PROJ
    row = lambda bi, i: (bi, i, 0)
    whole = lambda bi, i: (0, 0)
    bsd = lambda wd: jax.ShapeDtypeStruct((b, s, wd), BF16)
    return pl.pallas_call(
        _inproj_kernel,
        grid=(b, s // tm),
        in_specs=[pl.BlockSpec((1, tm, d), row),
                  pl.BlockSpec((1, d), whole),
                  pl.BlockSpec((d, IN_COLS), whole),
                  pl.BlockSpec((B_W, d), whole),
                  pl.BlockSpec((B_W, d), whole)],
        out_specs=[pl.BlockSpec((1, tm, A_Q), row),
                   pl.BlockSpec((1, tm, 2 * LANES), row),
                   pl.BlockSpec((1, tm, 2 * LANES), row),
                   pl.BlockSpec((1, B_W, tm), lambda bi, i: (bi, 0, i)),
                   pl.BlockSpec((1, tm, B_W), row),
                   pl.BlockSpec((1, tm // TK_DIFF, DIFF_HEADS * VROWS, TK_DIFF), lambda bi, i: (bi, i, 0, 0)),
                   pl.BlockSpec((1, tm, MIX), row)],
        out_shape=[bsd(A_Q), bsd(2 * LANES), bsd(2 * LANES),
                   jax.ShapeDtypeStruct((b, B_W, s), BF16),
                   bsd(B_W),
                   jax.ShapeDtypeStruct((b, s // TK_DIFF, DIFF_HEADS * VROWS, TK_DIFF), BF16),
                   bsd(MIX)],
        compiler_params=pltpu.CompilerParams(
            dimension_semantics=("arbitrary", "arbitrary"), vmem_limit_bytes=VMEM_LIMIT),
        name="inproj",
    )(x, g, w, wqt, wvt)


def _swa_kernel(sinks_ref, q_ref, kp_ref, kc_ref, vp_ref, vc_ref, sg_ref, o_ref, kcat, vcat):
    i = pl.program_id(1)
    kcat[0:WINDOW, :] = kp_ref[0]
    kcat[WINDOW:, :] = kc_ref[0]
    vcat[0:WINDOW, :] = vp_ref[0]
    vcat[WINDOW:, :] = vc_ref[0]

    r = lax.broadcasted_iota(jnp.int32, (WINDOW, 2 * WINDOW), 0)
    c = lax.broadcasted_iota(jnp.int32, (WINDOW, 2 * WINDOW), 1)
    dist = r + WINDOW - c
    distf = dist.astype(F32)
    band = (dist >= 0) & (dist < WINDOW)
    first_band = band & ((i > 0) | (c >= WINDOW))
    lane_lo = lax.broadcasted_iota(jnp.int32, (WINDOW, LANES), 1) < HEAD_DIM
    group = SWA_Q_HEADS // SWA_KV_HEADS

    for w in range(TQ_SWA // WINDOW):
        rows = slice(w * WINDOW, (w + 1) * WINDOW)
        krows = slice(w * WINDOW, (w + 2) * WINDOW)
        valid = first_band if w == 0 else band
        for cg in range(A_Q // LANES):
            cols = slice(cg * LANES, (cg + 1) * LANES)
            g = (2 * cg) // group
            gcols = slice(g * LANES, (g + 1) * LANES)
            qc = q_ref[0, rows, cols]
            kw = kcat[krows, gcols]
            vw = vcat[krows, gcols]
            outs = []
            for half in range(2):
                hh = 2 * cg + half
                slope = 2.0 ** (-8.0 * (hh + 1) / SWA_Q_HEADS)
                qh = jnp.where(lane_lo if half == 0 else ~lane_lo, qc, jnp.zeros_like(qc))
                s = _nt_dot(qh, kw) - slope * distf
                s = jnp.where(valid, s, NEG_BIG)
                sink = sinks_ref[hh]
                m = jnp.maximum(jnp.max(s, axis=-1, keepdims=True), sink)
                p = jnp.exp(s - m)
                denom = jnp.sum(p, axis=-1, keepdims=True) + jnp.exp(sink - m)
                o = jnp.dot(p.astype(BF16), vw, preferred_element_type=F32)
                outs.append(o / denom)
            o = jnp.where(lane_lo, outs[0], outs[1])
            o_ref[0, rows, cols] = (o * sg_ref[0, rows, cols].astype(F32)).astype(BF16)


def _swa(sinks, aq, akd, avd, sg):
    b, s, _ = aq.shape
    tq = TQ_SWA
    wpt = tq // WINDOW
    cur = lambda bi, i: (bi, i, 0)
    prev = lambda bi, i: (bi, jnp.maximum(i * wpt - 1, 0), 0)
    return pl.pallas_call(
        _swa_kernel,
        grid=(b, s // tq),
        in_specs=[pl.BlockSpec(memory_space=pltpu.SMEM),
                  pl.BlockSpec((1, tq, A_Q), cur),
                  pl.BlockSpec((1, WINDOW, 2 * LANES), prev),
                  pl.BlockSpec((1, tq, 2 * LANES), cur),
                  pl.BlockSpec((1, WINDOW, 2 * LANES), prev),
                  pl.BlockSpec((1, tq, 2 * LANES), cur),
                  pl.BlockSpec((1, tq, A_Q), cur)],
        out_specs=pl.BlockSpec((1, tq, A_Q), cur),
        out_shape=jax.ShapeDtypeStruct((b, s, A_Q), BF16),
        scratch_shapes=[pltpu.VMEM((tq + WINDOW, 2 * LANES), BF16),
                        pltpu.VMEM((tq + WINDOW, 2 * LANES), BF16)],
        compiler_params=pltpu.CompilerParams(
            dimension_semantics=("arbitrary", "arbitrary"), vmem_limit_bytes=VMEM_LIMIT),
        name="swa",
    )(sinks, aq, akd, akd, avd, avd, sg)


def _diff_kernel(slopes_ref, lq1_ref, lk1_ref, lq2_ref, lk2_ref, subg_ref,
                 qt_ref, k_ref, vt_ref, sg_ref, o_ref,
                 qa_ref, s_ref, m_ref, acc_ref, *, lambda_init):
    tg, tk = TG_DIFF, TK_DIFF
    nchain = 2 * NG_DIFF
    hd = pl.program_id(1)
    i = pl.program_id(2)
    slope = slopes_ref[hd] * LOG2E

    def key_consts(n):
        kk = lax.broadcasted_iota(jnp.int32, (n, LANES), 0).astype(F32)
        lane = lax.broadcasted_iota(jnp.int32, (n, LANES), 1)
        ab = slope * kk
        ab_hi = ab.astype(BF16).astype(F32)
        ab_lo = ab - ab_hi
        zk = jnp.zeros((n, LANES), F32)
        bias1 = jnp.where(lane == HEAD_DIM, ab_hi,
                          jnp.where(lane == HEAD_DIM + 1, ab_lo, zk)).astype(BF16)
        bias2 = jnp.where(lane == 0, ab_hi, jnp.where(lane == 1, ab_lo, zk)).astype(BF16)
        return lane < HEAD_DIM, bias1, bias2

    kconst = {n: key_consts(n) for n in (tg, tk)}

    rowq = lax.broadcasted_iota(jnp.int32, (LANES, tg), 0)
    one = jnp.ones((LANES, tg), F32)
    zq = jnp.zeros((LANES, tg), F32)
    for g in range(NG_DIFF):
        qt = qt_ref[0, :, g * tg:(g + 1) * tg].astype(F32)
        qa_ref[2 * g] = jnp.where(rowq < HEAD_DIM, qt,
                                  jnp.where(rowq < HEAD_DIM + 2, one, zq)).astype(BF16)
        qa_ref[2 * g + 1] = jnp.where(rowq >= HEAD_DIM, qt,
                                      jnp.where(rowq < 2, one, zq)).astype(BF16)
    m_ref[...] = jnp.full(m_ref.shape, NEG_BIG, F32)
    acc_ref[...] = jnp.zeros(acc_ref.shape, F32)

    krow = lax.broadcasted_iota(jnp.int32, (tg, tg), 0)
    qcol = lax.broadcasted_iota(jnp.int32, (tg, tg), 1)
    causal = krow <= qcol

    def scores(c, ka, n):
        s_ref[c, 0:n, :] = jnp.dot(ka, qa_ref[c], preferred_element_type=F32)

    def consume(c, vt, cj, n, masked):
        s = s_ref[c, 0:n, :]
        if masked:
            s = jnp.where(causal, s, NEG_BIG)
        m_old = m_ref[c]
        m_new = jnp.maximum(m_old, jnp.max(s, axis=0, keepdims=True) + cj)
        alpha = jnp.exp2(m_old - m_new)
        p = jnp.exp2(s - (m_new - cj)).astype(BF16)
        acc_ref[c] = alpha * acc_ref[c] + jnp.dot(vt, p, preferred_element_type=F32)
        m_ref[c] = m_new

    def keys(start, n):
        k = k_ref[0, pl.ds(pl.multiple_of(start, n), n), :]
        lane_lo, bias1, bias2 = kconst[n]
        return jnp.where(lane_lo, k, bias1), jnp.where(lane_lo, bias2, k)

    def block_bias(start):
        return slope * start.astype(F32)

    base = i * (NG_DIFF * tg)
    ka = keys(base, tg)
    for c in range(nchain):
        scores(c, ka[c % 2], tg)
    for jj in range(NG_DIFF):
        start = base + jj * tg
        vt = vt_ref[0, i * (NG_DIFF * tg // tk) + (jj * tg) // tk, :,
                    (jj * tg) % tk:(jj * tg) % tk + tg]
        cj = block_bias(start)
        ka = keys(start + tg, tg) if jj + 1 < NG_DIFF else None
        for g in range(jj, NG_DIFF):
            for c in (2 * g, 2 * g + 1):
                consume(c, vt, cj, tg, g == jj)
                if g > jj:
                    scores(c, ka[c % 2], tg)

    nblk = i * (NG_DIFF * tg // tk)

    @pl.when(i > 0)
    def _():
        ka0 = keys(jnp.int32(0), tk)
        for c in range(nchain):
            scores(c, ka0[c % 2], tk)

        def body(j, carry):
            start = j * tk
            vt = vt_ref[0, j]
            cj = block_bias(start)
            kan = keys(start + tk, tk)
            for c in range(nchain):
                consume(c, vt, cj, tk, False)
                scores(c, kan[c % 2], tk)
            return carry

        lax.fori_loop(0, nblk - 1, body, 0)
        last = nblk - 1
        vt = vt_ref[0, last]
        cj = block_bias(last * tk)
        for c in range(nchain):
            consume(c, vt, cj, tk, False)

    lam = (jnp.exp(jnp.sum(lq1_ref[...] * lk1_ref[...], axis=-1, keepdims=True))
           - jnp.exp(jnp.sum(lq2_ref[...] * lk2_ref[...], axis=-1, keepdims=True))
           + lambda_init)
    for g in range(NG_DIFF):
        a1 = acc_ref[2 * g]
        a2 = acc_ref[2 * g + 1]
        ot = (a1[0:LANES] / a1[LANES:LANES + 1]
              - lam * (a2[0:LANES] / a2[LANES:LANES + 1]))
        ot = ot * lax.rsqrt(jnp.mean(ot * ot, axis=0, keepdims=True) + SUBLN_EPS) * subg_ref[...]
        ot = ot * (1.0 - lambda_init)
        rows = slice(g * tg, (g + 1) * tg)
        o_ref[0, rows, :] = (ot.T * sg_ref[0, rows, :].astype(F32)).astype(BF16)


def _diff(slopes, lq1, lk1, lq2, lk2, subg, bqt, bk, bvt, sg, lambda_init):
    b, s, _ = bk.shape
    tg, tk = TG_DIFF, TK_DIFF
    tq = NG_DIFF * tg
    nchain = 2 * NG_DIFF
    smem = pl.BlockSpec(memory_space=pltpu.SMEM)
    small = lambda shape: pl.BlockSpec(shape, lambda bi, h, i: (0, 0))
    return pl.pallas_call(
        functools.partial(_diff_kernel, lambda_init=lambda_init),
        grid=(b, DIFF_HEADS, s // tq),
        in_specs=[smem,
                  small((1, HEAD_DIM)), small((1, HEAD_DIM)),
                  small((1, HEAD_DIM)), small((1, HEAD_DIM)),
                  small((LANES, 1)),
                  pl.BlockSpec((1, LANES, tq), lambda bi, h, i: (bi, h, i)),
                  pl.BlockSpec((1, s, LANES), lambda bi, h, i: (bi, 0, h)),
                  pl.BlockSpec((1, s // tk, VROWS, tk), lambda bi, h, i: (bi, 0, h, 0)),
                  pl.BlockSpec((1, tq, LANES), lambda bi, h, i: (bi, i, A_Q // LANES + h))],
        out_specs=pl.BlockSpec((1, tq, LANES), lambda bi, h, i: (bi, i, h)),
        out_shape=jax.ShapeDtypeStruct((b, s, B_W), BF16),
        scratch_shapes=[pltpu.VMEM((nchain, LANES, tg), BF16),
                        pltpu.VMEM((nchain, tk, tg), F32),
                        pltpu.VMEM((nchain, 1, tg), F32),
                        pltpu.VMEM((nchain, VROWS, tg), F32)],
        compiler_params=pltpu.CompilerParams(
            dimension_semantics=("arbitrary", "arbitrary", "arbitrary"),
            vmem_limit_bytes=VMEM_LIMIT),
        name="diffattn",
    )(slopes, lq1, lk1, lq2, lk2, subg, bqt, bk, bvt, sg)


def _outproj_kernel(x_ref, ma_ref, mb_ref, w_ref, fg_ref, o_ref, *, final_norm):
    y = (x_ref[...]
         + jnp.dot(ma_ref[...], w_ref[0:A_Q, :], preferred_element_type=F32)
         + jnp.dot(mb_ref[...], w_ref[A_Q:MIX, :], preferred_element_type=F32))
    if final_norm:
        ms = jnp.mean(y * y, axis=-1, keepdims=True)
        y = y * lax.rsqrt(ms + RMS_EPS) * fg_ref[...]
    o_ref[...] = y


def _outproj(xf, ma, mb, w, fg, final_norm):
    n, d = xf.shape
    tm = TM_PROJ
    row = lambda i: (i, 0)
    whole = lambda i: (0, 0)
    return pl.pallas_call(
        functools.partial(_outproj_kernel, final_norm=final_norm),
        grid=(n // tm,),
        in_specs=[pl.BlockSpec((tm, d), row),
                  pl.BlockSpec((tm, A_Q), row),
                  pl.BlockSpec((tm, B_W), row),
                  pl.BlockSpec((MIX, d), whole),
                  pl.BlockSpec((1, d), whole)],
        out_specs=pl.BlockSpec((tm, d), row),
        out_shape=jax.ShapeDtypeStruct((n, d), F32),
        compiler_params=pltpu.CompilerParams(
            dimension_semantics=("arbitrary",), vmem_limit_bytes=VMEM_LIMIT),
        name="outproj",
    )(xf, ma, mb, w, fg)


def kernel(x, norm_g, w_in, sinks, lambda_q1, lambda_k1, lambda_q2, lambda_k2,
           subln_g, w_out, final_g):
    b, s, d = x.shape
    depth = norm_g.shape[0]
    n = b * s
    diff_slopes = jnp.asarray(
        [2.0 ** (-8.0 * (h + 1) / DIFF_HEADS) for h in range(DIFF_HEADS)], F32)
    h3 = x
    for layer in range(depth):
        w = w_in[layer]
        aq, akd, avd, bqt, bk, bvt, sg = _inproj(
            h3, norm_g[layer].reshape(1, d), w.astype(BF16),
            w[:, OFF_BQ:OFF_BK].T.astype(BF16), w[:, OFF_BV:OFF_GATE].T.astype(BF16))
        mixed_a = _swa(sinks[layer], aq, akd, avd, sg)
        mixed_b = _diff(diff_slopes,
                        lambda_q1[layer].reshape(1, HEAD_DIM), lambda_k1[layer].reshape(1, HEAD_DIM),
                        lambda_q2[layer].reshape(1, HEAD_DIM), lambda_k2[layer].reshape(1, HEAD_DIM),
                        subln_g[layer].reshape(LANES, 1),
                        bqt, bk, bvt, sg, _lambda_init(layer))
        hf = _outproj(h3.reshape(n, d), mixed_a.reshape(n, A_Q), mixed_b.reshape(n, B_W),
                      w_out[layer].astype(BF16), final_g.reshape(1, d),
                      final_norm=(layer == depth - 1))
        h3 = hf.reshape(b, s, d)
    return h3
```

```python
import functools
import math

import jax
import jax.numpy as jnp
from jax import lax
from jax.experimental import pallas as pl
from jax.experimental.pallas import tpu as pltpu

F32 = jnp.float32
BF16 = jnp.bfloat16

HEAD_DIM = 64
LANES = 128
SWA_Q_HEADS = 8
SWA_KV_HEADS = 2
WINDOW = 128
DIFF_HEADS = 4
RMS_EPS = 1e-6
SUBLN_EPS = 1e-5
NEG_BIG = -1e30
LOG2E = math.log2(math.e)

A_Q = SWA_Q_HEADS * HEAD_DIM
A_KV = SWA_KV_HEADS * HEAD_DIM
B_W = DIFF_HEADS * 2 * HEAD_DIM
MIX = A_Q + B_W
OFF_AK = A_Q
OFF_AV = OFF_AK + A_KV
OFF_BQ = OFF_AV + A_KV
OFF_BK = OFF_BQ + B_W
OFF_BV = OFF_BK + B_W
OFF_GATE = OFF_BV + B_W
IN_COLS = OFF_GATE + MIX

TM_PROJ = 512
TQ_SWA = 512
TG_DIFF = 256
NG_DIFF = 4
TK_DIFF = 512
VROWS = LANES + 16
VMEM_LIMIT = 56 * 1024 * 1024


def _lambda_init(layer_idx):
    return 0.8 - 0.6 * math.exp(-0.3 * layer_idx)


def _nt_dot(a, b):
    return lax.dot_general(a, b, (((1,), (1,)), ((), ())), preferred_element_type=F32)


def _inproj_kernel(x_ref, g_ref, w_ref, wqt_ref, wvt_ref,
                   aq_ref, ak_ref, av_ref, bqt_ref, bk_ref, bvt_ref, sg_ref):
    x = x_ref[0]
    ms = jnp.mean(x * x, axis=-1, keepdims=True)
    h = (x * lax.rsqrt(ms + RMS_EPS) * g_ref[...]).astype(BF16)
    scale = HEAD_DIM ** -0.5

    def proj(lo, hi):
        return jnp.dot(h, w_ref[:, lo:hi], preferred_element_type=F32)

    def dup_heads(t, out_ref):
        lane = lax.broadcasted_iota(jnp.int32, t.shape, 1)
        lo = lane < HEAD_DIM
        r = pltpu.roll(t, HEAD_DIM, axis=1)
        out_ref[0, :, 0:LANES] = jnp.where(lo, t, r).astype(BF16)
        out_ref[0, :, LANES:2 * LANES] = jnp.where(lo, r, t).astype(BF16)

    aq_ref[0] = (proj(0, OFF_AK) * scale).astype(BF16)
    dup_heads(proj(OFF_AK, OFF_AV), ak_ref)
    dup_heads(proj(OFF_AV, OFF_BQ), av_ref)
    bqt_ref[0] = (_nt_dot(wqt_ref[...], h) * (scale * LOG2E)).astype(BF16)
    bk_ref[0] = proj(OFF_BK, OFF_BV).astype(BF16)
    bvt = _nt_dot(wvt_ref[...], h).astype(BF16)
    ones = jnp.ones((VROWS - LANES, TK_DIFF), BF16)
    for c in range(TM_PROJ // TK_DIFF):
        for hd in range(DIFF_HEADS):
            bvt_ref[0, c, hd * VROWS:hd * VROWS + LANES, :] = (
                bvt[hd * LANES:(hd + 1) * LANES, c * TK_DIFF:(c + 1) * TK_DIFF])
            bvt_ref[0, c, hd * VROWS + LANES:(hd + 1) * VROWS, :] = ones
    gate = proj(OFF_GATE, IN_COLS)
    sg_ref[0] = (gate * (1.0 / (1.0 + jnp.exp(-gate)))).astype(BF16)


def _inproj(x, g, w, wqt, wvt):
    b, s, d = x.shape
    tm = TM_PROJ
    row = lambda bi, i: (bi, i, 0)
    whole = lambda bi, i: (0, 0)
    bsd = lambda wd: jax.ShapeDtypeStruct((b, s, wd), BF16)
    return pl.pallas_call(
        _inproj_kernel,
        grid=(b, s // tm),
        in_specs=[pl.BlockSpec((1, tm, d), row),
                  pl.BlockSpec((1, d), whole),
                  pl.BlockSpec((d, IN_COLS), whole),
                  pl.BlockSpec((B_W, d), whole),
                  pl.BlockSpec((B_W, d), whole)],
        out_specs=[pl.BlockSpec((1, tm, A_Q), row),
                   pl.BlockSpec((1, tm, 2 * LANES), row),
                   pl.BlockSpec((1, tm, 2 * LANES), row),
                   pl.BlockSpec((1, B_W, tm), lambda bi, i: (bi, 0, i)),
                   pl.BlockSpec((1, tm, B_W), row),
                   pl.BlockSpec((1, tm // TK_DIFF, DIFF_HEADS * VROWS, TK_DIFF), lambda bi, i: (bi, i, 0, 0)),
                   pl.BlockSpec((1, tm, MIX), row)],
        out_shape=[bsd(A_Q), bsd(2 * LANES), bsd(2 * LANES),
                   jax.ShapeDtypeStruct((b, B_W, s), BF16),
                   bsd(B_W),
                   jax.ShapeDtypeStruct((b, s // TK_DIFF, DIFF_HEADS * VROWS, TK_DIFF), BF16),
                   bsd(MIX)],
        compiler_params=pltpu.CompilerParams(
            dimension_semantics=("arbitrary", "arbitrary"), vmem_limit_bytes=VMEM_LIMIT),
        name="inproj",
    )(x, g, w, wqt, wvt)


def _swa_kernel(sinks_ref, q_ref, kp_ref, kc_ref, vp_ref, vc_ref, sg_ref, o_ref, kcat, vcat):
    i = pl.program_id(1)
    kcat[0:WINDOW, :] = kp_ref[0]
    kcat[WINDOW:, :] = kc_ref[0]
    vcat[0:WINDOW, :] = vp_ref[0]
    vcat[WINDOW:, :] = vc_ref[0]

    r = lax.broadcasted_iota(jnp.int32, (WINDOW, 2 * WINDOW), 0)
    c = lax.broadcasted_iota(jnp.int32, (WINDOW, 2 * WINDOW), 1)
    dist = r + WINDOW - c
    distf = dist.astype(F32)
    band = (dist >= 0) & (dist < WINDOW)
    first_band = band & ((i > 0) | (c >= WINDOW))
    lane_lo = lax.broadcasted_iota(jnp.int32, (WINDOW, LANES), 1) < HEAD_DIM
    group = SWA_Q_HEADS // SWA_KV_HEADS

    for w in range(TQ_SWA // WINDOW):
        rows = slice(w * WINDOW, (w + 1) * WINDOW)
        krows = slice(w * WINDOW, (w + 2) * WINDOW)
        valid = first_band if w == 0 else band
        for cg in range(A_Q // LANES):
            cols = slice(cg * LANES, (cg + 1) * LANES)
            g = (2 * cg) // group
            gcols = slice(g * LANES, (g + 1) * LANES)
            qc = q_ref[0, rows, cols]
            kw = kcat[krows, gcols]
            vw = vcat[krows, gcols]
            outs = []
            for half in range(2):
                hh = 2 * cg + half
                slope = 2.0 ** (-8.0 * (hh + 1) / SWA_Q_HEADS)
                qh = jnp.where(lane_lo if half == 0 else ~lane_lo, qc, jnp.zeros_like(qc))
                s = _nt_dot(qh, kw) - slope * distf
                s = jnp.where(valid, s, NEG_BIG)
                sink = sinks_ref[hh]
                m = jnp.maximum(jnp.max(s, axis=-1, keepdims=True), sink)
                p = jnp.exp(s - m)
                denom = jnp.sum(p, axis=-1, keepdims=True) + jnp.exp(sink - m)
                o = jnp.dot(p.astype(BF16), vw, preferred_element_type=F32)
                outs.append(o / denom)
            o = jnp.where(lane_lo, outs[0], outs[1])
            o_ref[0, rows, cols] = (o * sg_ref[0, rows, cols].astype(F32)).astype(BF16)


def _swa(sinks, aq, akd, avd, sg):
    b, s, _ = aq.shape
    tq = TQ_SWA
    wpt = tq // WINDOW
    cur = lambda bi, i: (bi, i, 0)
    prev = lambda bi, i: (bi, jnp.maximum(i * wpt - 1, 0), 0)
    return pl.pallas_call(
        _swa_kernel,
        grid=(b, s // tq),
        in_specs=[pl.BlockSpec(memory_space=pltpu.SMEM),
                  pl.BlockSpec((1, tq, A_Q), cur),
                  pl.BlockSpec((1, WINDOW, 2 * LANES), prev),
                  pl.BlockSpec((1, tq, 2 * LANES), cur),
                  pl.BlockSpec((1, WINDOW, 2 * LANES), prev),
                  pl.BlockSpec((1, tq, 2 * LANES), cur),
                  pl.BlockSpec((1, tq, A_Q), cur)],
        out_specs=pl.BlockSpec((1, tq, A_Q), cur),
        out_shape=jax.ShapeDtypeStruct((b, s, A_Q), BF16),
        scratch_shapes=[pltpu.VMEM((tq + WINDOW, 2 * LANES), BF16),
                        pltpu.VMEM((tq + WINDOW, 2 * LANES), BF16)],
        compiler_params=pltpu.CompilerParams(
            dimension_semantics=("arbitrary", "arbitrary"), vmem_limit_bytes=VMEM_LIMIT),
        name="swa",
    )(sinks, aq, akd, akd, avd, avd, sg)


def _diff_kernel(slopes_ref, lq1_ref, lk1_ref, lq2_ref, lk2_ref, subg_ref,
                 qt_ref, k_ref, vt_ref, sg_ref, o_ref,
                 qa_ref, s_ref, m_ref, acc_ref, *, lambda_init):
    tg, tk = TG_DIFF, TK_DIFF
    nchain = 2 * NG_DIFF
    hd = pl.program_id(1)
    i = pl.program_id(2)
    slope = slopes_ref[hd] * LOG2E

    def key_consts(n):
        kk = lax.broadcasted_iota(jnp.int32, (n, LANES), 0).astype(F32)
        lane = lax.broadcasted_iota(jnp.int32, (n, LANES), 1)
        ab = slope * kk
        ab_hi = ab.astype(BF16).astype(F32)
        ab_lo = ab - ab_hi
        zk = jnp.zeros((n, LANES), F32)
        bias1 = jnp.where(lane == HEAD_DIM, ab_hi,
                          jnp.where(lane == HEAD_DIM + 1, ab_lo, zk)).astype(BF16)
        bias2 = jnp.where(lane == 0, ab_hi, jnp.where(lane == 1, ab_lo, zk)).astype(BF16)
        return lane < HEAD_DIM, bias1, bias2

    kconst = {n: key_consts(n) for n in (tg, tk)}

    rowq = lax.broadcasted_iota(jnp.int32, (LANES, tg), 0)
    one = jnp.ones((LANES, tg), F32)
    zq = jnp.zeros((LANES, tg), F32)
    for g in range(NG_DIFF):
        qt = qt_ref[0, :, g * tg:(g + 1) * tg].astype(F32)
        qa_ref[2 * g] = jnp.where(rowq < HEAD_DIM, qt,
                                  jnp.where(rowq < HEAD_DIM + 2, one, zq)).astype(BF16)
        qa_ref[2 * g + 1] = jnp.where(rowq >= HEAD_DIM, qt,
                                      jnp.where(rowq < 2, one, zq)).astype(BF16)
    m_ref[...] = jnp.full(m_ref.shape, NEG_BIG, F32)
    acc_ref[...] = jnp.zeros(acc_ref.shape, F32)

    krow = lax.broadcasted_iota(jnp.int32, (tg, tg), 0)
    qcol = lax.broadcasted_iota(jnp.int32, (tg, tg), 1)
    causal = krow <= qcol

    def scores(buf, c, ka, n):
        s_ref[buf, c, 0:n, :] = jnp.dot(ka, qa_ref[c], preferred_element_type=F32)

    def consume(buf, c, vt, cj, n, masked):
        s = s_ref[buf, c, 0:n, :]
        if masked:
            s = jnp.where(causal, s, NEG_BIG)
        m_old = m_ref[c]
        m_new = jnp.maximum(m_old, jnp.max(s, axis=0, keepdims=True) + cj)
        alpha = jnp.exp2(m_old - m_new)
        p = jnp.exp2(s - (m_new - cj)).astype(BF16)
        acc_ref[c] = alpha * acc_ref[c] + jnp.dot(vt, p, preferred_element_type=F32)
        m_ref[c] = m_new

    def keys(start, n):
        k = k_ref[0, pl.ds(pl.multiple_of(start, n), n), :]
        lane_lo, bias1, bias2 = kconst[n]
        return jnp.where(lane_lo, k, bias1), jnp.where(lane_lo, bias2, k)

    def block_bias(start):
        return slope * start.astype(F32)

    base = i * (NG_DIFF * tg)
    ka = keys(base, tg)
    for c in range(nchain):
        scores(0, c, ka[c % 2], tg)
    ka_first = keys(jnp.int32(0), tk)
    for jj in range(NG_DIFF):
        cur, nxt = jj % 2, 1 - jj % 2
        start = base + jj * tg
        vt = vt_ref[0, i * (NG_DIFF * tg // tk) + (jj * tg) // tk, :,
                    (jj * tg) % tk:(jj * tg) % tk + tg]
        cj = block_bias(start)
        ka = keys(start + tg, tg) if jj + 1 < NG_DIFF else None
        for g in range(jj, NG_DIFF):
            for c in (2 * g, 2 * g + 1):
                if g > jj:
                    scores(nxt, c, ka[c % 2], tg)
                consume(cur, c, vt, cj, tg, g == jj)
                if g == jj:
                    scores(0, c, ka_first[c % 2], tk)

    lam = (jnp.exp(jnp.sum(lq1_ref[...] * lk1_ref[...], axis=-1, keepdims=True))
           - jnp.exp(jnp.sum(lq2_ref[...] * lk2_ref[...], axis=-1, keepdims=True))
           + lambda_init)

    def finish(g):
        a1 = acc_ref[2 * g]
        a2 = acc_ref[2 * g + 1]
        ot = (a1[0:LANES] / a1[LANES:LANES + 1]
              - lam * (a2[0:LANES] / a2[LANES:LANES + 1]))
        ot = ot * lax.rsqrt(jnp.mean(ot * ot, axis=0, keepdims=True) + SUBLN_EPS) * subg_ref[...]
        ot = ot * (1.0 - lambda_init)
        rows = slice(g * tg, (g + 1) * tg)
        o_ref[0, rows, :] = (ot.T * sg_ref[0, rows, :].astype(F32)).astype(BF16)

    nblk = i * (NG_DIFF * tg // tk)

    @pl.when(i > 0)
    def _():
        def step(j, cur, last):
            vt = vt_ref[0, j]
            cj = block_bias(j * tk)
            kan = None if last else keys((j + 1) * tk, tk)
            for g in range(NG_DIFF):
                for c in (2 * g, 2 * g + 1):
                    if not last:
                        scores(1 - cur, c, kan[c % 2], tk)
                    consume(cur, c, vt, cj, tk, False)
                if last:
                    finish(g)

        def body(t, carry):
            step(2 * t, 0, False)
            step(2 * t + 1, 1, False)
            return carry

        lax.fori_loop(0, nblk // 2 - 1, body, 0)
        step(nblk - 2, 0, False)
        step(nblk - 1, 1, True)

    @pl.when(i == 0)
    def _():
        for g in range(NG_DIFF):
            finish(g)


def _diff(slopes, lq1, lk1, lq2, lk2, subg, bqt, bk, bvt, sg, lambda_init):
    b, s, _ = bk.shape
    tg, tk = TG_DIFF, TK_DIFF
    tq = NG_DIFF * tg
    nchain = 2 * NG_DIFF
    smem = pl.BlockSpec(memory_space=pltpu.SMEM)
    small = lambda shape: pl.BlockSpec(shape, lambda bi, h, i: (0, 0))
    return pl.pallas_call(
        functools.partial(_diff_kernel, lambda_init=lambda_init),
        grid=(b, DIFF_HEADS, s // tq),
        in_specs=[smem,
                  small((1, HEAD_DIM)), small((1, HEAD_DIM)),
                  small((1, HEAD_DIM)), small((1, HEAD_DIM)),
                  small((LANES, 1)),
                  pl.BlockSpec((1, LANES, tq), lambda bi, h, i: (bi, h, i)),
                  pl.BlockSpec((1, s, LANES), lambda bi, h, i: (bi, 0, h)),
                  pl.BlockSpec((1, s // tk, VROWS, tk), lambda bi, h, i: (bi, 0, h, 0)),
                  pl.BlockSpec((1, tq, LANES), lambda bi, h, i: (bi, i, A_Q // LANES + h))],
        out_specs=pl.BlockSpec((1, tq, LANES), lambda bi, h, i: (bi, i, h)),
        out_shape=jax.ShapeDtypeStruct((b, s, B_W), BF16),
        scratch_shapes=[pltpu.VMEM((nchain, LANES, tg), BF16),
                        pltpu.VMEM((2, nchain, tk, tg), F32),
                        pltpu.VMEM((nchain, 1, tg), F32),
                        pltpu.VMEM((nchain, VROWS, tg), F32)],
        compiler_params=pltpu.CompilerParams(
            dimension_semantics=("arbitrary", "arbitrary", "arbitrary"),
            vmem_limit_bytes=VMEM_LIMIT),
        name="diffattn",
    )(slopes, lq1, lk1, lq2, lk2, subg, bqt, bk, bvt, sg)


def _outproj_kernel(x_ref, ma_ref, mb_ref, w_ref, fg_ref, o_ref, *, final_norm):
    y = (x_ref[...]
         + jnp.dot(ma_ref[...], w_ref[0:A_Q, :], preferred_element_type=F32)
         + jnp.dot(mb_ref[...], w_ref[A_Q:MIX, :], preferred_element_type=F32))
    if final_norm:
        ms = jnp.mean(y * y, axis=-1, keepdims=True)
        y = y * lax.rsqrt(ms + RMS_EPS) * fg_ref[...]
    o_ref[...] = y


def _outproj(xf, ma, mb, w, fg, final_norm):
    n, d = xf.shape
    tm = TM_PROJ
    row = lambda i: (i, 0)
    whole = lambda i: (0, 0)
    return pl.pallas_call(
        functools.partial(_outproj_kernel, final_norm=final_norm),
        grid=(n // tm,),
        in_specs=[pl.BlockSpec((tm, d), row),
                  pl.BlockSpec((tm, A_Q), row),
                  pl.BlockSpec((tm, B_W), row),
                  pl.BlockSpec((MIX, d), whole),
                  pl.BlockSpec((1, d), whole)],
        out_specs=pl.BlockSpec((tm, d), row),
        out_shape=jax.ShapeDtypeStruct((n, d), F32),
        compiler_params=pltpu.CompilerParams(
            dimension_semantics=("arbitrary",), vmem_limit_bytes=VMEM_LIMIT),
        name="outproj",
    )(xf, ma, mb, w, fg)


def kernel(x, norm_g, w_in, sinks, lambda_q1, lambda_k1, lambda_q2, lambda_k2,
           subln_g, w_out, final_g):
    b, s, d = x.shape
    depth = norm_g.shape[0]
    n = b * s
    diff_slopes = jnp.asarray(
        [2.0 ** (-8.0 * (h + 1) / DIFF_HEADS) for h in range(DIFF_HEADS)], F32)
    h3 = x
    for layer in range(depth):
        w = w_in[layer]
        aq, akd, avd, bqt, bk, bvt, sg = _inproj(
            h3, norm_g[layer].reshape(1, d), w.astype(BF16),
            w[:, OFF_BQ:OFF_BK].T.astype(BF16), w[:, OFF_BV:OFF_GATE].T.astype(BF16))
        mixed_a = _swa(sinks[layer], aq, akd, avd, sg)
        mixed_b = _diff(diff_slopes,
                        lambda_q1[layer].reshape(1, HEAD_DIM), lambda_k1[layer].reshape(1, HEAD_DIM),
                        lambda_q2[layer].reshape(1, HEAD_DIM), lambda_k2[layer].reshape(1, HEAD_DIM),
                        subln_g[layer].reshape(LANES, 1),
                        bqt, bk, bvt, sg, _lambda_init(layer))
        hf = _outproj(h3.reshape(n, d), mixed_a.reshape(n, A_Q), mixed_b.reshape(n, B_W),
                      w_out[layer].astype(BF16), final_g.reshape(1, d),
                      final_norm=(layer == depth - 1))
        h3 = hf.reshape(b, s, d)
    return h3
```

```python
import functools
import math

import jax
import jax.numpy as jnp
from jax import lax
from jax.experimental import pallas as pl
from jax.experimental.pallas import tpu as pltpu

F32 = jnp.float32
BF16 = jnp.bfloat16

HEAD_DIM = 64
LANES = 128
SWA_Q_HEADS = 8
SWA_KV_HEADS = 2
WINDOW = 128
DIFF_HEADS = 4
RMS_EPS = 1e-6
SUBLN_EPS = 1e-5
NEG_BIG = -1e30
LOG2E = math.log2(math.e)

A_Q = SWA_Q_HEADS * HEAD_DIM
A_KV = SWA_KV_HEADS * HEAD_DIM
B_W = DIFF_HEADS * 2 * HEAD_DIM
MIX = A_Q + B_W
OFF_AK = A_Q
OFF_AV = OFF_AK + A_KV
OFF_BQ = OFF_AV + A_KV
OFF_BK = OFF_BQ + B_W
OFF_BV = OFF_BK + B_W
OFF_GATE = OFF_BV + B_W
IN_COLS = OFF_GATE + MIX

TM_PROJ = 512
TQ_SWA = 512
TG_DIFF = 256
NG_DIFF = 4
TK_DIFF = 512
VROWS = LANES + 16
VMEM_LIMIT = 56 * 1024 * 1024


def _lambda_init(layer_idx):
    return 0.8 - 0.6 * math.exp(-0.3 * layer_idx)


def _nt_dot(a, b):
    return lax.dot_general(a, b, (((1,), (1,)), ((), ())), preferred_element_type=F32)


def _inproj_kernel(x_ref, g_ref, w_ref, wqt_ref, wvt_ref,
                   aq_ref, ak_ref, av_ref, bqt_ref, bk_ref, bvt_ref, sg_ref):
    x = x_ref[0]
    ms = jnp.mean(x * x, axis=-1, keepdims=True)
    h = (x * lax.rsqrt(ms + RMS_EPS) * g_ref[...]).astype(BF16)
    scale = HEAD_DIM ** -0.5

    def proj(lo, hi):
        return jnp.dot(h, w_ref[:, lo:hi], preferred_element_type=F32)

    def dup_heads(t, out_ref):
        lane = lax.broadcasted_iota(jnp.int32, t.shape, 1)
        lo = lane < HEAD_DIM
        r = pltpu.roll(t, HEAD_DIM, axis=1)
        out_ref[0, :, 0:LANES] = jnp.where(lo, t, r).astype(BF16)
        out_ref[0, :, LANES:2 * LANES] = jnp.where(lo, r, t).astype(BF16)

    aq_ref[0] = (proj(0, OFF_AK) * scale).astype(BF16)
    dup_heads(proj(OFF_AK, OFF_AV), ak_ref)
    dup_heads(proj(OFF_AV, OFF_BQ), av_ref)
    bqt_ref[0] = (_nt_dot(wqt_ref[...], h) * (scale * LOG2E)).astype(BF16)
    bk_ref[0] = proj(OFF_BK, OFF_BV).astype(BF16)
    bvt = _nt_dot(wvt_ref[...], h).astype(BF16)
    ones = jnp.ones((VROWS - LANES, TK_DIFF), BF16)
    for c in range(TM_PROJ // TK_DIFF):
        for hd in range(DIFF_HEADS):
            bvt_ref[0, c, hd * VROWS:hd * VROWS + LANES, :] = (
                bvt[hd * LANES:(hd + 1) * LANES, c * TK_DIFF:(c + 1) * TK_DIFF])
            bvt_ref[0, c, hd * VROWS + LANES:(hd + 1) * VROWS, :] = ones
    gate = proj(OFF_GATE, IN_COLS)
    sg_ref[0] = (gate * (1.0 / (1.0 + jnp.exp(-gate)))).astype(BF16)


def _inproj(x, g, w, wqt, wvt):
    b, s, d = x.shape
    tm = TM_PROJ
    row = lambda bi, i: (bi, i, 0)
    whole = lambda bi, i: (0, 0)
    bsd = lambda wd: jax.ShapeDtypeStruct((b, s, wd), BF16)
    return pl.pallas_call(
        _inproj_kernel,
        grid=(b, s // tm),
        in_specs=[pl.BlockSpec((1, tm, d), row),
                  pl.BlockSpec((1, d), whole),
                  pl.BlockSpec((d, IN_COLS), whole),
                  pl.BlockSpec((B_W, d), whole),
                  pl.BlockSpec((B_W, d), whole)],
        out_specs=[pl.BlockSpec((1, tm, A_Q), row),
                   pl.BlockSpec((1, tm, 2 * LANES), row),
                   pl.BlockSpec((1, tm, 2 * LANES), row),
                   pl.BlockSpec((1, B_W, tm), lambda bi, i: (bi, 0, i)),
                   pl.BlockSpec((1, tm, B_W), row),
                   pl.BlockSpec((1, tm // TK_DIFF, DIFF_HEADS * VROWS, TK_DIFF), lambda bi, i: (bi, i, 0, 0)),
                   pl.BlockSpec((1, tm, MIX), row)],
        out_shape=[bsd(A_Q), bsd(2 * LANES), bsd(2 * LANES),
                   jax.ShapeDtypeStruct((b, B_W, s), BF16),
                   bsd(B_W),
                   jax.ShapeDtypeStruct((b, s // TK_DIFF, DIFF_HEADS * VROWS, TK_DIFF), BF16),
                   bsd(MIX)],
        compiler_params=pltpu.CompilerParams(
            dimension_semantics=("arbitrary", "arbitrary"), vmem_limit_bytes=VMEM_LIMIT),
        name="inproj",
    )(x, g, w, wqt, wvt)


def _swa_kernel(sinks_ref, q_ref, kp_ref, kc_ref, vp_ref, vc_ref, sg_ref, o_ref, kcat, vcat):
    i = pl.program_id(1)
    kcat[0:WINDOW, :] = kp_ref[0]
    kcat[WINDOW:, :] = kc_ref[0]
    vcat[0:WINDOW, :] = vp_ref[0]
    vcat[WINDOW:, :] = vc_ref[0]

    r = lax.broadcasted_iota(jnp.int32, (WINDOW, 2 * WINDOW), 0)
    c = lax.broadcasted_iota(jnp.int32, (WINDOW, 2 * WINDOW), 1)
    dist = r + WINDOW - c
    distf = dist.astype(F32)
    band = (dist >= 0) & (dist < WINDOW)
    first_band = band & ((i > 0) | (c >= WINDOW))
    lane_lo = lax.broadcasted_iota(jnp.int32, (WINDOW, LANES), 1) < HEAD_DIM
    group = SWA_Q_HEADS // SWA_KV_HEADS

    for w in range(TQ_SWA // WINDOW):
        rows = slice(w * WINDOW, (w + 1) * WINDOW)
        krows = slice(w * WINDOW, (w + 2) * WINDOW)
        valid = first_band if w == 0 else band
        for cg in range(A_Q // LANES):
            cols = slice(cg * LANES, (cg + 1) * LANES)
            g = (2 * cg) // group
            gcols = slice(g * LANES, (g + 1) * LANES)
            qc = q_ref[0, rows, cols]
            kw = kcat[krows, gcols]
            vw = vcat[krows, gcols]
            outs = []
            for half in range(2):
                hh = 2 * cg + half
                slope = 2.0 ** (-8.0 * (hh + 1) / SWA_Q_HEADS)
                qh = jnp.where(lane_lo if half == 0 else ~lane_lo, qc, jnp.zeros_like(qc))
                s = _nt_dot(qh, kw) - slope * distf
                s = jnp.where(valid, s, NEG_BIG)
                sink = sinks_ref[hh]
                m = jnp.maximum(jnp.max(s, axis=-1, keepdims=True), sink)
                p = jnp.exp(s - m)
                denom = jnp.sum(p, axis=-1, keepdims=True) + jnp.exp(sink - m)
                o = jnp.dot(p.astype(BF16), vw, preferred_element_type=F32)
                outs.append(o / denom)
            o = jnp.where(lane_lo, outs[0], outs[1])
            o_ref[0, rows, cols] = (o * sg_ref[0, rows, cols].astype(F32)).astype(BF16)


def _swa(sinks, aq, akd, avd, sg):
    b, s, _ = aq.shape
    tq = TQ_SWA
    wpt = tq // WINDOW
    cur = lambda bi, i: (bi, i, 0)
    prev = lambda bi, i: (bi, jnp.maximum(i * wpt - 1, 0), 0)
    return pl.pallas_call(
        _swa_kernel,
        grid=(b, s // tq),
        in_specs=[pl.BlockSpec(memory_space=pltpu.SMEM),
                  pl.BlockSpec((1, tq, A_Q), cur),
                  pl.BlockSpec((1, WINDOW, 2 * LANES), prev),
                  pl.BlockSpec((1, tq, 2 * LANES), cur),
                  pl.BlockSpec((1, WINDOW, 2 * LANES), prev),
                  pl.BlockSpec((1, tq, 2 * LANES), cur),
                  pl.BlockSpec((1, tq, A_Q), cur)],
        out_specs=pl.BlockSpec((1, tq, A_Q), cur),
        out_shape=jax.ShapeDtypeStruct((b, s, A_Q), BF16),
        scratch_shapes=[pltpu.VMEM((tq + WINDOW, 2 * LANES), BF16),
                        pltpu.VMEM((tq + WINDOW, 2 * LANES), BF16)],
        compiler_params=pltpu.CompilerParams(
            dimension_semantics=("arbitrary", "arbitrary"), vmem_limit_bytes=VMEM_LIMIT),
        name="swa",
    )(sinks, aq, akd, akd, avd, avd, sg)


def _diff_kernel(slopes_ref, lq1_ref, lk1_ref, lq2_ref, lk2_ref, subg_ref,
                 qt_ref, k_ref, vt_ref, sg_ref, o_ref,
                 qa_ref, s_ref, m_ref, acc_ref, *, lambda_init):
    i = pl.program_id(2)

    @pl.when(i == 0)
    def _():
        _diff_tile(slopes_ref, lq1_ref, lk1_ref, lq2_ref, lk2_ref, subg_ref,
                   qt_ref, k_ref, vt_ref, sg_ref, o_ref, qa_ref, s_ref, m_ref, acc_ref,
                   lambda_init=lambda_init, first_tile=True)

    @pl.when(i > 0)
    def _():
        _diff_tile(slopes_ref, lq1_ref, lk1_ref, lq2_ref, lk2_ref, subg_ref,
                   qt_ref, k_ref, vt_ref, sg_ref, o_ref, qa_ref, s_ref, m_ref, acc_ref,
                   lambda_init=lambda_init, first_tile=False)


def _diff_tile(slopes_ref, lq1_ref, lk1_ref, lq2_ref, lk2_ref, subg_ref,
               qt_ref, k_ref, vt_ref, sg_ref, o_ref,
               qa_ref, s_ref, m_ref, acc_ref, *, lambda_init, first_tile):
    tg, tk = TG_DIFF, TK_DIFF
    nchain = 2 * NG_DIFF
    hd = pl.program_id(1)
    i = pl.program_id(2)
    slope = slopes_ref[hd] * LOG2E

    def key_consts(n):
        kk = lax.broadcasted_iota(jnp.int32, (n, LANES), 0).astype(F32)
        lane = lax.broadcasted_iota(jnp.int32, (n, LANES), 1)
        ab = slope * kk
        ab_hi = ab.astype(BF16).astype(F32)
        ab_lo = ab - ab_hi
        zk = jnp.zeros((n, LANES), F32)
        bias1 = jnp.where(lane == HEAD_DIM, ab_hi,
                          jnp.where(lane == HEAD_DIM + 1, ab_lo, zk)).astype(BF16)
        bias2 = jnp.where(lane == 0, ab_hi, jnp.where(lane == 1, ab_lo, zk)).astype(BF16)
        return lane < HEAD_DIM, bias1, bias2

    kconst = {n: key_consts(n) for n in ((tg,) if first_tile else (tg, tk))}

    rowq = lax.broadcasted_iota(jnp.int32, (LANES, tg), 0)
    one = jnp.ones((LANES, tg), F32)
    zq = jnp.zeros((LANES, tg), F32)
    for g in range(NG_DIFF):
        qt = qt_ref[0, :, g * tg:(g + 1) * tg].astype(F32)
        qa_ref[2 * g] = jnp.where(rowq < HEAD_DIM, qt,
                                  jnp.where(rowq < HEAD_DIM + 2, one, zq)).astype(BF16)
        qa_ref[2 * g + 1] = jnp.where(rowq >= HEAD_DIM, qt,
                                      jnp.where(rowq < 2, one, zq)).astype(BF16)
    m_ref[...] = jnp.full(m_ref.shape, NEG_BIG, F32)
    acc_ref[...] = jnp.zeros(acc_ref.shape, F32)

    krow = lax.broadcasted_iota(jnp.int32, (tg, tg), 0)
    qcol = lax.broadcasted_iota(jnp.int32, (tg, tg), 1)
    causal = krow <= qcol

    def scores(buf, c, ka, n):
        s_ref[buf, c, 0:n, :] = jnp.dot(ka, qa_ref[c], preferred_element_type=F32)

    def consume(buf, c, vt, cj, n, masked):
        s = s_ref[buf, c, 0:n, :]
        if masked:
            s = jnp.where(causal, s, NEG_BIG)
        m_old = m_ref[c]
        m_new = jnp.maximum(m_old, jnp.max(s, axis=0, keepdims=True) + cj)
        alpha = jnp.exp2(m_old - m_new)
        p = jnp.exp2(s - (m_new - cj)).astype(BF16)
        acc_ref[c] = alpha * acc_ref[c] + jnp.dot(vt, p, preferred_element_type=F32)
        m_ref[c] = m_new

    def keys(start, n):
        k = k_ref[0, pl.ds(pl.multiple_of(start, n), n), :]
        lane_lo, bias1, bias2 = kconst[n]
        return jnp.where(lane_lo, k, bias1), jnp.where(lane_lo, bias2, k)

    def block_bias(start):
        return slope * start.astype(F32)

    lam = (jnp.exp(jnp.sum(lq1_ref[...] * lk1_ref[...], axis=-1, keepdims=True))
           - jnp.exp(jnp.sum(lq2_ref[...] * lk2_ref[...], axis=-1, keepdims=True))
           + lambda_init)

    def finish(g):
        a1 = acc_ref[2 * g]
        a2 = acc_ref[2 * g + 1]
        ot = (a1[0:LANES] / a1[LANES:LANES + 1]
              - lam * (a2[0:LANES] / a2[LANES:LANES + 1]))
        ot = ot * lax.rsqrt(jnp.mean(ot * ot, axis=0, keepdims=True) + SUBLN_EPS) * subg_ref[...]
        ot = ot * (1.0 - lambda_init)
        rows = slice(g * tg, (g + 1) * tg)
        o_ref[0, rows, :] = (ot.T * sg_ref[0, rows, :].astype(F32)).astype(BF16)

    base = i * (NG_DIFF * tg)
    ka = keys(base, tg)
    for c in range(nchain):
        scores(0, c, ka[c % 2], tg)
    ka_first = None if first_tile else keys(jnp.int32(0), tk)
    for jj in range(NG_DIFF):
        cur, nxt = jj % 2, 1 - jj % 2
        start = base + jj * tg
        vt = vt_ref[0, i * (NG_DIFF * tg // tk) + (jj * tg) // tk, :,
                    (jj * tg) % tk:(jj * tg) % tk + tg]
        cj = block_bias(start)
        ka = keys(start + tg, tg) if jj + 1 < NG_DIFF else None
        for g in range(jj, NG_DIFF):
            for c in (2 * g, 2 * g + 1):
                if g > jj:
                    scores(nxt, c, ka[c % 2], tg)
                consume(cur, c, vt, cj, tg, g == jj)
                if g == jj and not first_tile:
                    scores(0, c, ka_first[c % 2], tk)
            if g == jj and first_tile:
                finish(g)
    if first_tile:
        return

    nblk = i * (NG_DIFF * tg // tk)

    def step(j, cur, last):
        vt = vt_ref[0, j]
        cj = block_bias(j * tk)
        kan = None if last else keys((j + 1) * tk, tk)
        for g in range(NG_DIFF):
            for c in (2 * g, 2 * g + 1):
                if not last:
                    scores(1 - cur, c, kan[c % 2], tk)
                consume(cur, c, vt, cj, tk, False)
            if last:
                finish(g)

    def body(t, carry):
        step(2 * t, 0, False)
        step(2 * t + 1, 1, False)
        return carry

    lax.fori_loop(0, nblk // 2 - 1, body, 0)
    step(nblk - 2, 0, False)
    step(nblk - 1, 1, True)


def _diff(slopes, lq1, lk1, lq2, lk2, subg, bqt, bk, bvt, sg, lambda_init):
    b, s, _ = bk.shape
    tg, tk = TG_DIFF, TK_DIFF
    tq = NG_DIFF * tg
    nchain = 2 * NG_DIFF
    smem = pl.BlockSpec(memory_space=pltpu.SMEM)
    small = lambda shape: pl.BlockSpec(shape, lambda bi, h, i: (0, 0))
    return pl.pallas_call(
        functools.partial(_diff_kernel, lambda_init=lambda_init),
        grid=(b, DIFF_HEADS, s // tq),
        in_specs=[smem,
                  small((1, HEAD_DIM)), small((1, HEAD_DIM)),
                  small((1, HEAD_DIM)), small((1, HEAD_DIM)),
                  small((LANES, 1)),
                  pl.BlockSpec((1, LANES, tq), lambda bi, h, i: (bi, h, i)),
                  pl.BlockSpec((1, s, LANES), lambda bi, h, i: (bi, 0, h)),
                  pl.BlockSpec((1, s // tk, VROWS, tk), lambda bi, h, i: (bi, 0, h, 0)),
                  pl.BlockSpec((1, tq, LANES), lambda bi, h, i: (bi, i, A_Q // LANES + h))],
        out_specs=pl.BlockSpec((1, tq, LANES), lambda bi, h, i: (bi, i, h)),
        out_shape=jax.ShapeDtypeStruct((b, s, B_W), BF16),
        scratch_shapes=[pltpu.VMEM((nchain, LANES, tg), BF16),
                        pltpu.VMEM((2, nchain, tk, tg), F32),
                        pltpu.VMEM((nchain, 1, tg), F32),
                        pltpu.VMEM((nchain, VROWS, tg), F32)],
        compiler_params=pltpu.CompilerParams(
            dimension_semantics=("arbitrary", "arbitrary", "arbitrary"),
            vmem_limit_bytes=VMEM_LIMIT),
        name="diffattn",
    )(slopes, lq1, lk1, lq2, lk2, subg, bqt, bk, bvt, sg)


def _outproj_kernel(x_ref, ma_ref, mb_ref, w_ref, fg_ref, o_ref, *, final_norm):
    y = (x_ref[...]
         + jnp.dot(ma_ref[...], w_ref[0:A_Q, :], preferred_element_type=F32)
         + jnp.dot(mb_ref[...], w_ref[A_Q:MIX, :], preferred_element_type=F32))
    if final_norm:
        ms = jnp.mean(y * y, axis=-1, keepdims=True)
        y = y * lax.rsqrt(ms + RMS_EPS) * fg_ref[...]
    o_ref[...] = y


def _outproj(xf, ma, mb, w, fg, final_norm):
    n, d = xf.shape
    tm = TM_PROJ
    row = lambda i: (i, 0)
    whole = lambda i: (0, 0)
    return pl.pallas_call(
        functools.partial(_outproj_kernel, final_norm=final_norm),
        grid=(n // tm,),
        in_specs=[pl.BlockSpec((tm, d), row),
                  pl.BlockSpec((tm, A_Q), row),
                  pl.BlockSpec((tm, B_W), row),
                  pl.BlockSpec((MIX, d), whole),
                  pl.BlockSpec((1, d), whole)],
        out_specs=pl.BlockSpec((tm, d), row),
        out_shape=jax.ShapeDtypeStruct((n, d), F32),
        compiler_params=pltpu.CompilerParams(
            dimension_semantics=("arbitrary",), vmem_limit_bytes=VMEM_LIMIT),
        name="outproj",
    )(xf, ma, mb, w, fg)


def kernel(x, norm_g, w_in, sinks, lambda_q1, lambda_k1, lambda_q2, lambda_k2,
           subln_g, w_out, final_g):
    b, s, d = x.shape
    depth = norm_g.shape[0]
    n = b * s
    diff_slopes = jnp.asarray(
        [2.0 ** (-8.0 * (h + 1) / DIFF_HEADS) for h in range(DIFF_HEADS)], F32)
    h3 = x
    for layer in range(depth):
        w = w_in[layer]
        aq, akd, avd, bqt, bk, bvt, sg = _inproj(
            h3, norm_g[layer].reshape(1, d), w.astype(BF16),
            w[:, OFF_BQ:OFF_BK].T.astype(BF16), w[:, OFF_BV:OFF_GATE].T.astype(BF16))
        mixed_a = _swa(sinks[layer], aq, akd, avd, sg)
        mixed_b = _diff(diff_slopes,
                        lambda_q1[layer].reshape(1, HEAD_DIM), lambda_k1[layer].reshape(1, HEAD_DIM),
                        lambda_q2[layer].reshape(1, HEAD_DIM), lambda_k2[layer].reshape(1, HEAD_DIM),
                        subln_g[layer].reshape(LANES, 1),
                        bqt, bk, bvt, sg, _lambda_init(layer))
        hf = _outproj(h3.reshape(n, d), mixed_a.reshape(n, A_Q), mixed_b.reshape(n, B_W),
                      w_out[layer].astype(BF16), final_g.reshape(1, d),
                      final_norm=(layer == depth - 1))
        h3 = hf.reshape(b, s, d)
    return h3
```

```python
import functools
import math

import jax
import jax.numpy as jnp
import numpy as np
from jax import lax
from jax.experimental import pallas as pl
from jax.experimental.pallas import tpu as pltpu

F32 = jnp.float32
BF16 = jnp.bfloat16

HEAD_DIM = 64
LANES = 128
SWA_Q_HEADS = 8
SWA_KV_HEADS = 2
WINDOW = 128
DIFF_HEADS = 4
RMS_EPS = 1e-6
SUBLN_EPS = 1e-5
NEG_BIG = -1e30
LOG2E = math.log2(math.e)

A_Q = SWA_Q_HEADS * HEAD_DIM
A_KV = SWA_KV_HEADS * HEAD_DIM
B_W = DIFF_HEADS * 2 * HEAD_DIM
MIX = A_Q + B_W
OFF_AK = A_Q
OFF_AV = OFF_AK + A_KV
OFF_BQ = OFF_AV + A_KV
OFF_BK = OFF_BQ + B_W
OFF_BV = OFF_BK + B_W
OFF_GATE = OFF_BV + B_W
IN_COLS = OFF_GATE + MIX

TM_PROJ = 512
TQ_SWA = 512
TG_DIFF = 256
NG_DIFF = 4
TK_DIFF = 512
VROWS = LANES + 16
VROWS_A = HEAD_DIM + 16
VMEM_LIMIT = 56 * 1024 * 1024


def _lambda_init(layer_idx):
    return 0.8 - 0.6 * math.exp(-0.3 * layer_idx)


def _nt_dot(a, b):
    return lax.dot_general(a, b, (((1,), (1,)), ((), ())), preferred_element_type=F32)


def _inproj_kernel(x_ref, g_ref, wtok_ref, wfeat_ref,
                   aqt_ref, ak_ref, avt_ref, bqt_ref, bk_ref, bvt_ref, sg_ref):
    x = x_ref[0]
    ms = jnp.mean(x * x, axis=-1, keepdims=True)
    h = (x * lax.rsqrt(ms + RMS_EPS) * g_ref[...]).astype(BF16)
    qscale = HEAD_DIM ** -0.5 * LOG2E

    def tok(lo, hi):
        return jnp.dot(h, wtok_ref[:, lo:hi], preferred_element_type=F32)

    def feat(lo, hi):
        return _nt_dot(wfeat_ref[lo:hi, :], h)

    ak_ref[0] = tok(0, A_KV).astype(BF16)
    bk_ref[0] = tok(A_KV, A_KV + B_W).astype(BF16)
    gate = tok(A_KV + B_W, A_KV + B_W + MIX)
    sg_ref[0] = (gate * (1.0 / (1.0 + jnp.exp(-gate)))).astype(BF16)

    aqt_ref[0] = (feat(0, A_Q) * qscale).astype(BF16)
    avt = feat(A_Q, A_Q + A_KV).astype(BF16)
    for g in range(SWA_KV_HEADS):
        avt_ref[0, g * VROWS_A:g * VROWS_A + HEAD_DIM, :] = avt[g * HEAD_DIM:(g + 1) * HEAD_DIM]
        avt_ref[0, g * VROWS_A + HEAD_DIM:(g + 1) * VROWS_A, :] = jnp.ones(
            (VROWS_A - HEAD_DIM, TM_PROJ), BF16)
    bqt_ref[0] = (feat(A_Q + A_KV, A_Q + A_KV + B_W) * qscale).astype(BF16)
    bvt = feat(A_Q + A_KV + B_W, A_Q + A_KV + 2 * B_W).astype(BF16)
    ones = jnp.ones((VROWS - LANES, TK_DIFF), BF16)
    for c in range(TM_PROJ // TK_DIFF):
        for hd in range(DIFF_HEADS):
            bvt_ref[0, c, hd * VROWS:hd * VROWS + LANES, :] = (
                bvt[hd * LANES:(hd + 1) * LANES, c * TK_DIFF:(c + 1) * TK_DIFF])
            bvt_ref[0, c, hd * VROWS + LANES:(hd + 1) * VROWS, :] = ones


def _inproj(x, g, wtok, wfeat):
    b, s, d = x.shape
    tm = TM_PROJ
    row = lambda bi, i: (bi, i, 0)
    col = lambda bi, i: (bi, 0, i)
    whole = lambda bi, i: (0, 0)
    bsd = lambda wd: jax.ShapeDtypeStruct((b, s, wd), BF16)
    bds = lambda wd: jax.ShapeDtypeStruct((b, wd, s), BF16)
    return pl.pallas_call(
        _inproj_kernel,
        grid=(b, s // tm),
        in_specs=[pl.BlockSpec((1, tm, d), row),
                  pl.BlockSpec((1, d), whole),
                  pl.BlockSpec(wtok.shape, whole),
                  pl.BlockSpec(wfeat.shape, whole)],
        out_specs=[pl.BlockSpec((1, A_Q, tm), col),
                   pl.BlockSpec((1, tm, A_KV), row),
                   pl.BlockSpec((1, SWA_KV_HEADS * VROWS_A, tm), col),
                   pl.BlockSpec((1, B_W, tm), col),
                   pl.BlockSpec((1, tm, B_W), row),
                   pl.BlockSpec((1, tm // TK_DIFF, DIFF_HEADS * VROWS, TK_DIFF), lambda bi, i: (bi, i, 0, 0)),
                   pl.BlockSpec((1, tm, MIX), row)],
        out_shape=[bds(A_Q), bsd(A_KV), bds(SWA_KV_HEADS * VROWS_A), bds(B_W), bsd(B_W),
                   jax.ShapeDtypeStruct((b, s // TK_DIFF, DIFF_HEADS * VROWS, TK_DIFF), BF16),
                   bsd(MIX)],
        compiler_params=pltpu.CompilerParams(
            dimension_semantics=("arbitrary", "arbitrary"), vmem_limit_bytes=VMEM_LIMIT),
        name="inproj",
    )(x, g, wtok, wfeat)


def _swa_kernel(sinks_ref, qt_ref, kp_ref, kc_ref, vtp_ref, vtc_ref, sg_ref, o_ref, kcat, vtcat):
    w_ = WINDOW
    i = pl.program_id(1)
    group = SWA_Q_HEADS // SWA_KV_HEADS
    nq = group * w_
    kcat[0:w_, :] = kp_ref[0]
    kcat[w_:, :] = kc_ref[0]
    vtcat[:, 0:w_] = vtp_ref[0]
    vtcat[:, w_:] = vtc_ref[0]

    kidx = lax.broadcasted_iota(jnp.int32, (2 * w_, nq), 0)
    qidx = lax.broadcasted_iota(jnp.int32, (2 * w_, nq), 1) & (w_ - 1)
    in_cur = kidx >= w_
    band = (in_cur & (kidx - w_ <= qidx)) | ((kidx < w_) & (kidx > qidx))
    first_band = band & ((i > 0) | in_cur)

    lane = lax.broadcasted_iota(jnp.int32, (2 * w_, LANES), 1)
    kpos = lax.broadcasted_iota(jnp.int32, (2 * w_, LANES), 0).astype(F32)
    zk = jnp.zeros((2 * w_, LANES), F32)
    kmask = (lane < HEAD_DIM, lane >= HEAD_DIM)
    kpos_lanes = (jnp.where((lane == HEAD_DIM) | (lane == HEAD_DIM + 1), kpos, zk).astype(BF16),
                  jnp.where(lane < 2, kpos, zk).astype(BF16))

    hrow = lax.broadcasted_iota(jnp.int32, (HEAD_DIM, nq), 0)
    hcol = lax.broadcasted_iota(jnp.int32, (HEAD_DIM, nq), 1)

    def slope_rows(g):
        out = jnp.zeros((HEAD_DIM, nq), F32)
        for u in range(group):
            sl = 2.0 ** (-8.0 * (g * group + u + 1) / SWA_Q_HEADS) * LOG2E
            hi = float(np.asarray(sl, dtype=BF16).astype(np.float32))
            lo = sl - hi
            in_head = (hcol >= u * w_) & (hcol < (u + 1) * w_)
            out = jnp.where(in_head & (hrow == 0), hi, jnp.where(in_head & (hrow == 1), lo, out))
        return out.astype(BF16)

    srows = [slope_rows(g) for g in range(SWA_KV_HEADS)]

    qpos = (lax.broadcasted_iota(jnp.int32, (1, w_), 1) + w_).astype(F32)

    def sink_row(g):
        return jnp.concatenate(
            [(sinks_ref[g * group + u] + 2.0 ** (-8.0 * (g * group + u + 1) / SWA_Q_HEADS) * qpos) * LOG2E
             for u in range(group)], axis=1)

    sinkv = [sink_row(g) for g in range(SWA_KV_HEADS)]

    def scores(w, g):
        keys = kcat[w * w_:(w + 2) * w_, :]
        ka = jnp.where(kmask[g], keys, kpos_lanes[g])
        qh = jnp.concatenate(
            [qt_ref[0, (g * group + u) * HEAD_DIM:(g * group + u + 1) * HEAD_DIM, w * w_:(w + 1) * w_]
             for u in range(group)], axis=1)
        wq = jnp.concatenate([qh, srows[g]] if g == 0 else [srows[g], qh], axis=0)
        return jnp.dot(ka, wq, preferred_element_type=F32)

    def consume(w, g, s):
        s = jnp.where(first_band if w == 0 else band, s, NEG_BIG)
        m = jnp.maximum(jnp.max(s, axis=0, keepdims=True), sinkv[g])
        p = jnp.exp2(s - m).astype(BF16)
        vt = vtcat[g * VROWS_A:(g + 1) * VROWS_A, w * w_:(w + 2) * w_]
        o = jnp.dot(vt, p, preferred_element_type=F32)
        denom = o[HEAD_DIM:HEAD_DIM + 1] + jnp.exp2(sinkv[g] - m)
        ot = o[0:HEAD_DIM] / denom
        rows = slice(w * w_, (w + 1) * w_)
        for pair in range(group // 2):
            cols = slice((g * (group // 2) + pair) * LANES, (g * (group // 2) + pair + 1) * LANES)
            two = jnp.concatenate([ot[:, (2 * pair) * w_:(2 * pair + 1) * w_],
                                   ot[:, (2 * pair + 1) * w_:(2 * pair + 2) * w_]], axis=0)
            o_ref[0, rows, cols] = (two.T * sg_ref[0, rows, cols].astype(F32)).astype(BF16)

    chains = [(w, g) for w in range(TQ_SWA // w_) for g in range(SWA_KV_HEADS)]
    ahead = 3
    pending = [scores(*ch) for ch in chains[:ahead]]
    for n, (w, g) in enumerate(chains):
        if n + ahead < len(chains):
            pending.append(scores(*chains[n + ahead]))
        consume(w, g, pending.pop(0))


def _swa(sinks, aqt, ak, avt, sg):
    b, s, _ = ak.shape
    tq = TQ_SWA
    wpt = tq // WINDOW
    vr = SWA_KV_HEADS * VROWS_A
    prev_w = lambda i: jnp.maximum(i * wpt - 1, 0)
    return pl.pallas_call(
        _swa_kernel,
        grid=(b, s // tq),
        in_specs=[pl.BlockSpec(memory_space=pltpu.SMEM),
                  pl.BlockSpec((1, A_Q, tq), lambda bi, i: (bi, 0, i)),
                  pl.BlockSpec((1, WINDOW, A_KV), lambda bi, i: (bi, prev_w(i), 0)),
                  pl.BlockSpec((1, tq, A_KV), lambda bi, i: (bi, i, 0)),
                  pl.BlockSpec((1, vr, WINDOW), lambda bi, i: (bi, 0, prev_w(i))),
                  pl.BlockSpec((1, vr, tq), lambda bi, i: (bi, 0, i)),
                  pl.BlockSpec((1, tq, A_Q), lambda bi, i: (bi, i, 0))],
        out_specs=pl.BlockSpec((1, tq, A_Q), lambda bi, i: (bi, i, 0)),
        out_shape=jax.ShapeDtypeStruct((b, s, A_Q), BF16),
        scratch_shapes=[pltpu.VMEM((tq + WINDOW, A_KV), BF16),
                        pltpu.VMEM((vr, tq + WINDOW), BF16)],
        compiler_params=pltpu.CompilerParams(
            dimension_semantics=("arbitrary", "arbitrary"), vmem_limit_bytes=VMEM_LIMIT),
        name="swa",
    )(sinks, aqt, ak, ak, avt, avt, sg)


def _diff_kernel(slopes_ref, lq1_ref, lk1_ref, lq2_ref, lk2_ref, subg_ref,
                 qt_ref, k_ref, vt_ref, sg_ref, o_ref,
                 qa_ref, s_ref, m_ref, acc_ref, *, lambda_init):
    i = pl.program_id(2)

    @pl.when(i == 0)
    def _():
        _diff_tile(slopes_ref, lq1_ref, lk1_ref, lq2_ref, lk2_ref, subg_ref,
                   qt_ref, k_ref, vt_ref, sg_ref, o_ref, qa_ref, s_ref, m_ref, acc_ref,
                   lambda_init=lambda_init, first_tile=True)

    @pl.when(i > 0)
    def _():
        _diff_tile(slopes_ref, lq1_ref, lk1_ref, lq2_ref, lk2_ref, subg_ref,
                   qt_ref, k_ref, vt_ref, sg_ref, o_ref, qa_ref, s_ref, m_ref, acc_ref,
                   lambda_init=lambda_init, first_tile=False)


def _diff_tile(slopes_ref, lq1_ref, lk1_ref, lq2_ref, lk2_ref, subg_ref,
               qt_ref, k_ref, vt_ref, sg_ref, o_ref,
               qa_ref, s_ref, m_ref, acc_ref, *, lambda_init, first_tile):
    tg, tk = TG_DIFF, TK_DIFF
    nchain = 2 * NG_DIFF
    hd = pl.program_id(1)
    i = pl.program_id(2)
    slope = slopes_ref[hd] * LOG2E

    def key_consts(n):
        kk = lax.broadcasted_iota(jnp.int32, (n, LANES), 0).astype(F32)
        lane = lax.broadcasted_iota(jnp.int32, (n, LANES), 1)
        ab = slope * kk
        ab_hi = ab.astype(BF16).astype(F32)
        ab_lo = ab - ab_hi
        zk = jnp.zeros((n, LANES), F32)
        bias1 = jnp.where(lane == HEAD_DIM, ab_hi,
                          jnp.where(lane == HEAD_DIM + 1, ab_lo, zk)).astype(BF16)
        bias2 = jnp.where(lane == 0, ab_hi, jnp.where(lane == 1, ab_lo, zk)).astype(BF16)
        return lane < HEAD_DIM, bias1, bias2

    kconst = {n: key_consts(n) for n in ((tg,) if first_tile else (tg, tk))}

    rowq = lax.broadcasted_iota(jnp.int32, (LANES, tg), 0)
    one = jnp.ones((LANES, tg), F32)
    zq = jnp.zeros((LANES, tg), F32)
    for g in range(NG_DIFF):
        qt = qt_ref[0, :, g * tg:(g + 1) * tg].astype(F32)
        qa_ref[2 * g] = jnp.where(rowq < HEAD_DIM, qt,
                                  jnp.where(rowq < HEAD_DIM + 2, one, zq)).astype(BF16)
        qa_ref[2 * g + 1] = jnp.where(rowq >= HEAD_DIM, qt,
                                      jnp.where(rowq < 2, one, zq)).astype(BF16)
    m_ref[...] = jnp.full(m_ref.shape, NEG_BIG, F32)
    acc_ref[...] = jnp.zeros(acc_ref.shape, F32)

    krow = lax.broadcasted_iota(jnp.int32, (tg, tg), 0)
    qcol = lax.broadcasted_iota(jnp.int32, (tg, tg), 1)
    causal = krow <= qcol

    def scores(buf, c, ka, n):
        s_ref[buf, c, 0:n, :] = jnp.dot(ka, qa_ref[c], preferred_element_type=F32)

    def consume(buf, c, vt, cj, n, masked):
        s = s_ref[buf, c, 0:n, :]
        if masked:
            s = jnp.where(causal, s, NEG_BIG)
        m_old = m_ref[c]
        m_new = jnp.maximum(m_old, jnp.max(s, axis=0, keepdims=True) + cj)
        alpha = jnp.exp2(m_old - m_new)
        p = jnp.exp2(s - (m_new - cj)).astype(BF16)
        acc_ref[c] = alpha * acc_ref[c] + jnp.dot(vt, p, preferred_element_type=F32)
        m_ref[c] = m_new

    def keys(start, n):
        k = k_ref[0, pl.ds(pl.multiple_of(start, n), n), :]
        lane_lo, bias1, bias2 = kconst[n]
        return jnp.where(lane_lo, k, bias1), jnp.where(lane_lo, bias2, k)

    def block_bias(start):
        return slope * start.astype(F32)

    lam = (jnp.exp(jnp.sum(lq1_ref[...] * lk1_ref[...], axis=-1, keepdims=True))
           - jnp.exp(jnp.sum(lq2_ref[...] * lk2_ref[...], axis=-1, keepdims=True))
           + lambda_init)

    def finish(g):
        a1 = acc_ref[2 * g]
        a2 = acc_ref[2 * g + 1]
        ot = (a1[0:LANES] / a1[LANES:LANES + 1]
              - lam * (a2[0:LANES] / a2[LANES:LANES + 1]))
        ot = ot * lax.rsqrt(jnp.mean(ot * ot, axis=0, keepdims=True) + SUBLN_EPS) * subg_ref[...]
        ot = ot * (1.0 - lambda_init)
        rows = slice(g * tg, (g + 1) * tg)
        o_ref[0, rows, :] = (ot.T * sg_ref[0, rows, :].astype(F32)).astype(BF16)

    base = i * (NG_DIFF * tg)
    ka = keys(base, tg)
    for c in range(nchain):
        scores(0, c, ka[c % 2], tg)
    ka_first = None if first_tile else keys(jnp.int32(0), tk)
    for jj in range(NG_DIFF):
        cur, nxt = jj % 2, 1 - jj % 2
        start = base + jj * tg
        vt = vt_ref[0, i * (NG_DIFF * tg // tk) + (jj * tg) // tk, :,
                    (jj * tg) % tk:(jj * tg) % tk + tg]
        cj = block_bias(start)
        ka = keys(start + tg, tg) if jj + 1 < NG_DIFF else None
        for g in range(jj, NG_DIFF):
            for c in (2 * g, 2 * g + 1):
                if g > jj:
                    scores(nxt, c, ka[c % 2], tg)
                consume(cur, c, vt, cj, tg, g == jj)
                if g == jj and not first_tile:
                    scores(0, c, ka_first[c % 2], tk)
            if g == jj and first_tile:
                finish(g)
    if first_tile:
        return

    nblk = i * (NG_DIFF * tg // tk)

    def step(j, cur, last):
        vt = vt_ref[0, j]
        cj = block_bias(j * tk)
        kan = None if last else keys((j + 1) * tk, tk)
        for g in range(NG_DIFF):
            for c in (2 * g, 2 * g + 1):
                if not last:
                    scores(1 - cur, c, kan[c % 2], tk)
                consume(cur, c, vt, cj, tk, False)
            if last:
                finish(g)

    def body(t, carry):
        step(2 * t, 0, False)
        step(2 * t + 1, 1, False)
        return carry

    lax.fori_loop(0, nblk // 2 - 1, body, 0)
    step(nblk - 2, 0, False)
    step(nblk - 1, 1, True)


def _diff(slopes, lq1, lk1, lq2, lk2, subg, bqt, bk, bvt, sg, lambda_init):
    b, s, _ = bk.shape
    tg, tk = TG_DIFF, TK_DIFF
    tq = NG_DIFF * tg
    nchain = 2 * NG_DIFF
    smem = pl.BlockSpec(memory_space=pltpu.SMEM)
    small = lambda shape: pl.BlockSpec(shape, lambda bi, h, i: (0, 0))
    return pl.pallas_call(
        functools.partial(_diff_kernel, lambda_init=lambda_init),
        grid=(b, DIFF_HEADS, s // tq),
        in_specs=[smem,
                  small((1, HEAD_DIM)), small((1, HEAD_DIM)),
                  small((1, HEAD_DIM)), small((1, HEAD_DIM)),
                  small((LANES, 1)),
                  pl.BlockSpec((1, LANES, tq), lambda bi, h, i: (bi, h, i)),
                  pl.BlockSpec((1, s, LANES), lambda bi, h, i: (bi, 0, h)),
                  pl.BlockSpec((1, s // tk, VROWS, tk), lambda bi, h, i: (bi, 0, h, 0)),
                  pl.BlockSpec((1, tq, LANES), lambda bi, h, i: (bi, i, A_Q // LANES + h))],
        out_specs=pl.BlockSpec((1, tq, LANES), lambda bi, h, i: (bi, i, h)),
        out_shape=jax.ShapeDtypeStruct((b, s, B_W), BF16),
        scratch_shapes=[pltpu.VMEM((nchain, LANES, tg), BF16),
                        pltpu.VMEM((2, nchain, tk, tg), F32),
                        pltpu.VMEM((nchain, 1, tg), F32),
                        pltpu.VMEM((nchain, VROWS, tg), F32)],
        compiler_params=pltpu.CompilerParams(
            dimension_semantics=("arbitrary", "arbitrary", "arbitrary"),
            vmem_limit_bytes=VMEM_LIMIT),
        name="diffattn",
    )(slopes, lq1, lk1, lq2, lk2, subg, bqt, bk, bvt, sg)


def _outproj_kernel(x_ref, ma_ref, mb_ref, w_ref, fg_ref, o_ref, *, final_norm):
    y = (x_ref[...]
         + jnp.dot(ma_ref[...], w_ref[0:A_Q, :], preferred_element_type=F32)
         + jnp.dot(mb_ref[...], w_ref[A_Q:MIX, :], preferred_element_type=F32))
    if final_norm:
        ms = jnp.mean(y * y, axis=-1, keepdims=True)
        y = y * lax.rsqrt(ms + RMS_EPS) * fg_ref[...]
    o_ref[...] = y


def _outproj(xf, ma, mb, w, fg, final_norm):
    n, d = xf.shape
    tm = TM_PROJ
    row = lambda i: (i, 0)
    whole = lambda i: (0, 0)
    return pl.pallas_call(
        functools.partial(_outproj_kernel, final_norm=final_norm),
        grid=(n // tm,),
        in_specs=[pl.BlockSpec((tm, d), row),
                  pl.BlockSpec((tm, A_Q), row),
                  pl.BlockSpec((tm, B_W), row),
                  pl.BlockSpec((MIX, d), whole),
                  pl.BlockSpec((1, d), whole)],
        out_specs=pl.BlockSpec((tm, d), row),
        out_shape=jax.ShapeDtypeStruct((n, d), F32),
        compiler_params=pltpu.CompilerParams(
            dimension_semantics=("arbitrary",), vmem_limit_bytes=VMEM_LIMIT),
        name="outproj",
    )(xf, ma, mb, w, fg)


def kernel(x, norm_g, w_in, sinks, lambda_q1, lambda_k1, lambda_q2, lambda_k2,
           subln_g, w_out, final_g):
    b, s, d = x.shape
    depth = norm_g.shape[0]
    n = b * s
    diff_slopes = jnp.asarray(
        [2.0 ** (-8.0 * (h + 1) / DIFF_HEADS) for h in range(DIFF_HEADS)], F32)
    h3 = x
    for layer in range(depth):
        w = w_in[layer]
        cols = lambda lo, hi: w[:, lo:hi]
        wtok = jnp.concatenate(
            [cols(OFF_AK, OFF_AV), cols(OFF_BK, OFF_BV), cols(OFF_GATE, IN_COLS)], axis=1).astype(BF16)
        wfeat = jnp.concatenate(
            [cols(0, OFF_AK), cols(OFF_AV, OFF_BQ), cols(OFF_BQ, OFF_BK), cols(OFF_BV, OFF_GATE)],
            axis=1).T.astype(BF16)
        aqt, ak, avt, bqt, bk, bvt, sg = _inproj(h3, norm_g[layer].reshape(1, d), wtok, wfeat)
        mixed_a = _swa(sinks[layer], aqt, ak, avt, sg)
        mixed_b = _diff(diff_slopes,
                        lambda_q1[layer].reshape(1, HEAD_DIM), lambda_k1[layer].reshape(1, HEAD_DIM),
                        lambda_q2[layer].reshape(1, HEAD_DIM), lambda_k2[layer].reshape(1, HEAD_DIM),
                        subln_g[layer].reshape(LANES, 1),
                        bqt, bk, bvt, sg, _lambda_init(layer))
        hf = _outproj(h3.reshape(n, d), mixed_a.reshape(n, A_Q), mixed_b.reshape(n, B_W),
                      w_out[layer].astype(BF16), final_g.reshape(1, d),
                      final_norm=(layer == depth - 1))
        h3 = hf.reshape(b, s, d)
    return h3
```

```python
import functools
import math

import jax
import jax.numpy as jnp
import numpy as np
from jax import lax
from jax.experimental import pallas as pl
from jax.experimental.pallas import tpu as pltpu

F32 = jnp.float32
BF16 = jnp.bfloat16

HEAD_DIM = 64
LANES = 128
SWA_Q_HEADS = 8
SWA_KV_HEADS = 2
WINDOW = 128
DIFF_HEADS = 4
RMS_EPS = 1e-6
SUBLN_EPS = 1e-5
NEG_BIG = -1e30
LOG2E = math.log2(math.e)

A_Q = SWA_Q_HEADS * HEAD_DIM
A_KV = SWA_KV_HEADS * HEAD_DIM
B_W = DIFF_HEADS * 2 * HEAD_DIM
MIX = A_Q + B_W
OFF_AK = A_Q
OFF_AV = OFF_AK + A_KV
OFF_BQ = OFF_AV + A_KV
OFF_BK = OFF_BQ + B_W
OFF_BV = OFF_BK + B_W
OFF_GATE = OFF_BV + B_W
IN_COLS = OFF_GATE + MIX

TM_PROJ = 512
ROWS_OUT = 256
TQ_SWA = 512
TG_DIFF = 256
NG_DIFF = 4
TK_DIFF = 512
VROWS = LANES + 16
VROWS_A = HEAD_DIM + 16
VMEM_LIMIT = 56 * 1024 * 1024


def _lambda_init(layer_idx):
    return 0.8 - 0.6 * math.exp(-0.3 * layer_idx)


def _nt_dot(a, b):
    return lax.dot_general(a, b, (((1,), (1,)), ((), ())), preferred_element_type=F32)


def _inproj_kernel(x_ref, g_ref, w_ref,
                   aqt_ref, ak_ref, avt_ref, bqt_ref, bk_ref, bvt_ref, sg_ref):
    x = x_ref[0]
    ms = jnp.mean(x * x, axis=-1, keepdims=True)
    h = (x * lax.rsqrt(ms + RMS_EPS) * g_ref[...]).astype(BF16)
    qscale = HEAD_DIM ** -0.5 * LOG2E

    def tok(lo, hi):
        return jnp.dot(h, w_ref[:, lo:hi], preferred_element_type=F32)

    def feat(lo, hi):
        return lax.dot_general(w_ref[:, lo:hi], h, (((0,), (1,)), ((), ())),
                               preferred_element_type=F32)

    ak_ref[0] = tok(OFF_AK, OFF_AV).astype(BF16)
    bk_ref[0] = tok(OFF_BK, OFF_BV).astype(BF16)
    gate = tok(OFF_GATE, IN_COLS)
    sg_ref[0] = (gate * (1.0 / (1.0 + jnp.exp(-gate)))).astype(BF16)

    aqt_ref[0] = (feat(0, OFF_AK) * qscale).astype(BF16)
    avt = feat(OFF_AV, OFF_BQ).astype(BF16)
    for g in range(SWA_KV_HEADS):
        avt_ref[0, g * VROWS_A:g * VROWS_A + HEAD_DIM, :] = avt[g * HEAD_DIM:(g + 1) * HEAD_DIM]
        avt_ref[0, g * VROWS_A + HEAD_DIM:(g + 1) * VROWS_A, :] = jnp.ones(
            (VROWS_A - HEAD_DIM, TM_PROJ), BF16)
    bqt_ref[0] = (feat(OFF_BQ, OFF_BK) * qscale).astype(BF16)
    bvt = feat(OFF_BV, OFF_GATE).astype(BF16)
    ones = jnp.ones((VROWS - LANES, TK_DIFF), BF16)
    for c in range(TM_PROJ // TK_DIFF):
        for hd in range(DIFF_HEADS):
            bvt_ref[0, c, hd * VROWS:hd * VROWS + LANES, :] = (
                bvt[hd * LANES:(hd + 1) * LANES, c * TK_DIFF:(c + 1) * TK_DIFF])
            bvt_ref[0, c, hd * VROWS + LANES:(hd + 1) * VROWS, :] = ones


def _inproj(x, g, w):
    b, s, d = x.shape
    tm = TM_PROJ
    row = lambda bi, i: (bi, i, 0)
    col = lambda bi, i: (bi, 0, i)
    whole = lambda bi, i: (0, 0)
    bsd = lambda wd: jax.ShapeDtypeStruct((b, s, wd), BF16)
    bds = lambda wd: jax.ShapeDtypeStruct((b, wd, s), BF16)
    return pl.pallas_call(
        _inproj_kernel,
        grid=(b, s // tm),
        in_specs=[pl.BlockSpec((1, tm, d), row),
                  pl.BlockSpec((1, d), whole),
                  pl.BlockSpec(w.shape, whole)],
        out_specs=[pl.BlockSpec((1, A_Q, tm), col),
                   pl.BlockSpec((1, tm, A_KV), row),
                   pl.BlockSpec((1, SWA_KV_HEADS * VROWS_A, tm), col),
                   pl.BlockSpec((1, B_W, tm), col),
                   pl.BlockSpec((1, tm, B_W), row),
                   pl.BlockSpec((1, tm // TK_DIFF, DIFF_HEADS * VROWS, TK_DIFF), lambda bi, i: (bi, i, 0, 0)),
                   pl.BlockSpec((1, tm, MIX), row)],
        out_shape=[bds(A_Q), bsd(A_KV), bds(SWA_KV_HEADS * VROWS_A), bds(B_W), bsd(B_W),
                   jax.ShapeDtypeStruct((b, s // TK_DIFF, DIFF_HEADS * VROWS, TK_DIFF), BF16),
                   bsd(MIX)],
        compiler_params=pltpu.CompilerParams(
            dimension_semantics=("arbitrary", "arbitrary"), vmem_limit_bytes=VMEM_LIMIT),
        name="inproj",
    )(x, g, w)


def _swa_kernel(sinks_ref, qt_ref, kp_ref, kc_ref, vtp_ref, vtc_ref, sg_ref, o_ref, kcat, vtcat):
    w_ = WINDOW
    i = pl.program_id(1)
    group = SWA_Q_HEADS // SWA_KV_HEADS
    nq = group * w_
    kcat[0:w_, :] = kp_ref[0]
    kcat[w_:, :] = kc_ref[0]
    vtcat[:, 0:w_] = vtp_ref[0]
    vtcat[:, w_:] = vtc_ref[0]

    kidx = lax.broadcasted_iota(jnp.int32, (2 * w_, nq), 0)
    qidx = lax.broadcasted_iota(jnp.int32, (2 * w_, nq), 1) & (w_ - 1)
    in_cur = kidx >= w_
    band = (in_cur & (kidx - w_ <= qidx)) | ((kidx < w_) & (kidx > qidx))
    first_band = band & ((i > 0) | in_cur)

    lane = lax.broadcasted_iota(jnp.int32, (2 * w_, LANES), 1)
    kpos = lax.broadcasted_iota(jnp.int32, (2 * w_, LANES), 0).astype(F32)
    zk = jnp.zeros((2 * w_, LANES), F32)
    kmask = (lane < HEAD_DIM, lane >= HEAD_DIM)
    kpos_lanes = (jnp.where((lane == HEAD_DIM) | (lane == HEAD_DIM + 1), kpos, zk).astype(BF16),
                  jnp.where(lane < 2, kpos, zk).astype(BF16))

    hrow = lax.broadcasted_iota(jnp.int32, (HEAD_DIM, nq), 0)
    hcol = lax.broadcasted_iota(jnp.int32, (HEAD_DIM, nq), 1)

    def slope_rows(g):
        out = jnp.zeros((HEAD_DIM, nq), F32)
        for u in range(group):
            sl = 2.0 ** (-8.0 * (g * group + u + 1) / SWA_Q_HEADS) * LOG2E
            hi = float(np.asarray(sl, dtype=BF16).astype(np.float32))
            lo = sl - hi
            in_head = (hcol >= u * w_) & (hcol < (u + 1) * w_)
            out = jnp.where(in_head & (hrow == 0), hi, jnp.where(in_head & (hrow == 1), lo, out))
        return out.astype(BF16)

    srows = [slope_rows(g) for g in range(SWA_KV_HEADS)]

    qpos = (lax.broadcasted_iota(jnp.int32, (1, w_), 1) + w_).astype(F32)

    def sink_row(g):
        return jnp.concatenate(
            [(sinks_ref[g * group + u] + 2.0 ** (-8.0 * (g * group + u + 1) / SWA_Q_HEADS) * qpos) * LOG2E
             for u in range(group)], axis=1)

    sinkv = [sink_row(g) for g in range(SWA_KV_HEADS)]

    def scores(w, g):
        keys = kcat[w * w_:(w + 2) * w_, :]
        ka = jnp.where(kmask[g], keys, kpos_lanes[g])
        qh = jnp.concatenate(
            [qt_ref[0, (g * group + u) * HEAD_DIM:(g * group + u + 1) * HEAD_DIM, w * w_:(w + 1) * w_]
             for u in range(group)], axis=1)
        wq = jnp.concatenate([qh, srows[g]] if g == 0 else [srows[g], qh], axis=0)
        return jnp.dot(ka, wq, preferred_element_type=F32)

    def consume(w, g, s):
        s = jnp.where(first_band if w == 0 else band, s, NEG_BIG)
        m = jnp.maximum(jnp.max(s, axis=0, keepdims=True), sinkv[g])
        p = jnp.exp2(s - m).astype(BF16)
        vt = vtcat[g * VROWS_A:(g + 1) * VROWS_A, w * w_:(w + 2) * w_]
        o = jnp.dot(vt, p, preferred_element_type=F32)
        denom = o[HEAD_DIM:HEAD_DIM + 1] + jnp.exp2(sinkv[g] - m)
        ot = o[0:HEAD_DIM] / denom
        rows = slice(w * w_, (w + 1) * w_)
        for pair in range(group // 2):
            cols = slice((g * (group // 2) + pair) * LANES, (g * (group // 2) + pair + 1) * LANES)
            two = jnp.concatenate([ot[:, (2 * pair) * w_:(2 * pair + 1) * w_],
                                   ot[:, (2 * pair + 1) * w_:(2 * pair + 2) * w_]], axis=0)
            o_ref[0, rows, cols] = (two.T * sg_ref[0, rows, cols].astype(F32)).astype(BF16)

    chains = [(w, g) for w in range(TQ_SWA // w_) for g in range(SWA_KV_HEADS)]
    ahead = 3
    pending = [scores(*ch) for ch in chains[:ahead]]
    for n, (w, g) in enumerate(chains):
        if n + ahead < len(chains):
            pending.append(scores(*chains[n + ahead]))
        consume(w, g, pending.pop(0))


def _swa(sinks, aqt, ak, avt, sg):
    b, s, _ = ak.shape
    tq = TQ_SWA
    wpt = tq // WINDOW
    vr = SWA_KV_HEADS * VROWS_A
    prev_w = lambda i: jnp.maximum(i * wpt - 1, 0)
    return pl.pallas_call(
        _swa_kernel,
        grid=(b, s // tq),
        in_specs=[pl.BlockSpec(memory_space=pltpu.SMEM),
                  pl.BlockSpec((1, A_Q, tq), lambda bi, i: (bi, 0, i)),
                  pl.BlockSpec((1, WINDOW, A_KV), lambda bi, i: (bi, prev_w(i), 0)),
                  pl.BlockSpec((1, tq, A_KV), lambda bi, i: (bi, i, 0)),
                  pl.BlockSpec((1, vr, WINDOW), lambda bi, i: (bi, 0, prev_w(i))),
                  pl.BlockSpec((1, vr, tq), lambda bi, i: (bi, 0, i)),
                  pl.BlockSpec((1, tq, A_Q), lambda bi, i: (bi, i, 0))],
        out_specs=pl.BlockSpec((1, tq, A_Q), lambda bi, i: (bi, i, 0)),
        out_shape=jax.ShapeDtypeStruct((b, s, A_Q), BF16),
        scratch_shapes=[pltpu.VMEM((tq + WINDOW, A_KV), BF16),
                        pltpu.VMEM((vr, tq + WINDOW), BF16)],
        compiler_params=pltpu.CompilerParams(
            dimension_semantics=("arbitrary", "arbitrary"), vmem_limit_bytes=VMEM_LIMIT),
        name="swa",
    )(sinks, aqt, ak, ak, avt, avt, sg)


def _diff_kernel(slopes_ref, lq1_ref, lk1_ref, lq2_ref, lk2_ref, subg_ref,
                 qt_ref, k_ref, vt_ref, sg_ref, o_ref,
                 qa_ref, s_ref, m_ref, acc_ref, *, lambda_init):
    i = pl.program_id(2)

    @pl.when(i == 0)
    def _():
        _diff_tile(slopes_ref, lq1_ref, lk1_ref, lq2_ref, lk2_ref, subg_ref,
                   qt_ref, k_ref, vt_ref, sg_ref, o_ref, qa_ref, s_ref, m_ref, acc_ref,
                   lambda_init=lambda_init, first_tile=True)

    @pl.when(i > 0)
    def _():
        _diff_tile(slopes_ref, lq1_ref, lk1_ref, lq2_ref, lk2_ref, subg_ref,
                   qt_ref, k_ref, vt_ref, sg_ref, o_ref, qa_ref, s_ref, m_ref, acc_ref,
                   lambda_init=lambda_init, first_tile=False)


def _diff_tile(slopes_ref, lq1_ref, lk1_ref, lq2_ref, lk2_ref, subg_ref,
               qt_ref, k_ref, vt_ref, sg_ref, o_ref,
               qa_ref, s_ref, m_ref, acc_ref, *, lambda_init, first_tile):
    tg, tk = TG_DIFF, TK_DIFF
    nchain = 2 * NG_DIFF
    hd = pl.program_id(1)
    i = pl.program_id(2)
    slope = slopes_ref[hd] * LOG2E

    def key_consts(n):
        kk = lax.broadcasted_iota(jnp.int32, (n, LANES), 0).astype(F32)
        lane = lax.broadcasted_iota(jnp.int32, (n, LANES), 1)
        ab = slope * kk
        ab_hi = ab.astype(BF16).astype(F32)
        ab_lo = ab - ab_hi
        zk = jnp.zeros((n, LANES), F32)
        bias1 = jnp.where(lane == HEAD_DIM, ab_hi,
                          jnp.where(lane == HEAD_DIM + 1, ab_lo, zk)).astype(BF16)
        bias2 = jnp.where(lane == 0, ab_hi, jnp.where(lane == 1, ab_lo, zk)).astype(BF16)
        return lane < HEAD_DIM, bias1, bias2

    kconst = {n: key_consts(n) for n in ((tg,) if first_tile else (tg, tk))}

    rowq = lax.broadcasted_iota(jnp.int32, (LANES, tg), 0)
    one = jnp.ones((LANES, tg), F32)
    zq = jnp.zeros((LANES, tg), F32)
    for g in range(NG_DIFF):
        qt = qt_ref[0, :, g * tg:(g + 1) * tg].astype(F32)
        qa_ref[2 * g] = jnp.where(rowq < HEAD_DIM, qt,
                                  jnp.where(rowq < HEAD_DIM + 2, one, zq)).astype(BF16)
        qa_ref[2 * g + 1] = jnp.where(rowq >= HEAD_DIM, qt,
                                      jnp.where(rowq < 2, one, zq)).astype(BF16)
    m_ref[...] = jnp.full(m_ref.shape, NEG_BIG, F32)
    acc_ref[...] = jnp.zeros(acc_ref.shape, F32)

    krow = lax.broadcasted_iota(jnp.int32, (tg, tg), 0)
    qcol = lax.broadcasted_iota(jnp.int32, (tg, tg), 1)
    causal = krow <= qcol

    def scores(buf, c, ka, n):
        s_ref[buf, c, 0:n, :] = jnp.dot(ka, qa_ref[c], preferred_element_type=F32)

    def consume(buf, c, vt, cj, n, masked):
        s = s_ref[buf, c, 0:n, :]
        if masked:
            s = jnp.where(causal, s, NEG_BIG)
        m_old = m_ref[c]
        m_new = jnp.maximum(m_old, jnp.max(s, axis=0, keepdims=True) + cj)
        alpha = jnp.exp2(m_old - m_new)
        p = jnp.exp2(s - (m_new - cj)).astype(BF16)
        acc_ref[c] = alpha * acc_ref[c] + jnp.dot(vt, p, preferred_element_type=F32)
        m_ref[c] = m_new

    def keys(start, n):
        k = k_ref[0, pl.ds(pl.multiple_of(start, n), n), :]
        lane_lo, bias1, bias2 = kconst[n]
        return jnp.where(lane_lo, k, bias1), jnp.where(lane_lo, bias2, k)

    def block_bias(start):
        return slope * start.astype(F32)

    lam = (jnp.exp(jnp.sum(lq1_ref[...] * lk1_ref[...], axis=-1, keepdims=True))
           - jnp.exp(jnp.sum(lq2_ref[...] * lk2_ref[...], axis=-1, keepdims=True))
           + lambda_init)

    def finish(g):
        a1 = acc_ref[2 * g]
        a2 = acc_ref[2 * g + 1]
        ot = (a1[0:LANES] / a1[LANES:LANES + 1]
              - lam * (a2[0:LANES] / a2[LANES:LANES + 1]))
        ot = ot * lax.rsqrt(jnp.mean(ot * ot, axis=0, keepdims=True) + SUBLN_EPS) * subg_ref[...]
        ot = ot * (1.0 - lambda_init)
        rows = slice(g * tg, (g + 1) * tg)
        o_ref[0, rows, :] = (ot.T * sg_ref[0, rows, :].astype(F32)).astype(BF16)

    base = i * (NG_DIFF * tg)
    ka = keys(base, tg)
    for c in range(nchain):
        scores(0, c, ka[c % 2], tg)
    ka_first = None if first_tile else keys(jnp.int32(0), tk)
    for jj in range(NG_DIFF):
        cur, nxt = jj % 2, 1 - jj % 2
        start = base + jj * tg
        vt = vt_ref[0, i * (NG_DIFF * tg // tk) + (jj * tg) // tk, :,
                    (jj * tg) % tk:(jj * tg) % tk + tg]
        cj = block_bias(start)
        ka = keys(start + tg, tg) if jj + 1 < NG_DIFF else None
        for g in range(jj, NG_DIFF):
            for c in (2 * g, 2 * g + 1):
                if g > jj:
                    scores(nxt, c, ka[c % 2], tg)
                consume(cur, c, vt, cj, tg, g == jj)
                if g == jj and not first_tile:
                    scores(0, c, ka_first[c % 2], tk)
            if g == jj and first_tile:
                finish(g)
    if first_tile:
        return

    nblk = i * (NG_DIFF * tg // tk)

    def step(j, cur, last):
        vt = vt_ref[0, j]
        cj = block_bias(j * tk)
        kan = None if last else keys((j + 1) * tk, tk)
        for g in range(NG_DIFF):
            for c in (2 * g, 2 * g + 1):
                if not last:
                    scores(1 - cur, c, kan[c % 2], tk)
                consume(cur, c, vt, cj, tk, False)
            if last:
                finish(g)

    def body(t, carry):
        step(2 * t, 0, False)
        step(2 * t + 1, 1, False)
        return carry

    lax.fori_loop(0, nblk // 2 - 1, body, 0)
    step(nblk - 2, 0, False)
    step(nblk - 1, 1, True)


def _diff(slopes, lq1, lk1, lq2, lk2, subg, bqt, bk, bvt, sg, lambda_init):
    b, s, _ = bk.shape
    tg, tk = TG_DIFF, TK_DIFF
    tq = NG_DIFF * tg
    nchain = 2 * NG_DIFF
    smem = pl.BlockSpec(memory_space=pltpu.SMEM)
    small = lambda shape: pl.BlockSpec(shape, lambda bi, h, i: (0, 0))
    return pl.pallas_call(
        functools.partial(_diff_kernel, lambda_init=lambda_init),
        grid=(b, DIFF_HEADS, s // tq),
        in_specs=[smem,
                  small((1, HEAD_DIM)), small((1, HEAD_DIM)),
                  small((1, HEAD_DIM)), small((1, HEAD_DIM)),
                  small((LANES, 1)),
                  pl.BlockSpec((1, LANES, tq), lambda bi, h, i: (bi, h, i)),
                  pl.BlockSpec((1, s, LANES), lambda bi, h, i: (bi, 0, h)),
                  pl.BlockSpec((1, s // tk, VROWS, tk), lambda bi, h, i: (bi, 0, h, 0)),
                  pl.BlockSpec((1, tq, LANES), lambda bi, h, i: (bi, i, A_Q // LANES + h))],
        out_specs=pl.BlockSpec((1, tq, LANES), lambda bi, h, i: (bi, i, h)),
        out_shape=jax.ShapeDtypeStruct((b, s, B_W), BF16),
        scratch_shapes=[pltpu.VMEM((nchain, LANES, tg), BF16),
                        pltpu.VMEM((2, nchain, tk, tg), F32),
                        pltpu.VMEM((nchain, 1, tg), F32),
                        pltpu.VMEM((nchain, VROWS, tg), F32)],
        compiler_params=pltpu.CompilerParams(
            dimension_semantics=("arbitrary", "arbitrary", "arbitrary"),
            vmem_limit_bytes=VMEM_LIMIT),
        name="diffattn",
    )(slopes, lq1, lk1, lq2, lk2, subg, bqt, bk, bvt, sg)


def _outproj_kernel(x_ref, ma_ref, mb_ref, w_ref, fg_ref, o_ref, *, final_norm):
    for r in range(0, TM_PROJ, ROWS_OUT):
        rows = slice(r, r + ROWS_OUT)
        y = (x_ref[rows, :]
             + jnp.dot(ma_ref[rows, :], w_ref[0:A_Q, :], preferred_element_type=F32)
             + jnp.dot(mb_ref[rows, :], w_ref[A_Q:MIX, :], preferred_element_type=F32))
        if final_norm:
            ms = jnp.mean(y * y, axis=-1, keepdims=True)
            y = y * lax.rsqrt(ms + RMS_EPS) * fg_ref[...]
        o_ref[rows, :] = y


def _outproj(xf, ma, mb, w, fg, final_norm):
    n, d = xf.shape
    tm = TM_PROJ
    row = lambda i: (i, 0)
    whole = lambda i: (0, 0)
    return pl.pallas_call(
        functools.partial(_outproj_kernel, final_norm=final_norm),
        grid=(n // tm,),
        in_specs=[pl.BlockSpec((tm, d), row),
                  pl.BlockSpec((tm, A_Q), row),
                  pl.BlockSpec((tm, B_W), row),
                  pl.BlockSpec((MIX, d), whole),
                  pl.BlockSpec((1, d), whole)],
        out_specs=pl.BlockSpec((tm, d), row),
        out_shape=jax.ShapeDtypeStruct((n, d), F32),
        compiler_params=pltpu.CompilerParams(
            dimension_semantics=("arbitrary",), vmem_limit_bytes=VMEM_LIMIT),
        name="outproj",
    )(xf, ma, mb, w, fg)


def kernel(x, norm_g, w_in, sinks, lambda_q1, lambda_k1, lambda_q2, lambda_k2,
           subln_g, w_out, final_g):
    b, s, d = x.shape
    depth = norm_g.shape[0]
    n = b * s
    diff_slopes = jnp.asarray(
        [2.0 ** (-8.0 * (h + 1) / DIFF_HEADS) for h in range(DIFF_HEADS)], F32)
    h3 = x
    for layer in range(depth):
        aqt, ak, avt, bqt, bk, bvt, sg = _inproj(
            h3, norm_g[layer].reshape(1, d), w_in[layer].astype(BF16))
        mixed_a = _swa(sinks[layer], aqt, ak, avt, sg)
        mixed_b = _diff(diff_slopes,
                        lambda_q1[layer].reshape(1, HEAD_DIM), lambda_k1[layer].reshape(1, HEAD_DIM),
                        lambda_q2[layer].reshape(1, HEAD_DIM), lambda_k2[layer].reshape(1, HEAD_DIM),
                        subln_g[layer].reshape(LANES, 1),
                        bqt, bk, bvt, sg, _lambda_init(layer))
        hf = _outproj(h3.reshape(n, d), mixed_a.reshape(n, A_Q), mixed_b.reshape(n, B_W),
                      w_out[layer].astype(BF16), final_g.reshape(1, d),
                      final_norm=(layer == depth - 1))
        h3 = hf.reshape(b, s, d)
    return h3
```

```python
import functools
import math

import jax
import jax.numpy as jnp
import numpy as np
from jax import lax
from jax.experimental import pallas as pl
from jax.experimental.pallas import tpu as pltpu

F32 = jnp.float32
BF16 = jnp.bfloat16

HEAD_DIM = 64
LANES = 128
SWA_Q_HEADS = 8
SWA_KV_HEADS = 2
WINDOW = 128
DIFF_HEADS = 4
RMS_EPS = 1e-6
SUBLN_EPS = 1e-5
NEG_BIG = -1e30
LOG2E = math.log2(math.e)

A_Q = SWA_Q_HEADS * HEAD_DIM
A_KV = SWA_KV_HEADS * HEAD_DIM
B_W = DIFF_HEADS * 2 * HEAD_DIM
MIX = A_Q + B_W
OFF_AK = A_Q
OFF_AV = OFF_AK + A_KV
OFF_BQ = OFF_AV + A_KV
OFF_BK = OFF_BQ + B_W
OFF_BV = OFF_BK + B_W
OFF_GATE = OFF_BV + B_W
IN_COLS = OFF_GATE + MIX

TM_PROJ = 512
ROWS_OUT = 256
TQ_SWA = 512
TG_DIFF = 256
NG_DIFF = 4
TK_DIFF = 512
VROWS = LANES + 16
VROWS_A = HEAD_DIM + 16
VMEM_LIMIT = 56 * 1024 * 1024


def _lambda_init(layer_idx):
    return 0.8 - 0.6 * math.exp(-0.3 * layer_idx)


def _nt_dot(a, b):
    return lax.dot_general(a, b, (((1,), (1,)), ((), ())), preferred_element_type=F32)


def _inproj_kernel(x_ref, g_ref, w_ref,
                   aqt_ref, ak_ref, avt_ref, bqt_ref, bk_ref, bvt_ref, sg_ref):
    x = x_ref[0]
    ms = jnp.mean(x * x, axis=-1, keepdims=True)
    h = (x * lax.rsqrt(ms + RMS_EPS) * g_ref[...]).astype(BF16)
    qscale = HEAD_DIM ** -0.5 * LOG2E

    def tok(lo, hi):
        return jnp.dot(h, w_ref[:, lo:hi], preferred_element_type=F32)

    def feat(lo, hi):
        return lax.dot_general(w_ref[:, lo:hi], h, (((0,), (1,)), ((), ())),
                               preferred_element_type=F32)

    ak_ref[0] = tok(OFF_AK, OFF_AV).astype(BF16)
    bk_ref[0] = tok(OFF_BK, OFF_BV).astype(BF16)
    gate = tok(OFF_GATE, IN_COLS)
    sg_ref[0] = (gate * (1.0 / (1.0 + jnp.exp(-gate)))).astype(BF16)

    aqt_ref[0] = (feat(0, OFF_AK) * qscale).astype(BF16)
    avt = feat(OFF_AV, OFF_BQ).astype(BF16)
    for g in range(SWA_KV_HEADS):
        avt_ref[0, g * VROWS_A:g * VROWS_A + HEAD_DIM, :] = avt[g * HEAD_DIM:(g + 1) * HEAD_DIM]
        avt_ref[0, g * VROWS_A + HEAD_DIM:(g + 1) * VROWS_A, :] = jnp.ones(
            (VROWS_A - HEAD_DIM, TM_PROJ), BF16)
    bqt_ref[0] = (feat(OFF_BQ, OFF_BK) * qscale).astype(BF16)
    bvt = feat(OFF_BV, OFF_GATE).astype(BF16)
    ones = jnp.ones((VROWS - LANES, TK_DIFF), BF16)
    for c in range(TM_PROJ // TK_DIFF):
        for hd in range(DIFF_HEADS):
            bvt_ref[0, c, hd * VROWS:hd * VROWS + LANES, :] = (
                bvt[hd * LANES:(hd + 1) * LANES, c * TK_DIFF:(c + 1) * TK_DIFF])
            bvt_ref[0, c, hd * VROWS + LANES:(hd + 1) * VROWS, :] = ones


def _inproj(x, g, w):
    b, s, d = x.shape
    tm = TM_PROJ
    row = lambda bi, i: (bi, i, 0)
    col = lambda bi, i: (bi, 0, i)
    whole = lambda bi, i: (0, 0)
    bsd = lambda wd: jax.ShapeDtypeStruct((b, s, wd), BF16)
    bds = lambda wd: jax.ShapeDtypeStruct((b, wd, s), BF16)
    return pl.pallas_call(
        _inproj_kernel,
        grid=(b, s // tm),
        in_specs=[pl.BlockSpec((1, tm, d), row),
                  pl.BlockSpec((1, d), whole),
                  pl.BlockSpec(w.shape, whole)],
        out_specs=[pl.BlockSpec((1, A_Q, tm), col),
                   pl.BlockSpec((1, tm, A_KV), row),
                   pl.BlockSpec((1, SWA_KV_HEADS * VROWS_A, tm), col),
                   pl.BlockSpec((1, B_W, tm), col),
                   pl.BlockSpec((1, tm, B_W), row),
                   pl.BlockSpec((1, tm // TK_DIFF, DIFF_HEADS * VROWS, TK_DIFF), lambda bi, i: (bi, i, 0, 0)),
                   pl.BlockSpec((1, tm, MIX), row)],
        out_shape=[bds(A_Q), bsd(A_KV), bds(SWA_KV_HEADS * VROWS_A), bds(B_W), bsd(B_W),
                   jax.ShapeDtypeStruct((b, s // TK_DIFF, DIFF_HEADS * VROWS, TK_DIFF), BF16),
                   bsd(MIX)],
        compiler_params=pltpu.CompilerParams(
            dimension_semantics=("arbitrary", "arbitrary"), vmem_limit_bytes=VMEM_LIMIT),
        name="inproj",
    )(x, g, w)


def _swa_out_kernel(sinks_ref, qt_ref, kp_ref, kc_ref, vtp_ref, vtc_ref, sg_ref,
                    x_ref, mb_ref, wout_ref, fg_ref, o_ref, kcat, vtcat, ma_scr, *, final_norm):
    w_ = WINDOW
    i = pl.program_id(1)
    group = SWA_Q_HEADS // SWA_KV_HEADS
    nq = group * w_
    kcat[0:w_, :] = kp_ref[0]
    kcat[w_:, :] = kc_ref[0]
    vtcat[:, 0:w_] = vtp_ref[0]
    vtcat[:, w_:] = vtc_ref[0]

    kidx = lax.broadcasted_iota(jnp.int32, (2 * w_, nq), 0)
    qidx = lax.broadcasted_iota(jnp.int32, (2 * w_, nq), 1) & (w_ - 1)
    in_cur = kidx >= w_
    band = (in_cur & (kidx - w_ <= qidx)) | ((kidx < w_) & (kidx > qidx))
    first_band = band & ((i > 0) | in_cur)

    lane = lax.broadcasted_iota(jnp.int32, (2 * w_, LANES), 1)
    kpos = lax.broadcasted_iota(jnp.int32, (2 * w_, LANES), 0).astype(F32)
    zk = jnp.zeros((2 * w_, LANES), F32)
    kmask = (lane < HEAD_DIM, lane >= HEAD_DIM)
    kpos_lanes = (jnp.where((lane == HEAD_DIM) | (lane == HEAD_DIM + 1), kpos, zk).astype(BF16),
                  jnp.where(lane < 2, kpos, zk).astype(BF16))

    hrow = lax.broadcasted_iota(jnp.int32, (HEAD_DIM, nq), 0)
    hcol = lax.broadcasted_iota(jnp.int32, (HEAD_DIM, nq), 1)

    def slope_rows(g):
        out = jnp.zeros((HEAD_DIM, nq), F32)
        for u in range(group):
            sl = 2.0 ** (-8.0 * (g * group + u + 1) / SWA_Q_HEADS) * LOG2E
            hi = float(np.asarray(sl, dtype=BF16).astype(np.float32))
            lo = sl - hi
            in_head = (hcol >= u * w_) & (hcol < (u + 1) * w_)
            out = jnp.where(in_head & (hrow == 0), hi, jnp.where(in_head & (hrow == 1), lo, out))
        return out.astype(BF16)

    srows = [slope_rows(g) for g in range(SWA_KV_HEADS)]

    qpos = (lax.broadcasted_iota(jnp.int32, (1, w_), 1) + w_).astype(F32)

    def sink_row(g):
        return jnp.concatenate(
            [(sinks_ref[g * group + u] + 2.0 ** (-8.0 * (g * group + u + 1) / SWA_Q_HEADS) * qpos) * LOG2E
             for u in range(group)], axis=1)

    sinkv = [sink_row(g) for g in range(SWA_KV_HEADS)]

    def scores(w, g):
        keys = kcat[w * w_:(w + 2) * w_, :]
        ka = jnp.where(kmask[g], keys, kpos_lanes[g])
        qh = jnp.concatenate(
            [qt_ref[0, (g * group + u) * HEAD_DIM:(g * group + u + 1) * HEAD_DIM, w * w_:(w + 1) * w_]
             for u in range(group)], axis=1)
        wq = jnp.concatenate([qh, srows[g]] if g == 0 else [srows[g], qh], axis=0)
        return jnp.dot(ka, wq, preferred_element_type=F32)

    def consume(w, g, s):
        s = jnp.where(first_band if w == 0 else band, s, NEG_BIG)
        m = jnp.maximum(jnp.max(s, axis=0, keepdims=True), sinkv[g])
        p = jnp.exp2(s - m).astype(BF16)
        vt = vtcat[g * VROWS_A:(g + 1) * VROWS_A, w * w_:(w + 2) * w_]
        o = jnp.dot(vt, p, preferred_element_type=F32)
        denom = o[HEAD_DIM:HEAD_DIM + 1] + jnp.exp2(sinkv[g] - m)
        ot = o[0:HEAD_DIM] / denom
        rows = slice(w * w_, (w + 1) * w_)
        for pair in range(group // 2):
            cols = slice((g * (group // 2) + pair) * LANES, (g * (group // 2) + pair + 1) * LANES)
            two = jnp.concatenate([ot[:, (2 * pair) * w_:(2 * pair + 1) * w_],
                                   ot[:, (2 * pair + 1) * w_:(2 * pair + 2) * w_]], axis=0)
            ma_scr[rows, cols] = (two.T * sg_ref[0, rows, cols].astype(F32)).astype(BF16)

    chains = [(w, g) for w in range(TQ_SWA // w_) for g in range(SWA_KV_HEADS)]
    ahead = 3
    pending = [scores(*ch) for ch in chains[:ahead]]
    for n, (w, g) in enumerate(chains):
        if n + ahead < len(chains):
            pending.append(scores(*chains[n + ahead]))
        consume(w, g, pending.pop(0))

    for r in range(0, TQ_SWA, ROWS_OUT):
        rows = slice(r, r + ROWS_OUT)
        y = (x_ref[0, rows, :]
             + jnp.dot(ma_scr[rows, :], wout_ref[0:A_Q, :], preferred_element_type=F32)
             + jnp.dot(mb_ref[0, rows, :], wout_ref[A_Q:MIX, :], preferred_element_type=F32))
        if final_norm:
            ms = jnp.mean(y * y, axis=-1, keepdims=True)
            y = y * lax.rsqrt(ms + RMS_EPS) * fg_ref[...]
        o_ref[0, rows, :] = y


def _swa_out(sinks, aqt, ak, avt, sg, x, mixed_b, wout, fg, final_norm):
    b, s, d = x.shape
    tq = TQ_SWA
    wpt = tq // WINDOW
    vr = SWA_KV_HEADS * VROWS_A
    prev_w = lambda i: jnp.maximum(i * wpt - 1, 0)
    tile = lambda bi, i: (bi, i, 0)
    whole = lambda bi, i: (0, 0)
    return pl.pallas_call(
        functools.partial(_swa_out_kernel, final_norm=final_norm),
        grid=(b, s // tq),
        in_specs=[pl.BlockSpec(memory_space=pltpu.SMEM),
                  pl.BlockSpec((1, A_Q, tq), lambda bi, i: (bi, 0, i)),
                  pl.BlockSpec((1, WINDOW, A_KV), lambda bi, i: (bi, prev_w(i), 0)),
                  pl.BlockSpec((1, tq, A_KV), tile),
                  pl.BlockSpec((1, vr, WINDOW), lambda bi, i: (bi, 0, prev_w(i))),
                  pl.BlockSpec((1, vr, tq), lambda bi, i: (bi, 0, i)),
                  pl.BlockSpec((1, tq, A_Q), tile),
                  pl.BlockSpec((1, tq, d), tile),
                  pl.BlockSpec((1, tq, B_W), tile),
                  pl.BlockSpec((MIX, d), whole),
                  pl.BlockSpec((1, d), whole)],
        out_specs=pl.BlockSpec((1, tq, d), tile),
        out_shape=jax.ShapeDtypeStruct((b, s, d), F32),
        scratch_shapes=[pltpu.VMEM((tq + WINDOW, A_KV), BF16),
                        pltpu.VMEM((vr, tq + WINDOW), BF16),
                        pltpu.VMEM((tq, A_Q), BF16)],
        compiler_params=pltpu.CompilerParams(
            dimension_semantics=("arbitrary", "arbitrary"), vmem_limit_bytes=VMEM_LIMIT),
        name="swa_out",
    )(sinks, aqt, ak, ak, avt, avt, sg, x, mixed_b, wout, fg)


def _diff_kernel(slopes_ref, lq1_ref, lk1_ref, lq2_ref, lk2_ref, subg_ref,
                 qt_ref, k_ref, vt_ref, sg_ref, o_ref,
                 qa_ref, s_ref, m_ref, acc_ref, *, lambda_init):
    i = pl.program_id(2)

    @pl.when(i == 0)
    def _():
        _diff_tile(slopes_ref, lq1_ref, lk1_ref, lq2_ref, lk2_ref, subg_ref,
                   qt_ref, k_ref, vt_ref, sg_ref, o_ref, qa_ref, s_ref, m_ref, acc_ref,
                   lambda_init=lambda_init, first_tile=True)

    @pl.when(i > 0)
    def _():
        _diff_tile(slopes_ref, lq1_ref, lk1_ref, lq2_ref, lk2_ref, subg_ref,
                   qt_ref, k_ref, vt_ref, sg_ref, o_ref, qa_ref, s_ref, m_ref, acc_ref,
                   lambda_init=lambda_init, first_tile=False)


def _diff_tile(slopes_ref, lq1_ref, lk1_ref, lq2_ref, lk2_ref, subg_ref,
               qt_ref, k_ref, vt_ref, sg_ref, o_ref,
               qa_ref, s_ref, m_ref, acc_ref, *, lambda_init, first_tile):
    tg, tk = TG_DIFF, TK_DIFF
    nchain = 2 * NG_DIFF
    hd = pl.program_id(1)
    i = pl.program_id(2)
    slope = slopes_ref[hd] * LOG2E

    def key_consts(n):
        kk = lax.broadcasted_iota(jnp.int32, (n, LANES), 0).astype(F32)
        lane = lax.broadcasted_iota(jnp.int32, (n, LANES), 1)
        ab = slope * kk
        ab_hi = ab.astype(BF16).astype(F32)
        ab_lo = ab - ab_hi
        zk = jnp.zeros((n, LANES), F32)
        bias1 = jnp.where(lane == HEAD_DIM, ab_hi,
                          jnp.where(lane == HEAD_DIM + 1, ab_lo, zk)).astype(BF16)
        bias2 = jnp.where(lane == 0, ab_hi, jnp.where(lane == 1, ab_lo, zk)).astype(BF16)
        return lane < HEAD_DIM, bias1, bias2

    kconst = {n: key_consts(n) for n in ((tg,) if first_tile else (tg, tk))}

    rowq = lax.broadcasted_iota(jnp.int32, (LANES, tg), 0)
    one = jnp.ones((LANES, tg), F32)
    zq = jnp.zeros((LANES, tg), F32)
    for g in range(NG_DIFF):
        qt = qt_ref[0, :, g * tg:(g + 1) * tg].astype(F32)
        qa_ref[2 * g] = jnp.where(rowq < HEAD_DIM, qt,
                                  jnp.where(rowq < HEAD_DIM + 2, one, zq)).astype(BF16)
        qa_ref[2 * g + 1] = jnp.where(rowq >= HEAD_DIM, qt,
                                      jnp.where(rowq < 2, one, zq)).astype(BF16)
    m_ref[...] = jnp.full(m_ref.shape, NEG_BIG, F32)
    acc_ref[...] = jnp.zeros(acc_ref.shape, F32)

    krow = lax.broadcasted_iota(jnp.int32, (tg, tg), 0)
    qcol = lax.broadcasted_iota(jnp.int32, (tg, tg), 1)
    causal = krow <= qcol

    def scores(buf, c, ka, n):
        s_ref[buf, c, 0:n, :] = jnp.dot(ka, qa_ref[c], preferred_element_type=F32)

    def consume(buf, c, vt, cj, n, masked):
        s = s_ref[buf, c, 0:n, :]
        if masked:
            s = jnp.where(causal, s, NEG_BIG)
        m_old = m_ref[c]
        m_new = jnp.maximum(m_old, jnp.max(s, axis=0, keepdims=True) + cj)
        alpha = jnp.exp2(m_old - m_new)
        p = jnp.exp2(s - (m_new - cj)).astype(BF16)
        acc_ref[c] = alpha * acc_ref[c] + jnp.dot(vt, p, preferred_element_type=F32)
        m_ref[c] = m_new

    def keys(start, n):
        k = k_ref[0, pl.ds(pl.multiple_of(start, n), n), :]
        lane_lo, bias1, bias2 = kconst[n]
        return jnp.where(lane_lo, k, bias1), jnp.where(lane_lo, bias2, k)

    def block_bias(start):
        return slope * start.astype(F32)

    lam = (jnp.exp(jnp.sum(lq1_ref[...] * lk1_ref[...], axis=-1, keepdims=True))
           - jnp.exp(jnp.sum(lq2_ref[...] * lk2_ref[...], axis=-1, keepdims=True))
           + lambda_init)

    def finish(g):
        a1 = acc_ref[2 * g]
        a2 = acc_ref[2 * g + 1]
        ot = (a1[0:LANES] / a1[LANES:LANES + 1]
              - lam * (a2[0:LANES] / a2[LANES:LANES + 1]))
        ot = ot * lax.rsqrt(jnp.mean(ot * ot, axis=0, keepdims=True) + SUBLN_EPS) * subg_ref[...]
        ot = ot * (1.0 - lambda_init)
        rows = slice(g * tg, (g + 1) * tg)
        o_ref[0, rows, :] = (ot.T * sg_ref[0, rows, :].astype(F32)).astype(BF16)

    base = i * (NG_DIFF * tg)
    ka = keys(base, tg)
    for c in range(nchain):
        scores(0, c, ka[c % 2], tg)
    ka_first = None if first_tile else keys(jnp.int32(0), tk)
    for jj in range(NG_DIFF):
        cur, nxt = jj % 2, 1 - jj % 2
        start = base + jj * tg
        vt = vt_ref[0, i * (NG_DIFF * tg // tk) + (jj * tg) // tk, :,
                    (jj * tg) % tk:(jj * tg) % tk + tg]
        cj = block_bias(start)
        ka = keys(start + tg, tg) if jj + 1 < NG_DIFF else None
        for g in range(jj, NG_DIFF):
            for c in (2 * g, 2 * g + 1):
                if g > jj:
                    scores(nxt, c, ka[c % 2], tg)
                consume(cur, c, vt, cj, tg, g == jj)
                if g == jj and not first_tile:
                    scores(0, c, ka_first[c % 2], tk)
            if g == jj and first_tile:
                finish(g)
    if first_tile:
        return

    nblk = i * (NG_DIFF * tg // tk)

    def step(j, cur, last):
        vt = vt_ref[0, j]
        cj = block_bias(j * tk)
        kan = None if last else keys((j + 1) * tk, tk)
        for g in range(NG_DIFF):
            for c in (2 * g, 2 * g + 1):
                if not last:
                    scores(1 - cur, c, kan[c % 2], tk)
                consume(cur, c, vt, cj, tk, False)
            if last:
                finish(g)

    def body(t, carry):
        step(2 * t, 0, False)
        step(2 * t + 1, 1, False)
        return carry

    lax.fori_loop(0, nblk // 2 - 1, body, 0)
    step(nblk - 2, 0, False)
    step(nblk - 1, 1, True)


def _diff(slopes, lq1, lk1, lq2, lk2, subg, bqt, bk, bvt, sg, lambda_init):
    b, s, _ = bk.shape
    tg, tk = TG_DIFF, TK_DIFF
    tq = NG_DIFF * tg
    nchain = 2 * NG_DIFF
    smem = pl.BlockSpec(memory_space=pltpu.SMEM)
    small = lambda shape: pl.BlockSpec(shape, lambda bi, h, i: (0, 0))
    return pl.pallas_call(
        functools.partial(_diff_kernel, lambda_init=lambda_init),
        grid=(b, DIFF_HEADS, s // tq),
        in_specs=[smem,
                  small((1, HEAD_DIM)), small((1, HEAD_DIM)),
                  small((1, HEAD_DIM)), small((1, HEAD_DIM)),
                  small((LANES, 1)),
                  pl.BlockSpec((1, LANES, tq), lambda bi, h, i: (bi, h, i)),
                  pl.BlockSpec((1, s, LANES), lambda bi, h, i: (bi, 0, h)),
                  pl.BlockSpec((1, s // tk, VROWS, tk), lambda bi, h, i: (bi, 0, h, 0)),
                  pl.BlockSpec((1, tq, LANES), lambda bi, h, i: (bi, i, A_Q // LANES + h))],
        out_specs=pl.BlockSpec((1, tq, LANES), lambda bi, h, i: (bi, i, h)),
        out_shape=jax.ShapeDtypeStruct((b, s, B_W), BF16),
        scratch_shapes=[pltpu.VMEM((nchain, LANES, tg), BF16),
                        pltpu.VMEM((2, nchain, tk, tg), F32),
                        pltpu.VMEM((nchain, 1, tg), F32),
                        pltpu.VMEM((nchain, VROWS, tg), F32)],
        compiler_params=pltpu.CompilerParams(
            dimension_semantics=("arbitrary", "arbitrary", "arbitrary"),
            vmem_limit_bytes=VMEM_LIMIT),
        name="diffattn",
    )(slopes, lq1, lk1, lq2, lk2, subg, bqt, bk, bvt, sg)


def kernel(x, norm_g, w_in, sinks, lambda_q1, lambda_k1, lambda_q2, lambda_k2,
           subln_g, w_out, final_g):
    b, s, d = x.shape
    depth = norm_g.shape[0]
    diff_slopes = jnp.asarray(
        [2.0 ** (-8.0 * (h + 1) / DIFF_HEADS) for h in range(DIFF_HEADS)], F32)
    h3 = x
    for layer in range(depth):
        aqt, ak, avt, bqt, bk, bvt, sg = _inproj(
            h3, norm_g[layer].reshape(1, d), w_in[layer].astype(BF16))
        mixed_b = _diff(diff_slopes,
                        lambda_q1[layer].reshape(1, HEAD_DIM), lambda_k1[layer].reshape(1, HEAD_DIM),
                        lambda_q2[layer].reshape(1, HEAD_DIM), lambda_k2[layer].reshape(1, HEAD_DIM),
                        subln_g[layer].reshape(LANES, 1),
                        bqt, bk, bvt, sg, _lambda_init(layer))
        h3 = _swa_out(sinks[layer], aqt, ak, avt, sg, h3, mixed_b,
                      w_out[layer].astype(BF16), final_g.reshape(1, d),
                      final_norm=(layer == depth - 1))
    return h3
```

```python
import functools
import math

import jax
import jax.numpy as jnp
import numpy as np
from jax import lax
from jax.experimental import pallas as pl
from jax.experimental.pallas import tpu as pltpu

F32 = jnp.float32
BF16 = jnp.bfloat16

HEAD_DIM = 64
LANES = 128
SWA_Q_HEADS = 8
SWA_KV_HEADS = 2
WINDOW = 128
DIFF_HEADS = 4
RMS_EPS = 1e-6
SUBLN_EPS = 1e-5
NEG_BIG = -1e30
LOG2E = math.log2(math.e)

A_Q = SWA_Q_HEADS * HEAD_DIM
A_KV = SWA_KV_HEADS * HEAD_DIM
B_W = DIFF_HEADS * 2 * HEAD_DIM
MIX = A_Q + B_W
OFF_AK = A_Q
OFF_AV = OFF_AK + A_KV
OFF_BQ = OFF_AV + A_KV
OFF_BK = OFF_BQ + B_W
OFF_BV = OFF_BK + B_W
OFF_GATE = OFF_BV + B_W
IN_COLS = OFF_GATE + MIX

TM_PROJ = 1024
ROWS_OUT = 256
TQ_SWA = 1024
TG_DIFF = 256
NG_DIFF = 4
TK_DIFF = 512
VROWS = LANES + 16
VROWS_A = HEAD_DIM + 16
VMEM_LIMIT = 56 * 1024 * 1024


def _lambda_init(layer_idx):
    return 0.8 - 0.6 * math.exp(-0.3 * layer_idx)


def _nt_dot(a, b):
    return lax.dot_general(a, b, (((1,), (1,)), ((), ())), preferred_element_type=F32)


def _inproj_kernel(x_ref, g_ref, w_ref,
                   aqt_ref, ak_ref, avt_ref, bqt_ref, bk_ref, bvt_ref, sg_ref):
    x = x_ref[0]
    ms = jnp.mean(x * x, axis=-1, keepdims=True)
    h = (x * lax.rsqrt(ms + RMS_EPS) * g_ref[...]).astype(BF16)
    qscale = HEAD_DIM ** -0.5 * LOG2E

    def tok(lo, hi):
        return jnp.dot(h, w_ref[:, lo:hi], preferred_element_type=F32)

    def feat(lo, hi):
        return lax.dot_general(w_ref[:, lo:hi], h, (((0,), (1,)), ((), ())),
                               preferred_element_type=F32)

    ak_ref[0] = tok(OFF_AK, OFF_AV).astype(BF16)
    bk_ref[0] = tok(OFF_BK, OFF_BV).astype(BF16)
    gate = tok(OFF_GATE, IN_COLS)
    sg_ref[0] = (gate * (1.0 / (1.0 + jnp.exp(-gate)))).astype(BF16)

    aqt_ref[0] = (feat(0, OFF_AK) * qscale).astype(BF16)
    avt = feat(OFF_AV, OFF_BQ).astype(BF16)
    for g in range(SWA_KV_HEADS):
        avt_ref[0, g * VROWS_A:g * VROWS_A + HEAD_DIM, :] = avt[g * HEAD_DIM:(g + 1) * HEAD_DIM]
        avt_ref[0, g * VROWS_A + HEAD_DIM:(g + 1) * VROWS_A, :] = jnp.ones(
            (VROWS_A - HEAD_DIM, TM_PROJ), BF16)
    bqt_ref[0] = (feat(OFF_BQ, OFF_BK) * qscale).astype(BF16)
    bvt = feat(OFF_BV, OFF_GATE).astype(BF16)
    ones = jnp.ones((VROWS - LANES, TK_DIFF), BF16)
    for c in range(TM_PROJ // TK_DIFF):
        for hd in range(DIFF_HEADS):
            bvt_ref[0, c, hd * VROWS:hd * VROWS + LANES, :] = (
                bvt[hd * LANES:(hd + 1) * LANES, c * TK_DIFF:(c + 1) * TK_DIFF])
            bvt_ref[0, c, hd * VROWS + LANES:(hd + 1) * VROWS, :] = ones


def _inproj(x, g, w):
    b, s, d = x.shape
    tm = TM_PROJ
    row = lambda bi, i: (bi, i, 0)
    col = lambda bi, i: (bi, 0, i)
    whole = lambda bi, i: (0, 0)
    bsd = lambda wd: jax.ShapeDtypeStruct((b, s, wd), BF16)
    bds = lambda wd: jax.ShapeDtypeStruct((b, wd, s), BF16)
    return pl.pallas_call(
        _inproj_kernel,
        grid=(b, s // tm),
        in_specs=[pl.BlockSpec((1, tm, d), row),
                  pl.BlockSpec((1, d), whole),
                  pl.BlockSpec(w.shape, whole)],
        out_specs=[pl.BlockSpec((1, A_Q, tm), col),
                   pl.BlockSpec((1, tm, A_KV), row),
                   pl.BlockSpec((1, SWA_KV_HEADS * VROWS_A, tm), col),
                   pl.BlockSpec((1, B_W, tm), col),
                   pl.BlockSpec((1, tm, B_W), row),
                   pl.BlockSpec((1, tm // TK_DIFF, DIFF_HEADS * VROWS, TK_DIFF), lambda bi, i: (bi, i, 0, 0)),
                   pl.BlockSpec((1, tm, MIX), row)],
        out_shape=[bds(A_Q), bsd(A_KV), bds(SWA_KV_HEADS * VROWS_A), bds(B_W), bsd(B_W),
                   jax.ShapeDtypeStruct((b, s // TK_DIFF, DIFF_HEADS * VROWS, TK_DIFF), BF16),
                   bsd(MIX)],
        compiler_params=pltpu.CompilerParams(
            dimension_semantics=("arbitrary", "arbitrary"), vmem_limit_bytes=VMEM_LIMIT),
        name="inproj",
    )(x, g, w)


def _swa_out_kernel(sinks_ref, qt_ref, kp_ref, kc_ref, vtp_ref, vtc_ref, sg_ref,
                    x_ref, mb_ref, wout_ref, fg_ref, o_ref, kcat, vtcat, ma_scr, *, final_norm):
    w_ = WINDOW
    i = pl.program_id(1)
    group = SWA_Q_HEADS // SWA_KV_HEADS
    nq = group * w_
    kcat[0:w_, :] = kp_ref[0]
    kcat[w_:, :] = kc_ref[0]
    vtcat[:, 0:w_] = vtp_ref[0]
    vtcat[:, w_:] = vtc_ref[0]

    kidx = lax.broadcasted_iota(jnp.int32, (2 * w_, nq), 0)
    qidx = lax.broadcasted_iota(jnp.int32, (2 * w_, nq), 1) & (w_ - 1)
    in_cur = kidx >= w_
    band = (in_cur & (kidx - w_ <= qidx)) | ((kidx < w_) & (kidx > qidx))
    first_band = band & ((i > 0) | in_cur)

    lane = lax.broadcasted_iota(jnp.int32, (2 * w_, LANES), 1)
    kpos = lax.broadcasted_iota(jnp.int32, (2 * w_, LANES), 0).astype(F32)
    zk = jnp.zeros((2 * w_, LANES), F32)
    kmask = (lane < HEAD_DIM, lane >= HEAD_DIM)
    kpos_lanes = (jnp.where((lane == HEAD_DIM) | (lane == HEAD_DIM + 1), kpos, zk).astype(BF16),
                  jnp.where(lane < 2, kpos, zk).astype(BF16))

    hrow = lax.broadcasted_iota(jnp.int32, (HEAD_DIM, nq), 0)
    hcol = lax.broadcasted_iota(jnp.int32, (HEAD_DIM, nq), 1)

    def slope_rows(g):
        out = jnp.zeros((HEAD_DIM, nq), F32)
        for u in range(group):
            sl = 2.0 ** (-8.0 * (g * group + u + 1) / SWA_Q_HEADS) * LOG2E
            hi = float(np.asarray(sl, dtype=BF16).astype(np.float32))
            lo = sl - hi
            in_head = (hcol >= u * w_) & (hcol < (u + 1) * w_)
            out = jnp.where(in_head & (hrow == 0), hi, jnp.where(in_head & (hrow == 1), lo, out))
        return out.astype(BF16)

    srows = [slope_rows(g) for g in range(SWA_KV_HEADS)]

    qpos = (lax.broadcasted_iota(jnp.int32, (1, w_), 1) + w_).astype(F32)

    def sink_row(g):
        return jnp.concatenate(
            [(sinks_ref[g * group + u] + 2.0 ** (-8.0 * (g * group + u + 1) / SWA_Q_HEADS) * qpos) * LOG2E
             for u in range(group)], axis=1)

    sinkv = [sink_row(g) for g in range(SWA_KV_HEADS)]

    def scores(w, g):
        keys = kcat[w * w_:(w + 2) * w_, :]
        ka = jnp.where(kmask[g], keys, kpos_lanes[g])
        qh = jnp.concatenate(
            [qt_ref[0, (g * group + u) * HEAD_DIM:(g * group + u + 1) * HEAD_DIM, w * w_:(w + 1) * w_]
             for u in range(group)], axis=1)
        wq = jnp.concatenate([qh, srows[g]] if g == 0 else [srows[g], qh], axis=0)
        return jnp.dot(ka, wq, preferred_element_type=F32)

    def consume(w, g, s):
        s = jnp.where(first_band if w == 0 else band, s, NEG_BIG)
        m = jnp.maximum(jnp.max(s, axis=0, keepdims=True), sinkv[g])
        p = jnp.exp2(s - m).astype(BF16)
        vt = vtcat[g * VROWS_A:(g + 1) * VROWS_A, w * w_:(w + 2) * w_]
        o = jnp.dot(vt, p, preferred_element_type=F32)
        denom = o[HEAD_DIM:HEAD_DIM + 1] + jnp.exp2(sinkv[g] - m)
        ot = o[0:HEAD_DIM] / denom
        rows = slice(w * w_, (w + 1) * w_)
        for pair in range(group // 2):
            cols = slice((g * (group // 2) + pair) * LANES, (g * (group // 2) + pair + 1) * LANES)
            two = jnp.concatenate([ot[:, (2 * pair) * w_:(2 * pair + 1) * w_],
                                   ot[:, (2 * pair + 1) * w_:(2 * pair + 2) * w_]], axis=0)
            ma_scr[rows, cols] = (two.T * sg_ref[0, rows, cols].astype(F32)).astype(BF16)

    chains = [(w, g) for w in range(TQ_SWA // w_) for g in range(SWA_KV_HEADS)]
    ahead = 3
    pending = [scores(*ch) for ch in chains[:ahead]]
    for n, (w, g) in enumerate(chains):
        if n + ahead < len(chains):
            pending.append(scores(*chains[n + ahead]))
        consume(w, g, pending.pop(0))

    for r in range(0, TQ_SWA, ROWS_OUT):
        rows = slice(r, r + ROWS_OUT)
        y = (x_ref[0, rows, :]
             + jnp.dot(ma_scr[rows, :], wout_ref[0:A_Q, :], preferred_element_type=F32)
             + jnp.dot(mb_ref[0, rows, :], wout_ref[A_Q:MIX, :], preferred_element_type=F32))
        if final_norm:
            ms = jnp.mean(y * y, axis=-1, keepdims=True)
            y = y * lax.rsqrt(ms + RMS_EPS) * fg_ref[...]
        o_ref[0, rows, :] = y


def _swa_out(sinks, aqt, ak, avt, sg, x, mixed_b, wout, fg, final_norm):
    b, s, d = x.shape
    tq = TQ_SWA
    wpt = tq // WINDOW
    vr = SWA_KV_HEADS * VROWS_A
    prev_w = lambda i: jnp.maximum(i * wpt - 1, 0)
    tile = lambda bi, i: (bi, i, 0)
    whole = lambda bi, i: (0, 0)
    return pl.pallas_call(
        functools.partial(_swa_out_kernel, final_norm=final_norm),
        grid=(b, s // tq),
        in_specs=[pl.BlockSpec(memory_space=pltpu.SMEM),
                  pl.BlockSpec((1, A_Q, tq), lambda bi, i: (bi, 0, i)),
                  pl.BlockSpec((1, WINDOW, A_KV), lambda bi, i: (bi, prev_w(i), 0)),
                  pl.BlockSpec((1, tq, A_KV), tile),
                  pl.BlockSpec((1, vr, WINDOW), lambda bi, i: (bi, 0, prev_w(i))),
                  pl.BlockSpec((1, vr, tq), lambda bi, i: (bi, 0, i)),
                  pl.BlockSpec((1, tq, A_Q), tile),
                  pl.BlockSpec((1, tq, d), tile),
                  pl.BlockSpec((1, tq, B_W), tile),
                  pl.BlockSpec((MIX, d), whole),
                  pl.BlockSpec((1, d), whole)],
        out_specs=pl.BlockSpec((1, tq, d), tile),
        out_shape=jax.ShapeDtypeStruct((b, s, d), F32),
        scratch_shapes=[pltpu.VMEM((tq + WINDOW, A_KV), BF16),
                        pltpu.VMEM((vr, tq + WINDOW), BF16),
                        pltpu.VMEM((tq, A_Q), BF16)],
        compiler_params=pltpu.CompilerParams(
            dimension_semantics=("arbitrary", "arbitrary"), vmem_limit_bytes=VMEM_LIMIT),
        name="swa_out",
    )(sinks, aqt, ak, ak, avt, avt, sg, x, mixed_b, wout, fg)


def _diff_kernel(slopes_ref, lq1_ref, lk1_ref, lq2_ref, lk2_ref, subg_ref,
                 qt_ref, k_ref, vt_ref, sg_ref, o_ref,
                 qa_ref, s_ref, m_ref, acc_ref, *, lambda_init):
    i = pl.program_id(2)

    @pl.when(i == 0)
    def _():
        _diff_tile(slopes_ref, lq1_ref, lk1_ref, lq2_ref, lk2_ref, subg_ref,
                   qt_ref, k_ref, vt_ref, sg_ref, o_ref, qa_ref, s_ref, m_ref, acc_ref,
                   lambda_init=lambda_init, first_tile=True)

    @pl.when(i > 0)
    def _():
        _diff_tile(slopes_ref, lq1_ref, lk1_ref, lq2_ref, lk2_ref, subg_ref,
                   qt_ref, k_ref, vt_ref, sg_ref, o_ref, qa_ref, s_ref, m_ref, acc_ref,
                   lambda_init=lambda_init, first_tile=False)


def _diff_tile(slopes_ref, lq1_ref, lk1_ref, lq2_ref, lk2_ref, subg_ref,
               qt_ref, k_ref, vt_ref, sg_ref, o_ref,
               qa_ref, s_ref, m_ref, acc_ref, *, lambda_init, first_tile):
    tg, tk = TG_DIFF, TK_DIFF
    nchain = 2 * NG_DIFF
    hd = pl.program_id(1)
    i = pl.program_id(2)
    slope = slopes_ref[hd] * LOG2E

    def key_consts(n):
        kk = lax.broadcasted_iota(jnp.int32, (n, LANES), 0).astype(F32)
        lane = lax.broadcasted_iota(jnp.int32, (n, LANES), 1)
        ab = slope * kk
        ab_hi = ab.astype(BF16).astype(F32)
        ab_lo = ab - ab_hi
        zk = jnp.zeros((n, LANES), F32)
        bias1 = jnp.where(lane == HEAD_DIM, ab_hi,
                          jnp.where(lane == HEAD_DIM + 1, ab_lo, zk)).astype(BF16)
        bias2 = jnp.where(lane == 0, ab_hi, jnp.where(lane == 1, ab_lo, zk)).astype(BF16)
        return lane < HEAD_DIM, bias1, bias2

    kconst = {n: key_consts(n) for n in ((tg,) if first_tile else (tg, tk))}

    rowq = lax.broadcasted_iota(jnp.int32, (LANES, tg), 0)
    one = jnp.ones((LANES, tg), F32)
    zq = jnp.zeros((LANES, tg), F32)
    for g in range(NG_DIFF):
        qt = qt_ref[0, :, g * tg:(g + 1) * tg].astype(F32)
        qa_ref[2 * g] = jnp.where(rowq < HEAD_DIM, qt,
                                  jnp.where(rowq < HEAD_DIM + 2, one, zq)).astype(BF16)
        qa_ref[2 * g + 1] = jnp.where(rowq >= HEAD_DIM, qt,
                                      jnp.where(rowq < 2, one, zq)).astype(BF16)
    m_ref[...] = jnp.full(m_ref.shape, NEG_BIG, F32)
    acc_ref[...] = jnp.zeros(acc_ref.shape, F32)

    krow = lax.broadcasted_iota(jnp.int32, (tg, tg), 0)
    qcol = lax.broadcasted_iota(jnp.int32, (tg, tg), 1)
    causal = krow <= qcol

    def scores(buf, c, ka, n):
        s_ref[buf, c, 0:n, :] = jnp.dot(ka, qa_ref[c], preferred_element_type=F32)

    def consume(buf, c, vt, cj, n, masked):
        s = s_ref[buf, c, 0:n, :]
        if masked:
            s = jnp.where(causal, s, NEG_BIG)
        m_old = m_ref[c]
        m_new = jnp.maximum(m_old, jnp.max(s, axis=0, keepdims=True) + cj)
        alpha = jnp.exp2(m_old - m_new)
        p = jnp.exp2(s - (m_new - cj)).astype(BF16)
        acc_ref[c] = alpha * acc_ref[c] + jnp.dot(vt, p, preferred_element_type=F32)
        m_ref[c] = m_new

    def keys(start, n):
        k = k_ref[0, pl.ds(pl.multiple_of(start, n), n), :]
        lane_lo, bias1, bias2 = kconst[n]
        return jnp.where(lane_lo, k, bias1), jnp.where(lane_lo, bias2, k)

    def block_bias(start):
        return slope * start.astype(F32)

    lam = (jnp.exp(jnp.sum(lq1_ref[...] * lk1_ref[...], axis=-1, keepdims=True))
           - jnp.exp(jnp.sum(lq2_ref[...] * lk2_ref[...], axis=-1, keepdims=True))
           + lambda_init)

    def finish(g):
        a1 = acc_ref[2 * g]
        a2 = acc_ref[2 * g + 1]
        ot = (a1[0:LANES] / a1[LANES:LANES + 1]
              - lam * (a2[0:LANES] / a2[LANES:LANES + 1]))
        ot = ot * lax.rsqrt(jnp.mean(ot * ot, axis=0, keepdims=True) + SUBLN_EPS) * subg_ref[...]
        ot = ot * (1.0 - lambda_init)
        rows = slice(g * tg, (g + 1) * tg)
        o_ref[0, rows, :] = (ot.T * sg_ref[0, rows, :].astype(F32)).astype(BF16)

    base = i * (NG_DIFF * tg)
    ka = keys(base, tg)
    for c in range(nchain):
        scores(0, c, ka[c % 2], tg)
    ka_first = None if first_tile else keys(jnp.int32(0), tk)
    for jj in range(NG_DIFF):
        cur, nxt = jj % 2, 1 - jj % 2
        start = base + jj * tg
        vt = vt_ref[0, i * (NG_DIFF * tg // tk) + (jj * tg) // tk, :,
                    (jj * tg) % tk:(jj * tg) % tk + tg]
        cj = block_bias(start)
        ka = keys(start + tg, tg) if jj + 1 < NG_DIFF else None
        for g in range(jj, NG_DIFF):
            for c in (2 * g, 2 * g + 1):
                if g > jj:
                    scores(nxt, c, ka[c % 2], tg)
                consume(cur, c, vt, cj, tg, g == jj)
                if g == jj and not first_tile:
                    scores(0, c, ka_first[c % 2], tk)
            if g == jj and first_tile:
                finish(g)
    if first_tile:
        return

    nblk = i * (NG_DIFF * tg // tk)

    def step(j, cur, last):
        vt = vt_ref[0, j]
        cj = block_bias(j * tk)
        kan = None if last else keys((j + 1) * tk, tk)
        for g in range(NG_DIFF):
            for c in (2 * g, 2 * g + 1):
                if not last:
                    scores(1 - cur, c, kan[c % 2], tk)
                consume(cur, c, vt, cj, tk, False)
            if last:
                finish(g)

    def body(t, carry):
        step(2 * t, 0, False)
        step(2 * t + 1, 1, False)
        return carry

    lax.fori_loop(0, nblk // 2 - 1, body, 0)
    step(nblk - 2, 0, False)
    step(nblk - 1, 1, True)


def _diff(slopes, lq1, lk1, lq2, lk2, subg, bqt, bk, bvt, sg, lambda_init):
    b, s, _ = bk.shape
    tg, tk = TG_DIFF, TK_DIFF
    tq = NG_DIFF * tg
    nchain = 2 * NG_DIFF
    smem = pl.BlockSpec(memory_space=pltpu.SMEM)
    small = lambda shape: pl.BlockSpec(shape, lambda bi, h, i: (0, 0))
    return pl.pallas_call(
        functools.partial(_diff_kernel, lambda_init=lambda_init),
        grid=(b, DIFF_HEADS, s // tq),
        in_specs=[smem,
                  small((1, HEAD_DIM)), small((1, HEAD_DIM)),
                  small((1, HEAD_DIM)), small((1, HEAD_DIM)),
                  small((LANES, 1)),
                  pl.BlockSpec((1, LANES, tq), lambda bi, h, i: (bi, h, i)),
                  pl.BlockSpec((1, s, LANES), lambda bi, h, i: (bi, 0, h)),
                  pl.BlockSpec((1, s // tk, VROWS, tk), lambda bi, h, i: (bi, 0, h, 0)),
                  pl.BlockSpec((1, tq, LANES), lambda bi, h, i: (bi, i, A_Q // LANES + h))],
        out_specs=pl.BlockSpec((1, tq, LANES), lambda bi, h, i: (bi, i, h)),
        out_shape=jax.ShapeDtypeStruct((b, s, B_W), BF16),
        scratch_shapes=[pltpu.VMEM((nchain, LANES, tg), BF16),
                        pltpu.VMEM((2, nchain, tk, tg), F32),
                        pltpu.VMEM((nchain, 1, tg), F32),
                        pltpu.VMEM((nchain, VROWS, tg), F32)],
        compiler_params=pltpu.CompilerParams(
            dimension_semantics=("arbitrary", "arbitrary", "arbitrary"),
            vmem_limit_bytes=VMEM_LIMIT),
        name="diffattn",
    )(slopes, lq1, lk1, lq2, lk2, subg, bqt, bk, bvt, sg)


def kernel(x, norm_g, w_in, sinks, lambda_q1, lambda_k1, lambda_q2, lambda_k2,
           subln_g, w_out, final_g):
    b, s, d = x.shape
    depth = norm_g.shape[0]
    diff_slopes = jnp.asarray(
        [2.0 ** (-8.0 * (h + 1) / DIFF_HEADS) for h in range(DIFF_HEADS)], F32)
    h3 = x
    for layer in range(depth):
        aqt, ak, avt, bqt, bk, bvt, sg = _inproj(
            h3, norm_g[layer].reshape(1, d), w_in[layer].astype(BF16))
        mixed_b = _diff(diff_slopes,
                        lambda_q1[layer].reshape(1, HEAD_DIM), lambda_k1[layer].reshape(1, HEAD_DIM),
                        lambda_q2[layer].reshape(1, HEAD_DIM), lambda_k2[layer].reshape(1, HEAD_DIM),
                        subln_g[layer].reshape(LANES, 1),
                        bqt, bk, bvt, sg, _lambda_init(layer))
        h3 = _swa_out(sinks[layer], aqt, ak, avt, sg, h3, mixed_b,
                      w_out[layer].astype(BF16), final_g.reshape(1, d),
                      final_norm=(layer == depth - 1))
    return h3
```

```python
import functools
import math

import jax
import jax.numpy as jnp
import numpy as np
from jax import lax
from jax.experimental import pallas as pl
from jax.experimental.pallas import tpu as pltpu

F32 = jnp.float32
BF16 = jnp.bfloat16

HEAD_DIM = 64
LANES = 128
SWA_Q_HEADS = 8
SWA_KV_HEADS = 2
WINDOW = 128
DIFF_HEADS = 4
RMS_EPS = 1e-6
SUBLN_EPS = 1e-5
NEG_BIG = -1e30
LOG2E = math.log2(math.e)

A_Q = SWA_Q_HEADS * HEAD_DIM
A_KV = SWA_KV_HEADS * HEAD_DIM
B_W = DIFF_HEADS * 2 * HEAD_DIM
MIX = A_Q + B_W
OFF_AK = A_Q
OFF_AV = OFF_AK + A_KV
OFF_BQ = OFF_AV + A_KV
OFF_BK = OFF_BQ + B_W
OFF_BV = OFF_BK + B_W
OFF_GATE = OFF_BV + B_W
IN_COLS = OFF_GATE + MIX

TM_PROJ = 1024
ROWS_OUT = 256
COLS_OUT = 256
TQ_SWA = 1024
TG_DIFF = 256
NG_DIFF = 4
TK_DIFF = 512
VROWS = LANES + 16
VROWS_A = HEAD_DIM + 16
VMEM_LIMIT = 56 * 1024 * 1024


def _lambda_init(layer_idx):
    return 0.8 - 0.6 * math.exp(-0.3 * layer_idx)


def _nt_dot(a, b):
    return lax.dot_general(a, b, (((1,), (1,)), ((), ())), preferred_element_type=F32)


def _inproj_kernel(x_ref, g_ref, w_ref,
                   aqt_ref, ak_ref, avt_ref, bqt_ref, bk_ref, bvt_ref, sg_ref):
    x = x_ref[0]
    ms = jnp.mean(x * x, axis=-1, keepdims=True)
    h = (x * lax.rsqrt(ms + RMS_EPS) * g_ref[...]).astype(BF16)
    qscale = HEAD_DIM ** -0.5 * LOG2E

    def tok(lo, hi):
        return jnp.dot(h, w_ref[:, lo:hi], preferred_element_type=F32)

    def feat(lo, hi):
        return lax.dot_general(w_ref[:, lo:hi], h, (((0,), (1,)), ((), ())),
                               preferred_element_type=F32)

    ak_ref[0] = tok(OFF_AK, OFF_AV).astype(BF16)
    bk_ref[0] = tok(OFF_BK, OFF_BV).astype(BF16)
    gate = tok(OFF_GATE, IN_COLS)
    sg_ref[0] = (gate * (1.0 / (1.0 + jnp.exp(-gate)))).astype(BF16)

    aqt_ref[0] = (feat(0, OFF_AK) * qscale).astype(BF16)
    avt = feat(OFF_AV, OFF_BQ).astype(BF16)
    for g in range(SWA_KV_HEADS):
        avt_ref[0, g * VROWS_A:g * VROWS_A + HEAD_DIM, :] = avt[g * HEAD_DIM:(g + 1) * HEAD_DIM]
        avt_ref[0, g * VROWS_A + HEAD_DIM:(g + 1) * VROWS_A, :] = jnp.ones(
            (VROWS_A - HEAD_DIM, TM_PROJ), BF16)
    bqt_ref[0] = (feat(OFF_BQ, OFF_BK) * qscale).astype(BF16)
    bvt = feat(OFF_BV, OFF_GATE).astype(BF16)
    ones = jnp.ones((VROWS - LANES, TK_DIFF), BF16)
    for c in range(TM_PROJ // TK_DIFF):
        for hd in range(DIFF_HEADS):
            bvt_ref[0, c, hd * VROWS:hd * VROWS + LANES, :] = (
                bvt[hd * LANES:(hd + 1) * LANES, c * TK_DIFF:(c + 1) * TK_DIFF])
            bvt_ref[0, c, hd * VROWS + LANES:(hd + 1) * VROWS, :] = ones


def _inproj(x, g, w):
    b, s, d = x.shape
    tm = TM_PROJ
    row = lambda bi, i: (bi, i, 0)
    col = lambda bi, i: (bi, 0, i)
    whole = lambda bi, i: (0, 0)
    bsd = lambda wd: jax.ShapeDtypeStruct((b, s, wd), BF16)
    bds = lambda wd: jax.ShapeDtypeStruct((b, wd, s), BF16)
    return pl.pallas_call(
        _inproj_kernel,
        grid=(b, s // tm),
        in_specs=[pl.BlockSpec((1, tm, d), row),
                  pl.BlockSpec((1, d), whole),
                  pl.BlockSpec(w.shape, whole)],
        out_specs=[pl.BlockSpec((1, A_Q, tm), col),
                   pl.BlockSpec((1, tm, A_KV), row),
                   pl.BlockSpec((1, SWA_KV_HEADS * VROWS_A, tm), col),
                   pl.BlockSpec((1, B_W, tm), col),
                   pl.BlockSpec((1, tm, B_W), row),
                   pl.BlockSpec((1, tm // TK_DIFF, DIFF_HEADS * VROWS, TK_DIFF), lambda bi, i: (bi, i, 0, 0)),
                   pl.BlockSpec((1, tm, MIX), row)],
        out_shape=[bds(A_Q), bsd(A_KV), bds(SWA_KV_HEADS * VROWS_A), bds(B_W), bsd(B_W),
                   jax.ShapeDtypeStruct((b, s // TK_DIFF, DIFF_HEADS * VROWS, TK_DIFF), BF16),
                   bsd(MIX)],
        compiler_params=pltpu.CompilerParams(
            dimension_semantics=("arbitrary", "arbitrary"), vmem_limit_bytes=VMEM_LIMIT),
        name="inproj",
    )(x, g, w)


def _swa_out_kernel(sinks_ref, qt_ref, kp_ref, kc_ref, vtp_ref, vtc_ref, sg_ref,
                    x_ref, mb_ref, wout_ref, fg_ref, o_ref, kcat, vtcat, ma_scr, *, final_norm):
    w_ = WINDOW
    i = pl.program_id(1)
    group = SWA_Q_HEADS // SWA_KV_HEADS
    nq = group * w_
    kcat[0:w_, :] = kp_ref[0]
    kcat[w_:, :] = kc_ref[0]
    vtcat[:, 0:w_] = vtp_ref[0]
    vtcat[:, w_:] = vtc_ref[0]

    kidx = lax.broadcasted_iota(jnp.int32, (2 * w_, nq), 0)
    qidx = lax.broadcasted_iota(jnp.int32, (2 * w_, nq), 1) & (w_ - 1)
    in_cur = kidx >= w_
    band = (in_cur & (kidx - w_ <= qidx)) | ((kidx < w_) & (kidx > qidx))
    first_band = band & ((i > 0) | in_cur)

    lane = lax.broadcasted_iota(jnp.int32, (2 * w_, LANES), 1)
    kpos = lax.broadcasted_iota(jnp.int32, (2 * w_, LANES), 0).astype(F32)
    zk = jnp.zeros((2 * w_, LANES), F32)
    kmask = (lane < HEAD_DIM, lane >= HEAD_DIM)
    kpos_lanes = (jnp.where((lane == HEAD_DIM) | (lane == HEAD_DIM + 1), kpos, zk).astype(BF16),
                  jnp.where(lane < 2, kpos, zk).astype(BF16))

    hrow = lax.broadcasted_iota(jnp.int32, (HEAD_DIM, nq), 0)
    hcol = lax.broadcasted_iota(jnp.int32, (HEAD_DIM, nq), 1)

    def slope_rows(g):
        out = jnp.zeros((HEAD_DIM, nq), F32)
        for u in range(group):
            sl = 2.0 ** (-8.0 * (g * group + u + 1) / SWA_Q_HEADS) * LOG2E
            hi = float(np.asarray(sl, dtype=BF16).astype(np.float32))
            lo = sl - hi
            in_head = (hcol >= u * w_) & (hcol < (u + 1) * w_)
            out = jnp.where(in_head & (hrow == 0), hi, jnp.where(in_head & (hrow == 1), lo, out))
        return out.astype(BF16)

    srows = [slope_rows(g) for g in range(SWA_KV_HEADS)]

    qpos = (lax.broadcasted_iota(jnp.int32, (1, w_), 1) + w_).astype(F32)

    def sink_row(g):
        return jnp.concatenate(
            [(sinks_ref[g * group + u] + 2.0 ** (-8.0 * (g * group + u + 1) / SWA_Q_HEADS) * qpos) * LOG2E
             for u in range(group)], axis=1)

    sinkv = [sink_row(g) for g in range(SWA_KV_HEADS)]

    def scores(w, g):
        keys = kcat[w * w_:(w + 2) * w_, :]
        ka = jnp.where(kmask[g], keys, kpos_lanes[g])
        qh = jnp.concatenate(
            [qt_ref[0, (g * group + u) * HEAD_DIM:(g * group + u + 1) * HEAD_DIM, w * w_:(w + 1) * w_]
             for u in range(group)], axis=1)
        wq = jnp.concatenate([qh, srows[g]] if g == 0 else [srows[g], qh], axis=0)
        return jnp.dot(ka, wq, preferred_element_type=F32)

    def consume(w, g, s):
        s = jnp.where(first_band if w == 0 else band, s, NEG_BIG)
        m = jnp.maximum(jnp.max(s, axis=0, keepdims=True), sinkv[g])
        p = jnp.exp2(s - m).astype(BF16)
        vt = vtcat[g * VROWS_A:(g + 1) * VROWS_A, w * w_:(w + 2) * w_]
        o = jnp.dot(vt, p, preferred_element_type=F32)
        denom = o[HEAD_DIM:HEAD_DIM + 1] + jnp.exp2(sinkv[g] - m)
        ot = o[0:HEAD_DIM] / denom
        rows = slice(w * w_, (w + 1) * w_)
        for pair in range(group // 2):
            cols = slice((g * (group // 2) + pair) * LANES, (g * (group // 2) + pair + 1) * LANES)
            two = jnp.concatenate([ot[:, (2 * pair) * w_:(2 * pair + 1) * w_],
                                   ot[:, (2 * pair + 1) * w_:(2 * pair + 2) * w_]], axis=0)
            ma_scr[rows, cols] = (two.T * sg_ref[0, rows, cols].astype(F32)).astype(BF16)

    npiece = MIX // COLS_OUT

    def project(k, j):
        rows = slice(k * ROWS_OUT, (k + 1) * ROWS_OUT)
        cols = slice(j * COLS_OUT, (j + 1) * COLS_OUT)
        o_ref[0, rows, cols] = (
            x_ref[0, rows, cols]
            + jnp.dot(ma_scr[rows, :], wout_ref[0:A_Q, cols], preferred_element_type=F32)
            + jnp.dot(mb_ref[0, rows, :], wout_ref[A_Q:MIX, cols], preferred_element_type=F32))
        if final_norm and j == npiece - 1:
            y = o_ref[0, rows, :]
            ms = jnp.mean(y * y, axis=-1, keepdims=True)
            o_ref[0, rows, :] = y * lax.rsqrt(ms + RMS_EPS) * fg_ref[...]

    chains = [(w, g) for w in range(TQ_SWA // w_) for g in range(SWA_KV_HEADS)]
    chains_per_chunk = (ROWS_OUT // w_) * SWA_KV_HEADS
    ahead = 2
    pending = [scores(*ch) for ch in chains[:ahead]]
    ready = []
    for n, (w, g) in enumerate(chains):
        if n + ahead < len(chains):
            pending.append(scores(*chains[n + ahead]))
        consume(w, g, pending.pop(0))
        if (n + 1) % chains_per_chunk == 0:
            ready += [(n // chains_per_chunk, j) for j in range(npiece)]
        if ready:
            project(*ready.pop(0))
    for unit in ready:
        project(*unit)


def _swa_out(sinks, aqt, ak, avt, sg, x, mixed_b, wout, fg, final_norm):
    b, s, d = x.shape
    tq = TQ_SWA
    wpt = tq // WINDOW
    vr = SWA_KV_HEADS * VROWS_A
    prev_w = lambda i: jnp.maximum(i * wpt - 1, 0)
    tile = lambda bi, i: (bi, i, 0)
    whole = lambda bi, i: (0, 0)
    return pl.pallas_call(
        functools.partial(_swa_out_kernel, final_norm=final_norm),
        grid=(b, s // tq),
        in_specs=[pl.BlockSpec(memory_space=pltpu.SMEM),
                  pl.BlockSpec((1, A_Q, tq), lambda bi, i: (bi, 0, i)),
                  pl.BlockSpec((1, WINDOW, A_KV), lambda bi, i: (bi, prev_w(i), 0)),
                  pl.BlockSpec((1, tq, A_KV), tile),
                  pl.BlockSpec((1, vr, WINDOW), lambda bi, i: (bi, 0, prev_w(i))),
                  pl.BlockSpec((1, vr, tq), lambda bi, i: (bi, 0, i)),
                  pl.BlockSpec((1, tq, A_Q), tile),
                  pl.BlockSpec((1, tq, d), tile),
                  pl.BlockSpec((1, tq, B_W), tile),
                  pl.BlockSpec((MIX, d), whole),
                  pl.BlockSpec((1, d), whole)],
        out_specs=pl.BlockSpec((1, tq, d), tile),
        out_shape=jax.ShapeDtypeStruct((b, s, d), F32),
        scratch_shapes=[pltpu.VMEM((tq + WINDOW, A_KV), BF16),
                        pltpu.VMEM((vr, tq + WINDOW), BF16),
                        pltpu.VMEM((tq, A_Q), BF16)],
        compiler_params=pltpu.CompilerParams(
            dimension_semantics=("arbitrary", "arbitrary"), vmem_limit_bytes=VMEM_LIMIT),
        name="swa_out",
    )(sinks, aqt, ak, ak, avt, avt, sg, x, mixed_b, wout, fg)


def _diff_kernel(slopes_ref, lq1_ref, lk1_ref, lq2_ref, lk2_ref, subg_ref,
                 qt_ref, k_ref, vt_ref, sg_ref, o_ref,
                 qa_ref, s_ref, m_ref, acc_ref, *, lambda_init):
    i = pl.program_id(2)

    @pl.when(i == 0)
    def _():
        _diff_tile(slopes_ref, lq1_ref, lk1_ref, lq2_ref, lk2_ref, subg_ref,
                   qt_ref, k_ref, vt_ref, sg_ref, o_ref, qa_ref, s_ref, m_ref, acc_ref,
                   lambda_init=lambda_init, first_tile=True)

    @pl.when(i > 0)
    def _():
        _diff_tile(slopes_ref, lq1_ref, lk1_ref, lq2_ref, lk2_ref, subg_ref,
                   qt_ref, k_ref, vt_ref, sg_ref, o_ref, qa_ref, s_ref, m_ref, acc_ref,
                   lambda_init=lambda_init, first_tile=False)


def _diff_tile(slopes_ref, lq1_ref, lk1_ref, lq2_ref, lk2_ref, subg_ref,
               qt_ref, k_ref, vt_ref, sg_ref, o_ref,
               qa_ref, s_ref, m_ref, acc_ref, *, lambda_init, first_tile):
    tg, tk = TG_DIFF, TK_DIFF
    nchain = 2 * NG_DIFF
    hd = pl.program_id(1)
    i = pl.program_id(2)
    slope = slopes_ref[hd] * LOG2E

    def key_consts(n):
        kk = lax.broadcasted_iota(jnp.int32, (n, LANES), 0).astype(F32)
        lane = lax.broadcasted_iota(jnp.int32, (n, LANES), 1)
        ab = slope * kk
        ab_hi = ab.astype(BF16).astype(F32)
        ab_lo = ab - ab_hi
        zk = jnp.zeros((n, LANES), F32)
        bias1 = jnp.where(lane == HEAD_DIM, ab_hi,
                          jnp.where(lane == HEAD_DIM + 1, ab_lo, zk)).astype(BF16)
        bias2 = jnp.where(lane == 0, ab_hi, jnp.where(lane == 1, ab_lo, zk)).astype(BF16)
        return lane < HEAD_DIM, bias1, bias2

    kconst = {n: key_consts(n) for n in ((tg,) if first_tile else (tg, tk))}

    rowq = lax.broadcasted_iota(jnp.int32, (LANES, tg), 0)
    one = jnp.ones((LANES, tg), F32)
    zq = jnp.zeros((LANES, tg), F32)
    for g in range(NG_DIFF):
        qt = qt_ref[0, :, g * tg:(g + 1) * tg].astype(F32)
        qa_ref[2 * g] = jnp.where(rowq < HEAD_DIM, qt,
                                  jnp.where(rowq < HEAD_DIM + 2, one, zq)).astype(BF16)
        qa_ref[2 * g + 1] = jnp.where(rowq >= HEAD_DIM, qt,
                                      jnp.where(rowq < 2, one, zq)).astype(BF16)
    m_ref[...] = jnp.full(m_ref.shape, NEG_BIG, F32)
    acc_ref[...] = jnp.zeros(acc_ref.shape, F32)

    krow = lax.broadcasted_iota(jnp.int32, (tg, tg), 0)
    qcol = lax.broadcasted_iota(jnp.int32, (tg, tg), 1)
    causal = krow <= qcol

    def scores(buf, c, ka, n):
        s_ref[buf, c, 0:n, :] = jnp.dot(ka, qa_ref[c], preferred_element_type=F32)

    def consume(buf, c, vt, cj, n, masked):
        s = s_ref[buf, c, 0:n, :]
        if masked:
            s = jnp.where(causal, s, NEG_BIG)
        m_old = m_ref[c]
        m_new = jnp.maximum(m_old, jnp.max(s, axis=0, keepdims=True) + cj)
        alpha = jnp.exp2(m_old - m_new)
        p = jnp.exp2(s - (m_new - cj)).astype(BF16)
        acc_ref[c] = alpha * acc_ref[c] + jnp.dot(vt, p, preferred_element_type=F32)
        m_ref[c] = m_new

    def keys(start, n):
        k = k_ref[0, pl.ds(pl.multiple_of(start, n), n), :]
        lane_lo, bias1, bias2 = kconst[n]
        return jnp.where(lane_lo, k, bias1), jnp.where(lane_lo, bias2, k)

    def block_bias(start):
        return slope * start.astype(F32)

    lam = (jnp.exp(jnp.sum(lq1_ref[...] * lk1_ref[...], axis=-1, keepdims=True))
           - jnp.exp(jnp.sum(lq2_ref[...] * lk2_ref[...], axis=-1, keepdims=True))
           + lambda_init)

    def finish(g):
        a1 = acc_ref[2 * g]
        a2 = acc_ref[2 * g + 1]
        ot = (a1[0:LANES] / a1[LANES:LANES + 1]
              - lam * (a2[0:LANES] / a2[LANES:LANES + 1]))
        ot = ot * lax.rsqrt(jnp.mean(ot * ot, axis=0, keepdims=True) + SUBLN_EPS) * subg_ref[...]
        ot = ot * (1.0 - lambda_init)
        rows = slice(g * tg, (g + 1) * tg)
        o_ref[0, rows, :] = (ot.T * sg_ref[0, rows, :].astype(F32)).astype(BF16)

    base = i * (NG_DIFF * tg)
    ka = keys(base, tg)
    for c in range(nchain):
        scores(0, c, ka[c % 2], tg)
    ka_first = None if first_tile else keys(jnp.int32(0), tk)
    for jj in range(NG_DIFF):
        cur, nxt = jj % 2, 1 - jj % 2
        start = base + jj * tg
        vt = vt_ref[0, i * (NG_DIFF * tg // tk) + (jj * tg) // tk, :,
                    (jj * tg) % tk:(jj * tg) % tk + tg]
        cj = block_bias(start)
        ka = keys(start + tg, tg) if jj + 1 < NG_DIFF else None
        for g in range(jj, NG_DIFF):
            for c in (2 * g, 2 * g + 1):
                if g > jj:
                    scores(nxt, c, ka[c % 2], tg)
                consume(cur, c, vt, cj, tg, g == jj)
                if g == jj and not first_tile:
                    scores(0, c, ka_first[c % 2], tk)
            if g == jj and first_tile:
                finish(g)
    if first_tile:
        return

    nblk = i * (NG_DIFF * tg // tk)

    def step(j, cur, last):
        vt = vt_ref[0, j]
        cj = block_bias(j * tk)
        kan = None if last else keys((j + 1) * tk, tk)
        for g in range(NG_DIFF):
            for c in (2 * g, 2 * g + 1):
                if not last:
                    scores(1 - cur, c, kan[c % 2], tk)
                consume(cur, c, vt, cj, tk, False)
            if last:
                finish(g)

    def body(t, carry):
        step(2 * t, 0, False)
        step(2 * t + 1, 1, False)
        return carry

    lax.fori_loop(0, nblk // 2 - 1, body, 0)
    step(nblk - 2, 0, False)
    step(nblk - 1, 1, True)


def _diff(slopes, lq1, lk1, lq2, lk2, subg, bqt, bk, bvt, sg, lambda_init):
    b, s, _ = bk.shape
    tg, tk = TG_DIFF, TK_DIFF
    tq = NG_DIFF * tg
    nchain = 2 * NG_DIFF
    smem = pl.BlockSpec(memory_space=pltpu.SMEM)
    small = lambda shape: pl.BlockSpec(shape, lambda bi, h, i: (0, 0))
    return pl.pallas_call(
        functools.partial(_diff_kernel, lambda_init=lambda_init),
        grid=(b, DIFF_HEADS, s // tq),
        in_specs=[smem,
                  small((1, HEAD_DIM)), small((1, HEAD_DIM)),
                  small((1, HEAD_DIM)), small((1, HEAD_DIM)),
                  small((LANES, 1)),
                  pl.BlockSpec((1, LANES, tq), lambda bi, h, i: (bi, h, i)),
                  pl.BlockSpec((1, s, LANES), lambda bi, h, i: (bi, 0, h)),
                  pl.BlockSpec((1, s // tk, VROWS, tk), lambda bi, h, i: (bi, 0, h, 0)),
                  pl.BlockSpec((1, tq, LANES), lambda bi, h, i: (bi, i, A_Q // LANES + h))],
        out_specs=pl.BlockSpec((1, tq, LANES), lambda bi, h, i: (bi, i, h)),
        out_shape=jax.ShapeDtypeStruct((b, s, B_W), BF16),
        scratch_shapes=[pltpu.VMEM((nchain, LANES, tg), BF16),
                        pltpu.VMEM((2, nchain, tk, tg), F32),
                        pltpu.VMEM((nchain, 1, tg), F32),
                        pltpu.VMEM((nchain, VROWS, tg), F32)],
        compiler_params=pltpu.CompilerParams(
            dimension_semantics=("arbitrary", "arbitrary", "arbitrary"),
            vmem_limit_bytes=VMEM_LIMIT),
        name="diffattn",
    )(slopes, lq1, lk1, lq2, lk2, subg, bqt, bk, bvt, sg)


def kernel(x, norm_g, w_in, sinks, lambda_q1, lambda_k1, lambda_q2, lambda_k2,
           subln_g, w_out, final_g):
    b, s, d = x.shape
    depth = norm_g.shape[0]
    diff_slopes = jnp.asarray(
        [2.0 ** (-8.0 * (h + 1) / DIFF_HEADS) for h in range(DIFF_HEADS)], F32)
    h3 = x
    for layer in range(depth):
        aqt, ak, avt, bqt, bk, bvt, sg = _inproj(
            h3, norm_g[layer].reshape(1, d), w_in[layer].astype(BF16))
        mixed_b = _diff(diff_slopes,
                        lambda_q1[layer].reshape(1, HEAD_DIM), lambda_k1[layer].reshape(1, HEAD_DIM),
                        lambda_q2[layer].reshape(1, HEAD_DIM), lambda_k2[layer].reshape(1, HEAD_DIM),
                        subln_g[layer].reshape(LANES, 1),
                        bqt, bk, bvt, sg, _lambda_init(layer))
        h3 = _swa_out(sinks[layer], aqt, ak, avt, sg, h3, mixed_b,
                      w_out[layer].astype(BF16), final_g.reshape(1, d),
                      final_norm=(layer == depth - 1))
    return h3
```

```python
import functools
import math

import jax
import jax.numpy as jnp
import numpy as np
from jax import lax
from jax.experimental import pallas as pl
from jax.experimental.pallas import tpu as pltpu

F32 = jnp.float32
BF16 = jnp.bfloat16

HEAD_DIM = 64
LANES = 128
SWA_Q_HEADS = 8
SWA_KV_HEADS = 2
WINDOW = 128
DIFF_HEADS = 4
RMS_EPS = 1e-6
SUBLN_EPS = 1e-5
NEG_BIG = -1e30
LOG2E = math.log2(math.e)

A_Q = SWA_Q_HEADS * HEAD_DIM
A_KV = SWA_KV_HEADS * HEAD_DIM
B_W = DIFF_HEADS * 2 * HEAD_DIM
MIX = A_Q + B_W
OFF_AK = A_Q
OFF_AV = OFF_AK + A_KV
OFF_BQ = OFF_AV + A_KV
OFF_BK = OFF_BQ + B_W
OFF_BV = OFF_BK + B_W
OFF_GATE = OFF_BV + B_W
IN_COLS = OFF_GATE + MIX

TM_PROJ = 1024
ROWS_OUT = 256
COLS_OUT = 256
TQ_SWA = 1024
TG_DIFF = 256
NG_DIFF = 4
TQ_DIFF = NG_DIFF * TG_DIFF
TK_DIFF = 512
VROWS = LANES + 16
VROWS_A = HEAD_DIM + 16
VMEM_LIMIT = 56 * 1024 * 1024


def _lambda_init(layer_idx):
    return 0.8 - 0.6 * math.exp(-0.3 * layer_idx)


def _nt_dot(a, b):
    return lax.dot_general(a, b, (((1,), (1,)), ((), ())), preferred_element_type=F32)


def _inproj_kernel(x_ref, g_ref, w_ref,
                   aqt_ref, ak_ref, avt_ref, bqt_ref, bk_ref, bvt_ref, sg_ref):
    x = x_ref[0]
    ms = jnp.mean(x * x, axis=-1, keepdims=True)
    h = (x * lax.rsqrt(ms + RMS_EPS) * g_ref[...]).astype(BF16)
    qscale = HEAD_DIM ** -0.5 * LOG2E

    def tok(lo, hi):
        return jnp.dot(h, w_ref[:, lo:hi], preferred_element_type=F32)

    def feat(lo, hi):
        return lax.dot_general(w_ref[:, lo:hi], h, (((0,), (1,)), ((), ())),
                               preferred_element_type=F32)

    ak_ref[0] = tok(OFF_AK, OFF_AV).astype(BF16)
    bk_ref[0] = tok(OFF_BK, OFF_BV).astype(BF16)
    gate = tok(OFF_GATE, IN_COLS)
    sg_ref[0] = (gate * (1.0 / (1.0 + jnp.exp(-gate)))).astype(BF16)

    aqt_ref[0] = (feat(0, OFF_AK) * qscale).astype(BF16)
    avt = feat(OFF_AV, OFF_BQ).astype(BF16)
    for g in range(SWA_KV_HEADS):
        avt_ref[0, g * VROWS_A:g * VROWS_A + HEAD_DIM, :] = avt[g * HEAD_DIM:(g + 1) * HEAD_DIM]
        avt_ref[0, g * VROWS_A + HEAD_DIM:(g + 1) * VROWS_A, :] = jnp.ones(
            (VROWS_A - HEAD_DIM, TM_PROJ), BF16)
    bqt = (feat(OFF_BQ, OFF_BK) * qscale).astype(BF16)
    for c in range(TM_PROJ // TQ_DIFF):
        bqt_ref[0, c] = bqt[:, c * TQ_DIFF:(c + 1) * TQ_DIFF]
    bvt = feat(OFF_BV, OFF_GATE).astype(BF16)
    ones = jnp.ones((VROWS - LANES, TK_DIFF), BF16)
    for c in range(TM_PROJ // TK_DIFF):
        for hd in range(DIFF_HEADS):
            bvt_ref[0, c, hd * VROWS:hd * VROWS + LANES, :] = (
                bvt[hd * LANES:(hd + 1) * LANES, c * TK_DIFF:(c + 1) * TK_DIFF])
            bvt_ref[0, c, hd * VROWS + LANES:(hd + 1) * VROWS, :] = ones


def _inproj(x, g, w):
    b, s, d = x.shape
    tm = TM_PROJ
    row = lambda bi, i: (bi, i, 0)
    col = lambda bi, i: (bi, 0, i)
    whole = lambda bi, i: (0, 0)
    bsd = lambda wd: jax.ShapeDtypeStruct((b, s, wd), BF16)
    bds = lambda wd: jax.ShapeDtypeStruct((b, wd, s), BF16)
    return pl.pallas_call(
        _inproj_kernel,
        grid=(b, s // tm),
        in_specs=[pl.BlockSpec((1, tm, d), row),
                  pl.BlockSpec((1, d), whole),
                  pl.BlockSpec(w.shape, whole)],
        out_specs=[pl.BlockSpec((1, A_Q, tm), col),
                   pl.BlockSpec((1, tm, A_KV), row),
                   pl.BlockSpec((1, SWA_KV_HEADS * VROWS_A, tm), col),
                   pl.BlockSpec((1, tm // TQ_DIFF, B_W, TQ_DIFF), lambda bi, i: (bi, i, 0, 0)),
                   pl.BlockSpec((1, tm, B_W), row),
                   pl.BlockSpec((1, tm // TK_DIFF, DIFF_HEADS * VROWS, TK_DIFF), lambda bi, i: (bi, i, 0, 0)),
                   pl.BlockSpec((1, tm, MIX), row)],
        out_shape=[bds(A_Q), bsd(A_KV), bds(SWA_KV_HEADS * VROWS_A),
                   jax.ShapeDtypeStruct((b, s // TQ_DIFF, B_W, TQ_DIFF), BF16), bsd(B_W),
                   jax.ShapeDtypeStruct((b, s // TK_DIFF, DIFF_HEADS * VROWS, TK_DIFF), BF16),
                   bsd(MIX)],
        compiler_params=pltpu.CompilerParams(
            dimension_semantics=("arbitrary", "arbitrary"), vmem_limit_bytes=VMEM_LIMIT),
        name="inproj",
    )(x, g, w)


def _swa_out_kernel(sinks_ref, qt_ref, kp_ref, kc_ref, vtp_ref, vtc_ref, sg_ref,
                    x_ref, mb_ref, wout_ref, fg_ref, o_ref, kcat, vtcat, ma_scr, *, final_norm):
    w_ = WINDOW
    i = pl.program_id(1)
    group = SWA_Q_HEADS // SWA_KV_HEADS
    nq = group * w_
    kcat[0:w_, :] = kp_ref[0]
    kcat[w_:, :] = kc_ref[0]
    vtcat[:, 0:w_] = vtp_ref[0]
    vtcat[:, w_:] = vtc_ref[0]

    kidx = lax.broadcasted_iota(jnp.int32, (2 * w_, nq), 0)
    qidx = lax.broadcasted_iota(jnp.int32, (2 * w_, nq), 1) & (w_ - 1)
    in_cur = kidx >= w_
    band = (in_cur & (kidx - w_ <= qidx)) | ((kidx < w_) & (kidx > qidx))
    first_band = band & ((i > 0) | in_cur)

    lane = lax.broadcasted_iota(jnp.int32, (2 * w_, LANES), 1)
    kpos = lax.broadcasted_iota(jnp.int32, (2 * w_, LANES), 0).astype(F32)
    zk = jnp.zeros((2 * w_, LANES), F32)
    kmask = (lane < HEAD_DIM, lane >= HEAD_DIM)
    kpos_lanes = (jnp.where((lane == HEAD_DIM) | (lane == HEAD_DIM + 1), kpos, zk).astype(BF16),
                  jnp.where(lane < 2, kpos, zk).astype(BF16))

    hrow = lax.broadcasted_iota(jnp.int32, (HEAD_DIM, nq), 0)
    hcol = lax.broadcasted_iota(jnp.int32, (HEAD_DIM, nq), 1)

    def slope_rows(g):
        out = jnp.zeros((HEAD_DIM, nq), F32)
        for u in range(group):
            sl = 2.0 ** (-8.0 * (g * group + u + 1) / SWA_Q_HEADS) * LOG2E
            hi = float(np.asarray(sl, dtype=BF16).astype(np.float32))
            lo = sl - hi
            in_head = (hcol >= u * w_) & (hcol < (u + 1) * w_)
            out = jnp.where(in_head & (hrow == 0), hi, jnp.where(in_head & (hrow == 1), lo, out))
        return out.astype(BF16)

    srows = [slope_rows(g) for g in range(SWA_KV_HEADS)]

    qpos = (lax.broadcasted_iota(jnp.int32, (1, w_), 1) + w_).astype(F32)

    def sink_row(g):
        return jnp.concatenate(
            [(sinks_ref[g * group + u] + 2.0 ** (-8.0 * (g * group + u + 1) / SWA_Q_HEADS) * qpos) * LOG2E
             for u in range(group)], axis=1)

    sinkv = [sink_row(g) for g in range(SWA_KV_HEADS)]

    def scores(w, g):
        keys = kcat[w * w_:(w + 2) * w_, :]
        ka = jnp.where(kmask[g], keys, kpos_lanes[g])
        qh = jnp.concatenate(
            [qt_ref[0, (g * group + u) * HEAD_DIM:(g * group + u + 1) * HEAD_DIM, w * w_:(w + 1) * w_]
             for u in range(group)], axis=1)
        wq = jnp.concatenate([qh, srows[g]] if g == 0 else [srows[g], qh], axis=0)
        return jnp.dot(ka, wq, preferred_element_type=F32)

    def consume(w, g, s):
        s = jnp.where(first_band if w == 0 else band, s, NEG_BIG)
        m = jnp.maximum(jnp.max(s, axis=0, keepdims=True), sinkv[g])
        p = jnp.exp2(s - m).astype(BF16)
        vt = vtcat[g * VROWS_A:(g + 1) * VROWS_A, w * w_:(w + 2) * w_]
        o = jnp.dot(vt, p, preferred_element_type=F32)
        denom = o[HEAD_DIM:HEAD_DIM + 1] + jnp.exp2(sinkv[g] - m)
        ot = o[0:HEAD_DIM] / denom
        rows = slice(w * w_, (w + 1) * w_)
        for pair in range(group // 2):
            cols = slice((g * (group // 2) + pair) * LANES, (g * (group // 2) + pair + 1) * LANES)
            two = jnp.concatenate([ot[:, (2 * pair) * w_:(2 * pair + 1) * w_],
                                   ot[:, (2 * pair + 1) * w_:(2 * pair + 2) * w_]], axis=0)
            ma_scr[rows, cols] = (two.T * sg_ref[0, rows, cols].astype(F32)).astype(BF16)

    npiece = MIX // COLS_OUT

    def project(k, j):
        rows = slice(k * ROWS_OUT, (k + 1) * ROWS_OUT)
        cols = slice(j * COLS_OUT, (j + 1) * COLS_OUT)
        o_ref[0, rows, cols] = (
            x_ref[0, rows, cols]
            + jnp.dot(ma_scr[rows, :], wout_ref[0:A_Q, cols], preferred_element_type=F32)
            + jnp.dot(mb_ref[0, rows, :], wout_ref[A_Q:MIX, cols], preferred_element_type=F32))
        if final_norm and j == npiece - 1:
            y = o_ref[0, rows, :]
            ms = jnp.mean(y * y, axis=-1, keepdims=True)
            o_ref[0, rows, :] = y * lax.rsqrt(ms + RMS_EPS) * fg_ref[...]

    chains = [(w, g) for w in range(TQ_SWA // w_) for g in range(SWA_KV_HEADS)]
    chains_per_chunk = (ROWS_OUT // w_) * SWA_KV_HEADS
    ahead = 2
    pending = [scores(*ch) for ch in chains[:ahead]]
    ready = []
    for n, (w, g) in enumerate(chains):
        if n + ahead < len(chains):
            pending.append(scores(*chains[n + ahead]))
        consume(w, g, pending.pop(0))
        if (n + 1) % chains_per_chunk == 0:
            ready += [(n // chains_per_chunk, j) for j in range(npiece)]
        if ready:
            project(*ready.pop(0))
    for unit in ready:
        project(*unit)


def _swa_out(sinks, aqt, ak, avt, sg, x, mixed_b, wout, fg, final_norm):
    b, s, d = x.shape
    tq = TQ_SWA
    wpt = tq // WINDOW
    vr = SWA_KV_HEADS * VROWS_A
    prev_w = lambda i: jnp.maximum(i * wpt - 1, 0)
    tile = lambda bi, i: (bi, i, 0)
    whole = lambda bi, i: (0, 0)
    return pl.pallas_call(
        functools.partial(_swa_out_kernel, final_norm=final_norm),
        grid=(b, s // tq),
        in_specs=[pl.BlockSpec(memory_space=pltpu.SMEM),
                  pl.BlockSpec((1, A_Q, tq), lambda bi, i: (bi, 0, i)),
                  pl.BlockSpec((1, WINDOW, A_KV), lambda bi, i: (bi, prev_w(i), 0)),
                  pl.BlockSpec((1, tq, A_KV), tile),
                  pl.BlockSpec((1, vr, WINDOW), lambda bi, i: (bi, 0, prev_w(i))),
                  pl.BlockSpec((1, vr, tq), lambda bi, i: (bi, 0, i)),
                  pl.BlockSpec((1, tq, A_Q), tile),
                  pl.BlockSpec((1, tq, d), tile),
                  pl.BlockSpec((1, tq, B_W), tile),
                  pl.BlockSpec((MIX, d), whole),
                  pl.BlockSpec((1, d), whole)],
        out_specs=pl.BlockSpec((1, tq, d), tile),
        out_shape=jax.ShapeDtypeStruct((b, s, d), F32),
        scratch_shapes=[pltpu.VMEM((tq + WINDOW, A_KV), BF16),
                        pltpu.VMEM((vr, tq + WINDOW), BF16),
                        pltpu.VMEM((tq, A_Q), BF16)],
        compiler_params=pltpu.CompilerParams(
            dimension_semantics=("arbitrary", "arbitrary"), vmem_limit_bytes=VMEM_LIMIT),
        name="swa_out",
    )(sinks, aqt, ak, ak, avt, avt, sg, x, mixed_b, wout, fg)


def _diff_kernel(slopes_ref, lq1_ref, lk1_ref, lq2_ref, lk2_ref, subg_ref,
                 qt_ref, k_ref, vt_ref, sg_ref, o_ref,
                 qa_ref, s_ref, m_ref, acc_ref, *, lambda_init):
    tg, tk, tq = TG_DIFF, TK_DIFF, TQ_DIFF
    nchain = 2 * NG_DIFF
    ntile = qt_ref.shape[1]
    hd = pl.program_id(1)
    slope = slopes_ref[hd] * LOG2E

    def key_consts(n):
        kk = lax.broadcasted_iota(jnp.int32, (n, LANES), 0).astype(F32)
        lane = lax.broadcasted_iota(jnp.int32, (n, LANES), 1)
        ab = slope * kk
        ab_hi = ab.astype(BF16).astype(F32)
        ab_lo = ab - ab_hi
        zk = jnp.zeros((n, LANES), F32)
        bias1 = jnp.where(lane == HEAD_DIM, ab_hi,
                          jnp.where(lane == HEAD_DIM + 1, ab_lo, zk)).astype(BF16)
        bias2 = jnp.where(lane == 0, ab_hi, jnp.where(lane == 1, ab_lo, zk)).astype(BF16)
        return lane < HEAD_DIM, bias1, bias2

    lam = (jnp.exp(jnp.sum(lq1_ref[...] * lk1_ref[...], axis=-1, keepdims=True))
           - jnp.exp(jnp.sum(lq2_ref[...] * lk2_ref[...], axis=-1, keepdims=True))
           + lambda_init)

    def build_queries(t):
        rowq = lax.broadcasted_iota(jnp.int32, (LANES, tg), 0)
        one = jnp.ones((LANES, tg), F32)
        zq = jnp.zeros((LANES, tg), F32)
        for g in range(NG_DIFF):
            qt = qt_ref[0, t, :, g * tg:(g + 1) * tg].astype(F32)
            qa_ref[2 * g] = jnp.where(rowq < HEAD_DIM, qt,
                                      jnp.where(rowq < HEAD_DIM + 2, one, zq)).astype(BF16)
            qa_ref[2 * g + 1] = jnp.where(rowq >= HEAD_DIM, qt,
                                          jnp.where(rowq < 2, one, zq)).astype(BF16)

    def reset(c):
        m_ref[c] = jnp.full(m_ref.shape[1:], NEG_BIG, F32)
        acc_ref[c] = jnp.zeros(acc_ref.shape[1:], F32)

    def scores(buf, c, ka, n):
        s_ref[buf, c, 0:n, :] = jnp.dot(ka, qa_ref[c], preferred_element_type=F32)

    def tile(t, first_tile):
        kconst = {n: key_consts(n) for n in ((tg,) if first_tile else (tg, tk))}
        krow = lax.broadcasted_iota(jnp.int32, (tg, tg), 0)
        qcol = lax.broadcasted_iota(jnp.int32, (tg, tg), 1)
        causal = krow <= qcol
        t_next = jnp.minimum(t + 1, ntile - 1)

        def consume(buf, c, vt, cj, n, masked):
            s = s_ref[buf, c, 0:n, :]
            if masked:
                s = jnp.where(causal, s, NEG_BIG)
            m_old = m_ref[c]
            m_new = jnp.maximum(m_old, jnp.max(s, axis=0, keepdims=True) + cj)
            alpha = jnp.exp2(m_old - m_new)
            p = jnp.exp2(s - (m_new - cj)).astype(BF16)
            acc_ref[c] = alpha * acc_ref[c] + jnp.dot(vt, p, preferred_element_type=F32)
            m_ref[c] = m_new

        def keys(start, n):
            k = k_ref[0, pl.ds(pl.multiple_of(start, n), n), :]
            lane_lo, bias1, bias2 = kconst[n]
            return jnp.where(lane_lo, k, bias1), jnp.where(lane_lo, bias2, k)

        def block_bias(start):
            return slope * jnp.asarray(start, jnp.int32).astype(F32)

        def finish(g):
            a1 = acc_ref[2 * g]
            a2 = acc_ref[2 * g + 1]
            ot = (a1[0:LANES] / a1[LANES:LANES + 1]
                  - lam * (a2[0:LANES] / a2[LANES:LANES + 1]))
            ot = ot * lax.rsqrt(jnp.mean(ot * ot, axis=0, keepdims=True) + SUBLN_EPS) * subg_ref[...]
            ot = ot * (1.0 - lambda_init)
            rows = pl.ds(pl.multiple_of(t * tq + g * tg, tg), tg)
            o_ref[0, rows, :] = (ot.T * sg_ref[0, rows, :].astype(F32)).astype(BF16)

        def handoff(g):
            ka = keys(t_next * tq, tg)
            for c in (2 * g, 2 * g + 1):
                reset(c)
                scores(0, c, ka[c % 2], tg)

        base = t * tq
        if not first_tile:
            build_queries(t)
            ka = keys(base, tg)
            for c in range(nchain):
                reset(c)
                scores(0, c, ka[c % 2], tg)
        ka_first = None if first_tile else keys(jnp.int32(0), tk)
        for jj in range(NG_DIFF):
            cur, nxt = jj % 2, 1 - jj % 2
            start = base + jj * tg
            vt = vt_ref[0, t * (tq // tk) + (jj * tg) // tk, :,
                        (jj * tg) % tk:(jj * tg) % tk + tg]
            cj = block_bias(start)
            ka = keys(start + tg, tg) if jj + 1 < NG_DIFF else None
            for g in range(jj, NG_DIFF):
                for c in (2 * g, 2 * g + 1):
                    if g > jj:
                        scores(nxt, c, ka[c % 2], tg)
                    consume(cur, c, vt, cj, tg, g == jj)
                    if g == jj and not first_tile:
                        scores(0, c, ka_first[c % 2], tk)
                if g == jj and first_tile:
                    finish(g)
        if first_tile:
            return

        nblk = t * (tq // tk)

        def step(j, cur, last):
            vt = vt_ref[0, j]
            cj = block_bias(j * tk)
            kan = None if last else keys((j + 1) * tk, tk)
            for g in range(NG_DIFF):
                for c in (2 * g, 2 * g + 1):
                    if not last:
                        scores(1 - cur, c, kan[c % 2], tk)
                    consume(cur, c, vt, cj, tk, False)
                if last:
                    finish(g)

        def body(it, carry):
            step(2 * it, 0, False)
            step(2 * it + 1, 1, False)
            return carry

        lax.fori_loop(0, nblk // 2 - 1, body, 0)
        step(nblk - 2, 0, False)
        step(nblk - 1, 1, True)

    build_queries(0)
    lane_lo0, b10, b20 = key_consts(tg)
    k0 = k_ref[0, 0:tg, :]
    ka0 = (jnp.where(lane_lo0, k0, b10), jnp.where(lane_lo0, b20, k0))
    for c in range(nchain):
        reset(c)
        scores(0, c, ka0[c % 2], tg)
    tile(jnp.int32(0), True)

    def tile_body(t, carry):
        tile(t, False)
        return carry

    lax.fori_loop(1, ntile, tile_body, 0)


def _diff(slopes, lq1, lk1, lq2, lk2, subg, bqt, bk, bvt, sg, lambda_init):
    b, s, _ = bk.shape
    tg, tk, tq = TG_DIFF, TK_DIFF, TQ_DIFF
    nchain = 2 * NG_DIFF
    smem = pl.BlockSpec(memory_space=pltpu.SMEM)
    small = lambda shape: pl.BlockSpec(shape, lambda bi, h: (0, 0))
    return pl.pallas_call(
        functools.partial(_diff_kernel, lambda_init=lambda_init),
        grid=(b, DIFF_HEADS),
        in_specs=[smem,
                  small((1, HEAD_DIM)), small((1, HEAD_DIM)),
                  small((1, HEAD_DIM)), small((1, HEAD_DIM)),
                  small((LANES, 1)),
                  pl.BlockSpec((1, s // tq, LANES, tq), lambda bi, h: (bi, 0, h, 0)),
                  pl.BlockSpec((1, s, LANES), lambda bi, h: (bi, 0, h)),
                  pl.BlockSpec((1, s // tk, VROWS, tk), lambda bi, h: (bi, 0, h, 0)),
                  pl.BlockSpec((1, s, LANES), lambda bi, h: (bi, 0, A_Q // LANES + h))],
        out_specs=pl.BlockSpec((1, s, LANES), lambda bi, h: (bi, 0, h)),
        out_shape=jax.ShapeDtypeStruct((b, s, B_W), BF16),
        scratch_shapes=[pltpu.VMEM((nchain, LANES, tg), BF16),
                        pltpu.VMEM((2, nchain, tk, tg), F32),
                        pltpu.VMEM((nchain, 1, tg), F32),
                        pltpu.VMEM((nchain, VROWS, tg), F32)],
        compiler_params=pltpu.CompilerParams(
            dimension_semantics=("arbitrary", "arbitrary"),
            vmem_limit_bytes=VMEM_LIMIT),
        name="diffattn",
    )(slopes, lq1, lk1, lq2, lk2, subg, bqt, bk, bvt, sg)


def kernel(x, norm_g, w_in, sinks, lambda_q1, lambda_k1, lambda_q2, lambda_k2,
           subln_g, w_out, final_g):
    b, s, d = x.shape
    depth = norm_g.shape[0]
    diff_slopes = jnp.asarray(
        [2.0 ** (-8.0 * (h + 1) / DIFF_HEADS) for h in range(DIFF_HEADS)], F32)
    h3 = x
    for layer in range(depth):
        aqt, ak, avt, bqt, bk, bvt, sg = _inproj(
            h3, norm_g[layer].reshape(1, d), w_in[layer].astype(BF16))
        mixed_b = _diff(diff_slopes,
                        lambda_q1[layer].reshape(1, HEAD_DIM), lambda_k1[layer].reshape(1, HEAD_DIM),
                        lambda_q2[layer].reshape(1, HEAD_DIM), lambda_k2[layer].reshape(1, HEAD_DIM),
                        subln_g[layer].reshape(LANES, 1),
                        bqt, bk, bvt, sg, _lambda_init(layer))
        h3 = _swa_out(sinks[layer], aqt, ak, avt, sg, h3, mixed_b,
                      w_out[layer].astype(BF16), final_g.reshape(1, d),
                      final_norm=(layer == depth - 1))
    return h3
```

```python
import functools
import math

import jax
import jax.numpy as jnp
import numpy as np
from jax import lax
from jax.experimental import pallas as pl
from jax.experimental.pallas import tpu as pltpu

F32 = jnp.float32
BF16 = jnp.bfloat16

HEAD_DIM = 64
LANES = 128
SWA_Q_HEADS = 8
SWA_KV_HEADS = 2
WINDOW = 128
DIFF_HEADS = 4
RMS_EPS = 1e-6
SUBLN_EPS = 1e-5
NEG_BIG = -1e30
LOG2E = math.log2(math.e)

A_Q = SWA_Q_HEADS * HEAD_DIM
A_KV = SWA_KV_HEADS * HEAD_DIM
B_W = DIFF_HEADS * 2 * HEAD_DIM
MIX = A_Q + B_W
OFF_AK = A_Q
OFF_AV = OFF_AK + A_KV
OFF_BQ = OFF_AV + A_KV
OFF_BK = OFF_BQ + B_W
OFF_BV = OFF_BK + B_W
OFF_GATE = OFF_BV + B_W
IN_COLS = OFF_GATE + MIX

TM_PROJ = 1024
ROWS_OUT = 256
COLS_OUT = 256
TQ_SWA = 1024
TG_DIFF = 256
NG_DIFF = 4
TQ_DIFF = NG_DIFF * TG_DIFF
TK_DIFF = 512
VROWS = LANES + 16
VROWS_A = HEAD_DIM + 16
VMEM_LIMIT = 56 * 1024 * 1024


def _lambda_init(layer_idx):
    return 0.8 - 0.6 * math.exp(-0.3 * layer_idx)


def _nt_dot(a, b):
    return lax.dot_general(a, b, (((1,), (1,)), ((), ())), preferred_element_type=F32)


def _inproj_kernel(x_ref, g_ref, w_ref,
                   aqt_ref, ak_ref, avt_ref, bqt_ref, bk_ref, bvt_ref, sg_ref):
    x = x_ref[0]
    ms = jnp.mean(x * x, axis=-1, keepdims=True)
    h = (x * lax.rsqrt(ms + RMS_EPS) * g_ref[...]).astype(BF16)
    qscale = HEAD_DIM ** -0.5 * LOG2E

    def tok(lo, hi):
        return jnp.dot(h, w_ref[:, lo:hi], preferred_element_type=F32)

    def feat(lo, hi):
        return lax.dot_general(w_ref[:, lo:hi], h, (((0,), (1,)), ((), ())),
                               preferred_element_type=F32)

    ak_ref[0] = tok(OFF_AK, OFF_AV).astype(BF16)
    bk_ref[0] = tok(OFF_BK, OFF_BV).astype(BF16)
    gate = tok(OFF_GATE, IN_COLS)
    sg_ref[0] = (gate * (1.0 / (1.0 + jnp.exp(-gate)))).astype(BF16)

    aqt_ref[0] = (feat(0, OFF_AK) * qscale).astype(BF16)
    avt = feat(OFF_AV, OFF_BQ).astype(BF16)
    for g in range(SWA_KV_HEADS):
        avt_ref[0, g * VROWS_A:g * VROWS_A + HEAD_DIM, :] = avt[g * HEAD_DIM:(g + 1) * HEAD_DIM]
        avt_ref[0, g * VROWS_A + HEAD_DIM:(g + 1) * VROWS_A, :] = jnp.ones(
            (VROWS_A - HEAD_DIM, TM_PROJ), BF16)
    bqt = (feat(OFF_BQ, OFF_BK) * qscale).astype(BF16)
    for c in range(TM_PROJ // TQ_DIFF):
        bqt_ref[0, c] = bqt[:, c * TQ_DIFF:(c + 1) * TQ_DIFF]
    bvt = feat(OFF_BV, OFF_GATE).astype(BF16)
    ones = jnp.ones((VROWS - LANES, TK_DIFF), BF16)
    for c in range(TM_PROJ // TK_DIFF):
        for hd in range(DIFF_HEADS):
            bvt_ref[0, c, hd * VROWS:hd * VROWS + LANES, :] = (
                bvt[hd * LANES:(hd + 1) * LANES, c * TK_DIFF:(c + 1) * TK_DIFF])
            bvt_ref[0, c, hd * VROWS + LANES:(hd + 1) * VROWS, :] = ones


def _inproj(x, g, w):
    b, s, d = x.shape
    tm = TM_PROJ
    row = lambda bi, i: (bi, i, 0)
    col = lambda bi, i: (bi, 0, i)
    whole = lambda bi, i: (0, 0)
    bsd = lambda wd: jax.ShapeDtypeStruct((b, s, wd), BF16)
    bds = lambda wd: jax.ShapeDtypeStruct((b, wd, s), BF16)
    return pl.pallas_call(
        _inproj_kernel,
        grid=(b, s // tm),
        in_specs=[pl.BlockSpec((1, tm, d), row),
                  pl.BlockSpec((1, d), whole),
                  pl.BlockSpec(w.shape, whole)],
        out_specs=[pl.BlockSpec((1, A_Q, tm), col),
                   pl.BlockSpec((1, tm, A_KV), row),
                   pl.BlockSpec((1, SWA_KV_HEADS * VROWS_A, tm), col),
                   pl.BlockSpec((1, tm // TQ_DIFF, B_W, TQ_DIFF), lambda bi, i: (bi, i, 0, 0)),
                   pl.BlockSpec((1, tm, B_W), row),
                   pl.BlockSpec((1, tm // TK_DIFF, DIFF_HEADS * VROWS, TK_DIFF), lambda bi, i: (bi, i, 0, 0)),
                   pl.BlockSpec((1, tm, MIX), row)],
        out_shape=[bds(A_Q), bsd(A_KV), bds(SWA_KV_HEADS * VROWS_A),
                   jax.ShapeDtypeStruct((b, s // TQ_DIFF, B_W, TQ_DIFF), BF16), bsd(B_W),
                   jax.ShapeDtypeStruct((b, s // TK_DIFF, DIFF_HEADS * VROWS, TK_DIFF), BF16),
                   bsd(MIX)],
        compiler_params=pltpu.CompilerParams(
            dimension_semantics=("arbitrary", "arbitrary"), vmem_limit_bytes=VMEM_LIMIT),
        name="inproj",
    )(x, g, w)


def _swa_out_kernel(sinks_ref, qt_ref, kp_ref, kc_ref, vtp_ref, vtc_ref, sg_ref,
                    x_ref, mb_ref, wout_ref, fg_ref, o_ref, kcat, vtcat, ma_scr, *, final_norm):
    w_ = WINDOW
    i = pl.program_id(1)
    group = SWA_Q_HEADS // SWA_KV_HEADS
    nq = group * w_
    kcat[0:w_, :] = kp_ref[0]
    kcat[w_:, :] = kc_ref[0]
    vtcat[:, 0:w_] = vtp_ref[0]
    vtcat[:, w_:] = vtc_ref[0]

    kidx = lax.broadcasted_iota(jnp.int32, (2 * w_, nq), 0)
    qidx = lax.broadcasted_iota(jnp.int32, (2 * w_, nq), 1) & (w_ - 1)
    in_cur = kidx >= w_
    band = (in_cur & (kidx - w_ <= qidx)) | ((kidx < w_) & (kidx > qidx))
    first_band = band & ((i > 0) | in_cur)

    lane = lax.broadcasted_iota(jnp.int32, (2 * w_, LANES), 1)
    kpos = lax.broadcasted_iota(jnp.int32, (2 * w_, LANES), 0).astype(F32)
    zk = jnp.zeros((2 * w_, LANES), F32)
    kmask = (lane < HEAD_DIM, lane >= HEAD_DIM)
    kpos_lanes = (jnp.where((lane == HEAD_DIM) | (lane == HEAD_DIM + 1), kpos, zk).astype(BF16),
                  jnp.where(lane < 2, kpos, zk).astype(BF16))

    hrow = lax.broadcasted_iota(jnp.int32, (HEAD_DIM, nq), 0)
    hcol = lax.broadcasted_iota(jnp.int32, (HEAD_DIM, nq), 1)

    def slope_rows(g):
        out = jnp.zeros((HEAD_DIM, nq), F32)
        for u in range(group):
            sl = 2.0 ** (-8.0 * (g * group + u + 1) / SWA_Q_HEADS) * LOG2E
            hi = float(np.asarray(sl, dtype=BF16).astype(np.float32))
            lo = sl - hi
            in_head = (hcol >= u * w_) & (hcol < (u + 1) * w_)
            out = jnp.where(in_head & (hrow == 0), hi, jnp.where(in_head & (hrow == 1), lo, out))
        return out.astype(BF16)

    srows = [slope_rows(g) for g in range(SWA_KV_HEADS)]

    qpos = (lax.broadcasted_iota(jnp.int32, (1, w_), 1) + w_).astype(F32)

    def sink_row(g):
        return jnp.concatenate(
            [(sinks_ref[g * group + u] + 2.0 ** (-8.0 * (g * group + u + 1) / SWA_Q_HEADS) * qpos) * LOG2E
             for u in range(group)], axis=1)

    sinkv = [sink_row(g) for g in range(SWA_KV_HEADS)]

    def scores(w, g):
        keys = kcat[w * w_:(w + 2) * w_, :]
        ka = jnp.where(kmask[g], keys, kpos_lanes[g])
        qh = jnp.concatenate(
            [qt_ref[0, (g * group + u) * HEAD_DIM:(g * group + u + 1) * HEAD_DIM, w * w_:(w + 1) * w_]
             for u in range(group)], axis=1)
        wq = jnp.concatenate([qh, srows[g]] if g == 0 else [srows[g], qh], axis=0)
        return jnp.dot(ka, wq, preferred_element_type=F32)

    def consume(w, g, s):
        s = jnp.where(first_band if w == 0 else band, s, NEG_BIG)
        m = jnp.maximum(jnp.max(s, axis=0, keepdims=True), sinkv[g])
        p = jnp.exp2(s - m).astype(BF16)
        vt = vtcat[g * VROWS_A:(g + 1) * VROWS_A, w * w_:(w + 2) * w_]
        o = jnp.dot(vt, p, preferred_element_type=F32)
        denom = o[HEAD_DIM:HEAD_DIM + 1] + jnp.exp2(sinkv[g] - m)
        ot = o[0:HEAD_DIM] / denom
        rows = slice(w * w_, (w + 1) * w_)
        for pair in range(group // 2):
            cols = slice((g * (group // 2) + pair) * LANES, (g * (group // 2) + pair + 1) * LANES)
            two = jnp.concatenate([ot[:, (2 * pair) * w_:(2 * pair + 1) * w_],
                                   ot[:, (2 * pair + 1) * w_:(2 * pair + 2) * w_]], axis=0)
            ma_scr[rows, cols] = (two.T * sg_ref[0, rows, cols].astype(F32)).astype(BF16)

    npiece = MIX // COLS_OUT

    def project(k, j):
        rows = slice(k * ROWS_OUT, (k + 1) * ROWS_OUT)
        cols = slice(j * COLS_OUT, (j + 1) * COLS_OUT)
        o_ref[0, rows, cols] = (
            x_ref[0, rows, cols]
            + jnp.dot(ma_scr[rows, :], wout_ref[0:A_Q, cols], preferred_element_type=F32)
            + jnp.dot(mb_ref[0, rows, :], wout_ref[A_Q:MIX, cols], preferred_element_type=F32))
        if final_norm and j == npiece - 1:
            y = o_ref[0, rows, :]
            ms = jnp.mean(y * y, axis=-1, keepdims=True)
            o_ref[0, rows, :] = y * lax.rsqrt(ms + RMS_EPS) * fg_ref[...]

    chains = [(w, g) for w in range(TQ_SWA // w_) for g in range(SWA_KV_HEADS)]
    chains_per_chunk = (ROWS_OUT // w_) * SWA_KV_HEADS
    ahead = 2
    pending = [scores(*ch) for ch in chains[:ahead]]
    ready = []
    for n, (w, g) in enumerate(chains):
        if n + ahead < len(chains):
            pending.append(scores(*chains[n + ahead]))
        consume(w, g, pending.pop(0))
        if (n + 1) % chains_per_chunk == 0:
            ready += [(n // chains_per_chunk, j) for j in range(npiece)]
        if ready:
            project(*ready.pop(0))
    for unit in ready:
        project(*unit)


def _swa_out(sinks, aqt, ak, avt, sg, x, mixed_b, wout, fg, final_norm):
    b, s, d = x.shape
    tq = TQ_SWA
    wpt = tq // WINDOW
    vr = SWA_KV_HEADS * VROWS_A
    prev_w = lambda i: jnp.maximum(i * wpt - 1, 0)
    tile = lambda bi, i: (bi, i, 0)
    whole = lambda bi, i: (0, 0)
    return pl.pallas_call(
        functools.partial(_swa_out_kernel, final_norm=final_norm),
        grid=(b, s // tq),
        in_specs=[pl.BlockSpec(memory_space=pltpu.SMEM),
                  pl.BlockSpec((1, A_Q, tq), lambda bi, i: (bi, 0, i)),
                  pl.BlockSpec((1, WINDOW, A_KV), lambda bi, i: (bi, prev_w(i), 0)),
                  pl.BlockSpec((1, tq, A_KV), tile),
                  pl.BlockSpec((1, vr, WINDOW), lambda bi, i: (bi, 0, prev_w(i))),
                  pl.BlockSpec((1, vr, tq), lambda bi, i: (bi, 0, i)),
                  pl.BlockSpec((1, tq, A_Q), tile),
                  pl.BlockSpec((1, tq, d), tile),
                  pl.BlockSpec((1, tq, B_W), tile),
                  pl.BlockSpec((MIX, d), whole),
                  pl.BlockSpec((1, d), whole)],
        out_specs=pl.BlockSpec((1, tq, d), tile),
        out_shape=jax.ShapeDtypeStruct((b, s, d), F32),
        scratch_shapes=[pltpu.VMEM((tq + WINDOW, A_KV), BF16),
                        pltpu.VMEM((vr, tq + WINDOW), BF16),
                        pltpu.VMEM((tq, A_Q), BF16)],
        compiler_params=pltpu.CompilerParams(
            dimension_semantics=("arbitrary", "arbitrary"), vmem_limit_bytes=VMEM_LIMIT),
        name="swa_out",
    )(sinks, aqt, ak, ak, avt, avt, sg, x, mixed_b, wout, fg)


def _diff_kernel(slopes_ref, lq1_ref, lk1_ref, lq2_ref, lk2_ref, subg_ref,
                 qt_ref, k_ref, vt_ref, sg_ref, o_ref,
                 qa_ref, qn_ref, s_ref, m_ref, acc_ref, *, lambda_init):
    tg, tk, tq = TG_DIFF, TK_DIFF, TQ_DIFF
    nchain = 2 * NG_DIFF
    ntile = qt_ref.shape[1]
    hd = pl.program_id(1)
    slope = slopes_ref[hd] * LOG2E

    def key_consts(n):
        kk = lax.broadcasted_iota(jnp.int32, (n, LANES), 0).astype(F32)
        lane = lax.broadcasted_iota(jnp.int32, (n, LANES), 1)
        ab = slope * kk
        ab_hi = ab.astype(BF16).astype(F32)
        ab_lo = ab - ab_hi
        zk = jnp.zeros((n, LANES), F32)
        bias1 = jnp.where(lane == HEAD_DIM, ab_hi,
                          jnp.where(lane == HEAD_DIM + 1, ab_lo, zk)).astype(BF16)
        bias2 = jnp.where(lane == 0, ab_hi, jnp.where(lane == 1, ab_lo, zk)).astype(BF16)
        return lane < HEAD_DIM, bias1, bias2

    lam = (jnp.exp(jnp.sum(lq1_ref[...] * lk1_ref[...], axis=-1, keepdims=True))
           - jnp.exp(jnp.sum(lq2_ref[...] * lk2_ref[...], axis=-1, keepdims=True))
           + lambda_init)

    def build_queries(t, dst_ref):
        rowq = lax.broadcasted_iota(jnp.int32, (LANES, tg), 0)
        one = jnp.ones((LANES, tg), F32)
        zq = jnp.zeros((LANES, tg), F32)
        for g in range(NG_DIFF):
            qt = qt_ref[0, t, :, g * tg:(g + 1) * tg].astype(F32)
            dst_ref[2 * g] = jnp.where(rowq < HEAD_DIM, qt,
                                       jnp.where(rowq < HEAD_DIM + 2, one, zq)).astype(BF16)
            dst_ref[2 * g + 1] = jnp.where(rowq >= HEAD_DIM, qt,
                                           jnp.where(rowq < 2, one, zq)).astype(BF16)

    def reset(c):
        m_ref[c] = jnp.full(m_ref.shape[1:], NEG_BIG, F32)
        acc_ref[c] = jnp.zeros(acc_ref.shape[1:], F32)

    def scores(buf, c, ka, n, q_ref=qa_ref):
        s_ref[buf, c, 0:n, :] = jnp.dot(ka, q_ref[c], preferred_element_type=F32)

    def tile(t, first_tile):
        kconst = {n: key_consts(n) for n in ((tg,) if first_tile else (tg, tk))}
        krow = lax.broadcasted_iota(jnp.int32, (tg, tg), 0)
        qcol = lax.broadcasted_iota(jnp.int32, (tg, tg), 1)
        causal = krow <= qcol
        t_next = jnp.minimum(t + 1, ntile - 1)

        def consume(buf, c, vt, cj, n, masked):
            s = s_ref[buf, c, 0:n, :]
            if masked:
                s = jnp.where(causal, s, NEG_BIG)
            m_old = m_ref[c]
            m_new = jnp.maximum(m_old, jnp.max(s, axis=0, keepdims=True) + cj)
            alpha = jnp.exp2(m_old - m_new)
            p = jnp.exp2(s - (m_new - cj)).astype(BF16)
            acc_ref[c] = alpha * acc_ref[c] + jnp.dot(vt, p, preferred_element_type=F32)
            m_ref[c] = m_new

        def keys(start, n):
            k = k_ref[0, pl.ds(pl.multiple_of(start, n), n), :]
            lane_lo, bias1, bias2 = kconst[n]
            return jnp.where(lane_lo, k, bias1), jnp.where(lane_lo, bias2, k)

        def block_bias(start):
            return slope * jnp.asarray(start, jnp.int32).astype(F32)

        def finish(g):
            a1 = acc_ref[2 * g]
            a2 = acc_ref[2 * g + 1]
            ot = (a1[0:LANES] / a1[LANES:LANES + 1]
                  - lam * (a2[0:LANES] / a2[LANES:LANES + 1]))
            ot = ot * lax.rsqrt(jnp.mean(ot * ot, axis=0, keepdims=True) + SUBLN_EPS) * subg_ref[...]
            ot = ot * (1.0 - lambda_init)
            rows = pl.ds(pl.multiple_of(t * tq + g * tg, tg), tg)
            o_ref[0, rows, :] = (ot.T * sg_ref[0, rows, :].astype(F32)).astype(BF16)

        def handoff(g):
            ka = keys(t_next * tq, tg)
            for c in (2 * g, 2 * g + 1):
                reset(c)
                scores(0, c, ka[c % 2], tg, qn_ref)

        base = t * tq
        if not first_tile:
            build_queries(t, qa_ref)
        ka_first = None if first_tile else keys(jnp.int32(0), tk)
        for jj in range(NG_DIFF):
            cur, nxt = jj % 2, 1 - jj % 2
            start = base + jj * tg
            vt = vt_ref[0, t * (tq // tk) + (jj * tg) // tk, :,
                        (jj * tg) % tk:(jj * tg) % tk + tg]
            cj = block_bias(start)
            ka = keys(start + tg, tg) if jj + 1 < NG_DIFF else None
            for g in range(jj, NG_DIFF):
                for c in (2 * g, 2 * g + 1):
                    if g > jj:
                        scores(nxt, c, ka[c % 2], tg)
                    consume(cur, c, vt, cj, tg, g == jj)
                    if g == jj and not first_tile:
                        scores(0, c, ka_first[c % 2], tk)
                if g == jj and first_tile:
                    finish(g)
        if first_tile:
            build_queries(t_next, qn_ref)
            for g in range(NG_DIFF):
                handoff(g)
            return

        nblk = t * (tq // tk)

        def step(j, cur, last):
            vt = vt_ref[0, j]
            cj = block_bias(j * tk)
            kan = None if last else keys((j + 1) * tk, tk)
            if last:
                build_queries(t_next, qn_ref)
            for g in range(NG_DIFF):
                for c in (2 * g, 2 * g + 1):
                    if not last:
                        scores(1 - cur, c, kan[c % 2], tk)
                    consume(cur, c, vt, cj, tk, False)
                if last:
                    finish(g)
                    handoff(g)

        def body(it, carry):
            step(2 * it, 0, False)
            step(2 * it + 1, 1, False)
            return carry

        lax.fori_loop(0, nblk // 2 - 1, body, 0)
        step(nblk - 2, 0, False)
        step(nblk - 1, 1, True)

    build_queries(0, qa_ref)
    lane_lo0, b10, b20 = key_consts(tg)
    k0 = k_ref[0, 0:tg, :]
    ka0 = (jnp.where(lane_lo0, k0, b10), jnp.where(lane_lo0, b20, k0))
    for c in range(nchain):
        reset(c)
        scores(0, c, ka0[c % 2], tg)
    tile(jnp.int32(0), True)

    def tile_body(t, carry):
        tile(t, False)
        return carry

    lax.fori_loop(1, ntile, tile_body, 0)


def _diff(slopes, lq1, lk1, lq2, lk2, subg, bqt, bk, bvt, sg, lambda_init):
    b, s, _ = bk.shape
    tg, tk, tq = TG_DIFF, TK_DIFF, TQ_DIFF
    nchain = 2 * NG_DIFF
    smem = pl.BlockSpec(memory_space=pltpu.SMEM)
    small = lambda shape: pl.BlockSpec(shape, lambda bi, h: (0, 0))
    return pl.pallas_call(
        functools.partial(_diff_kernel, lambda_init=lambda_init),
        grid=(b, DIFF_HEADS),
        in_specs=[smem,
                  small((1, HEAD_DIM)), small((1, HEAD_DIM)),
                  small((1, HEAD_DIM)), small((1, HEAD_DIM)),
                  small((LANES, 1)),
                  pl.BlockSpec((1, s // tq, LANES, tq), lambda bi, h: (bi, 0, h, 0)),
                  pl.BlockSpec((1, s, LANES), lambda bi, h: (bi, 0, h)),
                  pl.BlockSpec((1, s // tk, VROWS, tk), lambda bi, h: (bi, 0, h, 0)),
                  pl.BlockSpec((1, s, LANES), lambda bi, h: (bi, 0, A_Q // LANES + h))],
        out_specs=pl.BlockSpec((1, s, LANES), lambda bi, h: (bi, 0, h)),
        out_shape=jax.ShapeDtypeStruct((b, s, B_W), BF16),
        scratch_shapes=[pltpu.VMEM((nchain, LANES, tg), BF16),
                        pltpu.VMEM((nchain, LANES, tg), BF16),
                        pltpu.VMEM((2, nchain, tk, tg), F32),
                        pltpu.VMEM((nchain, 1, tg), F32),
                        pltpu.VMEM((nchain, VROWS, tg), F32)],
        compiler_params=pltpu.CompilerParams(
            dimension_semantics=("arbitrary", "arbitrary"),
            vmem_limit_bytes=VMEM_LIMIT),
        name="diffattn",
    )(slopes, lq1, lk1, lq2, lk2, subg, bqt, bk, bvt, sg)


def kernel(x, norm_g, w_in, sinks, lambda_q1, lambda_k1, lambda_q2, lambda_k2,
           subln_g, w_out, final_g):
    b, s, d = x.shape
    depth = norm_g.shape[0]
    diff_slopes = jnp.asarray(
        [2.0 ** (-8.0 * (h + 1) / DIFF_HEADS) for h in range(DIFF_HEADS)], F32)
    h3 = x
    for layer in range(depth):
        aqt, ak, avt, bqt, bk, bvt, sg = _inproj(
            h3, norm_g[layer].reshape(1, d), w_in[layer].astype(BF16))
        mixed_b = _diff(diff_slopes,
                        lambda_q1[layer].reshape(1, HEAD_DIM), lambda_k1[layer].reshape(1, HEAD_DIM),
                        lambda_q2[layer].reshape(1, HEAD_DIM), lambda_k2[layer].reshape(1, HEAD_DIM),
                        subln_g[layer].reshape(LANES, 1),
                        bqt, bk, bvt, sg, _lambda_init(layer))
        h3 = _swa_out(sinks[layer], aqt, ak, avt, sg, h3, mixed_b,
                      w_out[layer].astype(BF16), final_g.reshape(1, d),
                      final_norm=(layer == depth - 1))
    return h3
```

```python
import functools
import math

import jax
import jax.numpy as jnp
import numpy as np
from jax import lax
from jax.experimental import pallas as pl
from jax.experimental.pallas import tpu as pltpu

F32 = jnp.float32
BF16 = jnp.bfloat16

HEAD_DIM = 64
LANES = 128
SWA_Q_HEADS = 8
SWA_KV_HEADS = 2
WINDOW = 128
DIFF_HEADS = 4
RMS_EPS = 1e-6
SUBLN_EPS = 1e-5
NEG_BIG = -1e30
LOG2E = math.log2(math.e)

A_Q = SWA_Q_HEADS * HEAD_DIM
A_KV = SWA_KV_HEADS * HEAD_DIM
B_W = DIFF_HEADS * 2 * HEAD_DIM
MIX = A_Q + B_W
OFF_AK = A_Q
OFF_AV = OFF_AK + A_KV
OFF_BQ = OFF_AV + A_KV
OFF_BK = OFF_BQ + B_W
OFF_BV = OFF_BK + B_W
OFF_GATE = OFF_BV + B_W
IN_COLS = OFF_GATE + MIX

TM_PROJ = 1024
ROWS_OUT = 256
COLS_OUT = 256
TQ_SWA = 1024
TG_DIFF = 256
NG_DIFF = 4
TQ_DIFF = NG_DIFF * TG_DIFF
TK_DIFF = 512
VROWS = LANES + 16
VROWS_A = HEAD_DIM + 16
VMEM_LIMIT = 56 * 1024 * 1024


def _lambda_init(layer_idx):
    return 0.8 - 0.6 * math.exp(-0.3 * layer_idx)


def _nt_dot(a, b):
    return lax.dot_general(a, b, (((1,), (1,)), ((), ())), preferred_element_type=F32)


def _inproj_kernel(x_ref, g_ref, w_ref,
                   aqt_ref, ak_ref, avt_ref, bqt_ref, bk_ref, bvt_ref, sg_ref):
    x = x_ref[0]
    ms = jnp.mean(x * x, axis=-1, keepdims=True)
    h = (x * lax.rsqrt(ms + RMS_EPS) * g_ref[...]).astype(BF16)
    qscale = HEAD_DIM ** -0.5 * LOG2E

    def tok(lo, hi):
        return jnp.dot(h, w_ref[:, lo:hi], preferred_element_type=F32)

    def feat(lo, hi):
        return lax.dot_general(w_ref[:, lo:hi], h, (((0,), (1,)), ((), ())),
                               preferred_element_type=F32)

    ak_ref[0] = tok(OFF_AK, OFF_AV).astype(BF16)
    bk_ref[0] = tok(OFF_BK, OFF_BV).astype(BF16)
    gate = tok(OFF_GATE, IN_COLS)
    sg_ref[0] = (gate * (1.0 / (1.0 + jnp.exp(-gate)))).astype(BF16)

    aqt_ref[0] = (feat(0, OFF_AK) * qscale).astype(BF16)
    avt = feat(OFF_AV, OFF_BQ).astype(BF16)
    for g in range(SWA_KV_HEADS):
        avt_ref[0, g * VROWS_A:g * VROWS_A + HEAD_DIM, :] = avt[g * HEAD_DIM:(g + 1) * HEAD_DIM]
        avt_ref[0, g * VROWS_A + HEAD_DIM:(g + 1) * VROWS_A, :] = jnp.ones(
            (VROWS_A - HEAD_DIM, TM_PROJ), BF16)
    bqt = (feat(OFF_BQ, OFF_BK) * qscale).astype(BF16)
    for c in range(TM_PROJ // TQ_DIFF):
        bqt_ref[0, c] = bqt[:, c * TQ_DIFF:(c + 1) * TQ_DIFF]
    bvt = feat(OFF_BV, OFF_GATE).astype(BF16)
    ones = jnp.ones((VROWS - LANES, TK_DIFF), BF16)
    for c in range(TM_PROJ // TK_DIFF):
        for hd in range(DIFF_HEADS):
            bvt_ref[0, c, hd * VROWS:hd * VROWS + LANES, :] = (
                bvt[hd * LANES:(hd + 1) * LANES, c * TK_DIFF:(c + 1) * TK_DIFF])
            bvt_ref[0, c, hd * VROWS + LANES:(hd + 1) * VROWS, :] = ones


def _inproj(x, g, w):
    b, s, d = x.shape
    tm = TM_PROJ
    row = lambda bi, i: (bi, i, 0)
    col = lambda bi, i: (bi, 0, i)
    whole = lambda bi, i: (0, 0)
    bsd = lambda wd: jax.ShapeDtypeStruct((b, s, wd), BF16)
    bds = lambda wd: jax.ShapeDtypeStruct((b, wd, s), BF16)
    return pl.pallas_call(
        _inproj_kernel,
        grid=(b, s // tm),
        in_specs=[pl.BlockSpec((1, tm, d), row),
                  pl.BlockSpec((1, d), whole),
                  pl.BlockSpec(w.shape, whole)],
        out_specs=[pl.BlockSpec((1, A_Q, tm), col),
                   pl.BlockSpec((1, tm, A_KV), row),
                   pl.BlockSpec((1, SWA_KV_HEADS * VROWS_A, tm), col),
                   pl.BlockSpec((1, tm // TQ_DIFF, B_W, TQ_DIFF), lambda bi, i: (bi, i, 0, 0)),
                   pl.BlockSpec((1, tm, B_W), row),
                   pl.BlockSpec((1, tm // TK_DIFF, DIFF_HEADS * VROWS, TK_DIFF), lambda bi, i: (bi, i, 0, 0)),
                   pl.BlockSpec((1, tm, MIX), row)],
        out_shape=[bds(A_Q), bsd(A_KV), bds(SWA_KV_HEADS * VROWS_A),
                   jax.ShapeDtypeStruct((b, s // TQ_DIFF, B_W, TQ_DIFF), BF16), bsd(B_W),
                   jax.ShapeDtypeStruct((b, s // TK_DIFF, DIFF_HEADS * VROWS, TK_DIFF), BF16),
                   bsd(MIX)],
        compiler_params=pltpu.CompilerParams(
            dimension_semantics=("arbitrary", "arbitrary"), vmem_limit_bytes=VMEM_LIMIT),
        name="inproj",
    )(x, g, w)


def _swa_out_kernel(sinks_ref, qt_ref, kp_ref, kc_ref, vtp_ref, vtc_ref, sg_ref,
                    x_ref, mb_ref, wout_ref, fg_ref, o_ref, kcat, vtcat, ma_scr, *, final_norm):
    w_ = WINDOW
    i = pl.program_id(1)
    group = SWA_Q_HEADS // SWA_KV_HEADS
    nq = group * w_
    kcat[0:w_, :] = kp_ref[0]
    kcat[w_:, :] = kc_ref[0]
    vtcat[:, 0:w_] = vtp_ref[0]
    vtcat[:, w_:] = vtc_ref[0]

    kidx = lax.broadcasted_iota(jnp.int32, (2 * w_, nq), 0)
    qidx = lax.broadcasted_iota(jnp.int32, (2 * w_, nq), 1) & (w_ - 1)
    in_cur = kidx >= w_
    band = (in_cur & (kidx - w_ <= qidx)) | ((kidx < w_) & (kidx > qidx))
    first_band = band & ((i > 0) | in_cur)

    lane = lax.broadcasted_iota(jnp.int32, (2 * w_, LANES), 1)
    kpos = lax.broadcasted_iota(jnp.int32, (2 * w_, LANES), 0).astype(F32)
    zk = jnp.zeros((2 * w_, LANES), F32)
    kmask = (lane < HEAD_DIM, lane >= HEAD_DIM)
    kpos_lanes = (jnp.where((lane == HEAD_DIM) | (lane == HEAD_DIM + 1), kpos, zk).astype(BF16),
                  jnp.where(lane < 2, kpos, zk).astype(BF16))

    hrow = lax.broadcasted_iota(jnp.int32, (HEAD_DIM, nq), 0)
    hcol = lax.broadcasted_iota(jnp.int32, (HEAD_DIM, nq), 1)

    def slope_rows(g):
        out = jnp.zeros((HEAD_DIM, nq), F32)
        for u in range(group):
            sl = 2.0 ** (-8.0 * (g * group + u + 1) / SWA_Q_HEADS) * LOG2E
            hi = float(np.asarray(sl, dtype=BF16).astype(np.float32))
            lo = sl - hi
            in_head = (hcol >= u * w_) & (hcol < (u + 1) * w_)
            out = jnp.where(in_head & (hrow == 0), hi, jnp.where(in_head & (hrow == 1), lo, out))
        return out.astype(BF16)

    srows = [slope_rows(g) for g in range(SWA_KV_HEADS)]

    qpos = (lax.broadcasted_iota(jnp.int32, (1, w_), 1) + w_).astype(F32)

    def sink_row(g):
        return jnp.concatenate(
            [(sinks_ref[g * group + u] + 2.0 ** (-8.0 * (g * group + u + 1) / SWA_Q_HEADS) * qpos) * LOG2E
             for u in range(group)], axis=1)

    sinkv = [sink_row(g) for g in range(SWA_KV_HEADS)]

    def scores(w, g):
        keys = kcat[w * w_:(w + 2) * w_, :]
        ka = jnp.where(kmask[g], keys, kpos_lanes[g])
        qh = jnp.concatenate(
            [qt_ref[0, (g * group + u) * HEAD_DIM:(g * group + u + 1) * HEAD_DIM, w * w_:(w + 1) * w_]
             for u in range(group)], axis=1)
        wq = jnp.concatenate([qh, srows[g]] if g == 0 else [srows[g], qh], axis=0)
        return jnp.dot(ka, wq, preferred_element_type=F32)

    def consume(w, g, s):
        s = jnp.where(first_band if w == 0 else band, s, NEG_BIG)
        m = jnp.maximum(jnp.max(s, axis=0, keepdims=True), sinkv[g])
        p = jnp.exp2(s - m).astype(BF16)
        vt = vtcat[g * VROWS_A:(g + 1) * VROWS_A, w * w_:(w + 2) * w_]
        o = jnp.dot(vt, p, preferred_element_type=F32)
        denom = o[HEAD_DIM:HEAD_DIM + 1] + jnp.exp2(sinkv[g] - m)
        ot = o[0:HEAD_DIM] / denom
        rows = slice(w * w_, (w + 1) * w_)
        for pair in range(group // 2):
            cols = slice((g * (group // 2) + pair) * LANES, (g * (group // 2) + pair + 1) * LANES)
            two = jnp.concatenate([ot[:, (2 * pair) * w_:(2 * pair + 1) * w_],
                                   ot[:, (2 * pair + 1) * w_:(2 * pair + 2) * w_]], axis=0)
            ma_scr[rows, cols] = (two.T * sg_ref[0, rows, cols].astype(F32)).astype(BF16)

    npiece = MIX // COLS_OUT

    def project(k, j):
        rows = slice(k * ROWS_OUT, (k + 1) * ROWS_OUT)
        cols = slice(j * COLS_OUT, (j + 1) * COLS_OUT)
        o_ref[0, rows, cols] = (
            x_ref[0, rows, cols]
            + jnp.dot(ma_scr[rows, :], wout_ref[0:A_Q, cols], preferred_element_type=F32)
            + jnp.dot(mb_ref[0, rows, :], wout_ref[A_Q:MIX, cols], preferred_element_type=F32))
        if final_norm and j == npiece - 1:
            y = o_ref[0, rows, :]
            ms = jnp.mean(y * y, axis=-1, keepdims=True)
            o_ref[0, rows, :] = y * lax.rsqrt(ms + RMS_EPS) * fg_ref[...]

    chains = [(w, g) for w in range(TQ_SWA // w_) for g in range(SWA_KV_HEADS)]
    chains_per_chunk = (ROWS_OUT // w_) * SWA_KV_HEADS
    ahead = 2
    pending = [scores(*ch) for ch in chains[:ahead]]
    ready = []
    for n, (w, g) in enumerate(chains):
        if n + ahead < len(chains):
            pending.append(scores(*chains[n + ahead]))
        consume(w, g, pending.pop(0))
        if (n + 1) % chains_per_chunk == 0:
            ready += [(n // chains_per_chunk, j) for j in range(npiece)]
        if ready:
            project(*ready.pop(0))
    for unit in ready:
        project(*unit)


def _swa_out(sinks, aqt, ak, avt, sg, x, mixed_b, wout, fg, final_norm):
    b, s, d = x.shape
    tq = TQ_SWA
    wpt = tq // WINDOW
    vr = SWA_KV_HEADS * VROWS_A
    prev_w = lambda i: jnp.maximum(i * wpt - 1, 0)
    tile = lambda bi, i: (bi, i, 0)
    whole = lambda bi, i: (0, 0)
    return pl.pallas_call(
        functools.partial(_swa_out_kernel, final_norm=final_norm),
        grid=(b, s // tq),
        in_specs=[pl.BlockSpec(memory_space=pltpu.SMEM),
                  pl.BlockSpec((1, A_Q, tq), lambda bi, i: (bi, 0, i)),
                  pl.BlockSpec((1, WINDOW, A_KV), lambda bi, i: (bi, prev_w(i), 0)),
                  pl.BlockSpec((1, tq, A_KV), tile),
                  pl.BlockSpec((1, vr, WINDOW), lambda bi, i: (bi, 0, prev_w(i))),
                  pl.BlockSpec((1, vr, tq), lambda bi, i: (bi, 0, i)),
                  pl.BlockSpec((1, tq, A_Q), tile),
                  pl.BlockSpec((1, tq, d), tile),
                  pl.BlockSpec((1, tq, B_W), tile),
                  pl.BlockSpec((MIX, d), whole),
                  pl.BlockSpec((1, d), whole)],
        out_specs=pl.BlockSpec((1, tq, d), tile),
        out_shape=jax.ShapeDtypeStruct((b, s, d), F32),
        scratch_shapes=[pltpu.VMEM((tq + WINDOW, A_KV), BF16),
                        pltpu.VMEM((vr, tq + WINDOW), BF16),
                        pltpu.VMEM((tq, A_Q), BF16)],
        compiler_params=pltpu.CompilerParams(
            dimension_semantics=("arbitrary", "arbitrary"), vmem_limit_bytes=VMEM_LIMIT),
        name="swa_out",
    )(sinks, aqt, ak, ak, avt, avt, sg, x, mixed_b, wout, fg)


def _diff_kernel(slopes_ref, lq1_ref, lk1_ref, lq2_ref, lk2_ref, subg_ref,
                 qt_ref, k_ref, vt_ref, sg_ref, o_ref,
                 qa_ref, s_ref, m_ref, acc_ref, *, lambda_init):
    tg, tk, tq = TG_DIFF, TK_DIFF, TQ_DIFF
    nchain = 2 * NG_DIFF
    ntile = qt_ref.shape[1]
    hd = pl.program_id(1)
    slope = slopes_ref[hd] * LOG2E

    def key_consts(n):
        kk = lax.broadcasted_iota(jnp.int32, (n, LANES), 0).astype(F32)
        lane = lax.broadcasted_iota(jnp.int32, (n, LANES), 1)
        ab = slope * kk
        ab_hi = ab.astype(BF16).astype(F32)
        ab_lo = ab - ab_hi
        zk = jnp.zeros((n, LANES), F32)
        bias1 = jnp.where(lane == HEAD_DIM, ab_hi,
                          jnp.where(lane == HEAD_DIM + 1, ab_lo, zk)).astype(BF16)
        bias2 = jnp.where(lane == 0, ab_hi, jnp.where(lane == 1, ab_lo, zk)).astype(BF16)
        return lane < HEAD_DIM, bias1, bias2

    lam = (jnp.exp(jnp.sum(lq1_ref[...] * lk1_ref[...], axis=-1, keepdims=True))
           - jnp.exp(jnp.sum(lq2_ref[...] * lk2_ref[...], axis=-1, keepdims=True))
           + lambda_init)

    def build_queries(t):
        rowq = lax.broadcasted_iota(jnp.int32, (LANES, tg), 0)
        one = jnp.ones((LANES, tg), F32)
        zq = jnp.zeros((LANES, tg), F32)
        for g in range(NG_DIFF):
            qt = qt_ref[0, t, :, g * tg:(g + 1) * tg].astype(F32)
            qa_ref[2 * g] = jnp.where(rowq < HEAD_DIM, qt,
                                      jnp.where(rowq < HEAD_DIM + 2, one, zq)).astype(BF16)
            qa_ref[2 * g + 1] = jnp.where(rowq >= HEAD_DIM, qt,
                                          jnp.where(rowq < 2, one, zq)).astype(BF16)

    def reset(c):
        m_ref[c] = jnp.full(m_ref.shape[1:], NEG_BIG, F32)
        acc_ref[c] = jnp.zeros(acc_ref.shape[1:], F32)

    def scores(buf, c, ka, n):
        s_ref[buf, c, 0:n, :] = jnp.dot(ka, qa_ref[c], preferred_element_type=F32)

    def tile(t, first_tile):
        kconst = {n: key_consts(n) for n in ((tg,) if first_tile else (tg, tk))}
        krow = lax.broadcasted_iota(jnp.int32, (tg, tg), 0)
        qcol = lax.broadcasted_iota(jnp.int32, (tg, tg), 1)
        causal = krow <= qcol

        def consume(buf, c, vt, cj, n, masked):
            s = s_ref[buf, c, 0:n, :]
            if masked:
                s = jnp.where(causal, s, NEG_BIG)
            m_old = m_ref[c]
            m_new = jnp.maximum(m_old, jnp.max(s, axis=0, keepdims=True) + cj)
            alpha = jnp.exp2(m_old - m_new)
            p = jnp.exp2(s - (m_new - cj)).astype(BF16)
            acc_ref[c] = alpha * acc_ref[c] + jnp.dot(vt, p, preferred_element_type=F32)
            m_ref[c] = m_new

        def keys(start, n):
            k = k_ref[0, pl.ds(pl.multiple_of(start, n), n), :]
            lane_lo, bias1, bias2 = kconst[n]
            return jnp.where(lane_lo, k, bias1), jnp.where(lane_lo, bias2, k)

        def block_bias(start):
            return slope * jnp.asarray(start, jnp.int32).astype(F32)

        def finish(g):
            a1 = acc_ref[2 * g]
            a2 = acc_ref[2 * g + 1]
            ot = (a1[0:LANES] / a1[LANES:LANES + 1]
                  - lam * (a2[0:LANES] / a2[LANES:LANES + 1]))
            ot = ot * lax.rsqrt(jnp.mean(ot * ot, axis=0, keepdims=True) + SUBLN_EPS) * subg_ref[...]
            ot = ot * (1.0 - lambda_init)
            rows = pl.ds(pl.multiple_of(t * tq + g * tg, tg), tg)
            o_ref[0, rows, :] = (ot.T * sg_ref[0, rows, :].astype(F32)).astype(BF16)

        base = t * tq
        build_queries(t)
        ka = keys(base, tg)
        for c in range(nchain):
            reset(c)
            scores(0, c, ka[c % 2], tg)
        ka_first = None if first_tile else keys(jnp.int32(0), tk)
        for jj in range(NG_DIFF):
            cur, nxt = jj % 2, 1 - jj % 2
            start = base + jj * tg
            vt = vt_ref[0, t * (tq // tk) + (jj * tg) // tk, :,
                        (jj * tg) % tk:(jj * tg) % tk + tg]
            cj = block_bias(start)
            ka = keys(start + tg, tg) if jj + 1 < NG_DIFF else None
            for g in range(jj, NG_DIFF):
                for c in (2 * g, 2 * g + 1):
                    if g > jj:
                        scores(nxt, c, ka[c % 2], tg)
                    consume(cur, c, vt, cj, tg, g == jj)
                    if g == jj and not first_tile:
                        scores(0, c, ka_first[c % 2], tk)
                if g == jj and first_tile:
                    finish(g)
        if first_tile:
            return

        nblk = t * (tq // tk)

        def step(j, cur, last):
            vt = vt_ref[0, j]
            cj = block_bias(j * tk)
            kan = None if last else keys((j + 1) * tk, tk)
            for g in range(NG_DIFF):
                for c in (2 * g, 2 * g + 1):
                    if not last:
                        scores(1 - cur, c, kan[c % 2], tk)
                    consume(cur, c, vt, cj, tk, False)
                if last:
                    finish(g)

        def body(it, carry):
            step(2 * it, 0, False)
            step(2 * it + 1, 1, False)
            return carry

        lax.fori_loop(0, nblk // 2 - 1, body, 0)
        step(nblk - 2, 0, False)
        step(nblk - 1, 1, True)

    tile(jnp.int32(0), True)

    def tile_body(t, carry):
        tile(t, False)
        return carry

    lax.fori_loop(1, ntile, tile_body, 0)


def _diff(slopes, lq1, lk1, lq2, lk2, subg, bqt, bk, bvt, sg, lambda_init):
    b, s, _ = bk.shape
    tg, tk, tq = TG_DIFF, TK_DIFF, TQ_DIFF
    nchain = 2 * NG_DIFF
    smem = pl.BlockSpec(memory_space=pltpu.SMEM)
    small = lambda shape: pl.BlockSpec(shape, lambda bi, h: (0, 0))
    return pl.pallas_call(
        functools.partial(_diff_kernel, lambda_init=lambda_init),
        grid=(b, DIFF_HEADS),
        in_specs=[smem,
                  small((1, HEAD_DIM)), small((1, HEAD_DIM)),
                  small((1, HEAD_DIM)), small((1, HEAD_DIM)),
                  small((LANES, 1)),
                  pl.BlockSpec((1, s // tq, LANES, tq), lambda bi, h: (bi, 0, h, 0)),
                  pl.BlockSpec((1, s, LANES), lambda bi, h: (bi, 0, h)),
                  pl.BlockSpec((1, s // tk, VROWS, tk), lambda bi, h: (bi, 0, h, 0)),
                  pl.BlockSpec((1, s, LANES), lambda bi, h: (bi, 0, A_Q // LANES + h))],
        out_specs=pl.BlockSpec((1, s, LANES), lambda bi, h: (bi, 0, h)),
        out_shape=jax.ShapeDtypeStruct((b, s, B_W), BF16),
        scratch_shapes=[pltpu.VMEM((nchain, LANES, tg), BF16),
                        pltpu.VMEM((2, nchain, tk, tg), F32),
                        pltpu.VMEM((nchain, 1, tg), F32),
                        pltpu.VMEM((nchain, VROWS, tg), F32)],
        compiler_params=pltpu.CompilerParams(
            dimension_semantics=("arbitrary", "arbitrary"),
            vmem_limit_bytes=VMEM_LIMIT),
        name="diffattn",
    )(slopes, lq1, lk1, lq2, lk2, subg, bqt, bk, bvt, sg)


def kernel(x, norm_g, w_in, sinks, lambda_q1, lambda_k1, lambda_q2, lambda_k2,
           subln_g, w_out, final_g):
    b, s, d = x.shape
    depth = norm_g.shape[0]
    diff_slopes = jnp.asarray(
        [2.0 ** (-8.0 * (h + 1) / DIFF_HEADS) for h in range(DIFF_HEADS)], F32)
    h3 = x
    for layer in range(depth):
        aqt, ak, avt, bqt, bk, bvt, sg = _inproj(
            h3, norm_g[layer].reshape(1, d), w_in[layer].astype(BF16))
        mixed_b = _diff(diff_slopes,
                        lambda_q1[layer].reshape(1, HEAD_DIM), lambda_k1[layer].reshape(1, HEAD_DIM),
                        lambda_q2[layer].reshape(1, HEAD_DIM), lambda_k2[layer].reshape(1, HEAD_DIM),
                        subln_g[layer].reshape(LANES, 1),
                        bqt, bk, bvt, sg, _lambda_init(layer))
        h3 = _swa_out(sinks[layer], aqt, ak, avt, sg, h3, mixed_b,
                      w_out[layer].astype(BF16), final_g.reshape(1, d),
                      final_norm=(layer == depth - 1))
    return h3
```

```python
import functools
import math

import jax
import jax.numpy as jnp
import numpy as np
from jax import lax
from jax.experimental import pallas as pl
from jax.experimental.pallas import tpu as pltpu

F32 = jnp.float32
BF16 = jnp.bfloat16

HEAD_DIM = 64
LANES = 128
SWA_Q_HEADS = 8
SWA_KV_HEADS = 2
WINDOW = 128
DIFF_HEADS = 4
RMS_EPS = 1e-6
SUBLN_EPS = 1e-5
NEG_BIG = -1e30
LOG2E = math.log2(math.e)

A_Q = SWA_Q_HEADS * HEAD_DIM
A_KV = SWA_KV_HEADS * HEAD_DIM
B_W = DIFF_HEADS * 2 * HEAD_DIM
MIX = A_Q + B_W
OFF_AK = A_Q
OFF_AV = OFF_AK + A_KV
OFF_BQ = OFF_AV + A_KV
OFF_BK = OFF_BQ + B_W
OFF_BV = OFF_BK + B_W
OFF_GATE = OFF_BV + B_W
IN_COLS = OFF_GATE + MIX

TM_PROJ = 1024
ROWS_OUT = 256
COLS_OUT = 256
TQ_SWA = 1024
TG_DIFF = 256
NG_DIFF = 4
TQ_DIFF = NG_DIFF * TG_DIFF
TK_DIFF = 512
VROWS = LANES + 16
VROWS_A = HEAD_DIM + 16
VMEM_LIMIT = 56 * 1024 * 1024


def _lambda_init(layer_idx):
    return 0.8 - 0.6 * math.exp(-0.3 * layer_idx)


def _nt_dot(a, b):
    return lax.dot_general(a, b, (((1,), (1,)), ((), ())), preferred_element_type=F32)


def _inproj_kernel(x_ref, g_ref, w_ref,
                   aqt_ref, ak_ref, avt_ref, bqt_ref, bk_ref, bvt_ref, sg_ref):
    x = x_ref[0]
    ms = jnp.mean(x * x, axis=-1, keepdims=True)
    h = (x * lax.rsqrt(ms + RMS_EPS) * g_ref[...]).astype(BF16)
    qscale = HEAD_DIM ** -0.5 * LOG2E

    def tok(lo, hi):
        return jnp.dot(h, w_ref[:, lo:hi], preferred_element_type=F32)

    def feat(lo, hi):
        return lax.dot_general(w_ref[:, lo:hi], h, (((0,), (1,)), ((), ())),
                               preferred_element_type=F32)

    ak_ref[0] = tok(OFF_AK, OFF_AV).astype(BF16)
    bk_ref[0] = tok(OFF_BK, OFF_BV).astype(BF16)
    gate = tok(OFF_GATE, IN_COLS)
    sg_ref[0] = (gate * (1.0 / (1.0 + jnp.exp(-gate)))).astype(BF16)

    aqt_ref[0] = (feat(0, OFF_AK) * qscale).astype(BF16)
    avt = feat(OFF_AV, OFF_BQ).astype(BF16)
    for g in range(SWA_KV_HEADS):
        avt_ref[0, g * VROWS_A:g * VROWS_A + HEAD_DIM, :] = avt[g * HEAD_DIM:(g + 1) * HEAD_DIM]
        avt_ref[0, g * VROWS_A + HEAD_DIM:(g + 1) * VROWS_A, :] = jnp.ones(
            (VROWS_A - HEAD_DIM, TM_PROJ), BF16)
    bqt = (feat(OFF_BQ, OFF_BK) * qscale).astype(BF16)
    for c in range(TM_PROJ // TQ_DIFF):
        bqt_ref[0, c] = bqt[:, c * TQ_DIFF:(c + 1) * TQ_DIFF]
    bvt = feat(OFF_BV, OFF_GATE).astype(BF16)
    ones = jnp.ones((VROWS - LANES, TK_DIFF), BF16)
    for c in range(TM_PROJ // TK_DIFF):
        for hd in range(DIFF_HEADS):
            bvt_ref[0, c, hd * VROWS:hd * VROWS + LANES, :] = (
                bvt[hd * LANES:(hd + 1) * LANES, c * TK_DIFF:(c + 1) * TK_DIFF])
            bvt_ref[0, c, hd * VROWS + LANES:(hd + 1) * VROWS, :] = ones


def _inproj(x, g, w):
    b, s, d = x.shape
    tm = TM_PROJ
    row = lambda bi, i: (bi, i, 0)
    col = lambda bi, i: (bi, 0, i)
    whole = lambda bi, i: (0, 0)
    bsd = lambda wd: jax.ShapeDtypeStruct((b, s, wd), BF16)
    bds = lambda wd: jax.ShapeDtypeStruct((b, wd, s), BF16)
    return pl.pallas_call(
        _inproj_kernel,
        grid=(b, s // tm),
        in_specs=[pl.BlockSpec((1, tm, d), row),
                  pl.BlockSpec((1, d), whole),
                  pl.BlockSpec(w.shape, whole)],
        out_specs=[pl.BlockSpec((1, A_Q, tm), col),
                   pl.BlockSpec((1, tm, A_KV), row),
                   pl.BlockSpec((1, SWA_KV_HEADS * VROWS_A, tm), col),
                   pl.BlockSpec((1, tm // TQ_DIFF, B_W, TQ_DIFF), lambda bi, i: (bi, i, 0, 0)),
                   pl.BlockSpec((1, tm, B_W), row),
                   pl.BlockSpec((1, tm // TK_DIFF, DIFF_HEADS * VROWS, TK_DIFF), lambda bi, i: (bi, i, 0, 0)),
                   pl.BlockSpec((1, tm, MIX), row)],
        out_shape=[bds(A_Q), bsd(A_KV), bds(SWA_KV_HEADS * VROWS_A),
                   jax.ShapeDtypeStruct((b, s // TQ_DIFF, B_W, TQ_DIFF), BF16), bsd(B_W),
                   jax.ShapeDtypeStruct((b, s // TK_DIFF, DIFF_HEADS * VROWS, TK_DIFF), BF16),
                   bsd(MIX)],
        compiler_params=pltpu.CompilerParams(
            dimension_semantics=("arbitrary", "arbitrary"), vmem_limit_bytes=VMEM_LIMIT),
        name="inproj",
    )(x, g, w)


def _swa_out_kernel(sinks_ref, qt_ref, kp_ref, kc_ref, vtp_ref, vtc_ref, sg_ref,
                    x_ref, mb_ref, wout_ref, fg_ref, o_ref, kcat, vtcat, ma_scr, *, final_norm):
    w_ = WINDOW
    i = pl.program_id(1)
    group = SWA_Q_HEADS // SWA_KV_HEADS
    nq = group * w_
    kcat[0:w_, :] = kp_ref[0]
    kcat[w_:, :] = kc_ref[0]
    vtcat[:, 0:w_] = vtp_ref[0]
    vtcat[:, w_:] = vtc_ref[0]

    kidx = lax.broadcasted_iota(jnp.int32, (2 * w_, nq), 0)
    qidx = lax.broadcasted_iota(jnp.int32, (2 * w_, nq), 1) & (w_ - 1)
    in_cur = kidx >= w_
    band = (in_cur & (kidx - w_ <= qidx)) | ((kidx < w_) & (kidx > qidx))
    first_band = band & ((i > 0) | in_cur)

    lane = lax.broadcasted_iota(jnp.int32, (2 * w_, LANES), 1)
    kpos = lax.broadcasted_iota(jnp.int32, (2 * w_, LANES), 0).astype(F32)
    zk = jnp.zeros((2 * w_, LANES), F32)
    kmask = (lane < HEAD_DIM, lane >= HEAD_DIM)
    kpos_lanes = (jnp.where((lane == HEAD_DIM) | (lane == HEAD_DIM + 1), kpos, zk).astype(BF16),
                  jnp.where(lane < 2, kpos, zk).astype(BF16))

    hrow = lax.broadcasted_iota(jnp.int32, (HEAD_DIM, nq), 0)
    hcol = lax.broadcasted_iota(jnp.int32, (HEAD_DIM, nq), 1)

    def slope_rows(g):
        out = jnp.zeros((HEAD_DIM, nq), F32)
        for u in range(group):
            sl = 2.0 ** (-8.0 * (g * group + u + 1) / SWA_Q_HEADS) * LOG2E
            hi = float(np.asarray(sl, dtype=BF16).astype(np.float32))
            lo = sl - hi
            in_head = (hcol >= u * w_) & (hcol < (u + 1) * w_)
            out = jnp.where(in_head & (hrow == 0), hi, jnp.where(in_head & (hrow == 1), lo, out))
        return out.astype(BF16)

    srows = [slope_rows(g) for g in range(SWA_KV_HEADS)]

    qpos = (lax.broadcasted_iota(jnp.int32, (1, w_), 1) + w_).astype(F32)

    def sink_row(g):
        return jnp.concatenate(
            [(sinks_ref[g * group + u] + 2.0 ** (-8.0 * (g * group + u + 1) / SWA_Q_HEADS) * qpos) * LOG2E
             for u in range(group)], axis=1)

    sinkv = [sink_row(g) for g in range(SWA_KV_HEADS)]

    def scores(w, g):
        keys = kcat[w * w_:(w + 2) * w_, :]
        ka = jnp.where(kmask[g], keys, kpos_lanes[g])
        qh = jnp.concatenate(
            [qt_ref[0, (g * group + u) * HEAD_DIM:(g * group + u + 1) * HEAD_DIM, w * w_:(w + 1) * w_]
             for u in range(group)], axis=1)
        wq = jnp.concatenate([qh, srows[g]] if g == 0 else [srows[g], qh], axis=0)
        return jnp.dot(ka, wq, preferred_element_type=F32)

    def consume(w, g, s):
        s = jnp.where(first_band if w == 0 else band, s, NEG_BIG)
        m = jnp.maximum(jnp.max(s, axis=0, keepdims=True), sinkv[g])
        p = jnp.exp2(s - m).astype(BF16)
        vt = vtcat[g * VROWS_A:(g + 1) * VROWS_A, w * w_:(w + 2) * w_]
        o = jnp.dot(vt, p, preferred_element_type=F32)
        denom = o[HEAD_DIM:HEAD_DIM + 1] + jnp.exp2(sinkv[g] - m)
        ot = o[0:HEAD_DIM] / denom
        rows = slice(w * w_, (w + 1) * w_)
        for pair in range(group // 2):
            cols = slice((g * (group // 2) + pair) * LANES, (g * (group // 2) + pair + 1) * LANES)
            two = jnp.concatenate([ot[:, (2 * pair) * w_:(2 * pair + 1) * w_],
                                   ot[:, (2 * pair + 1) * w_:(2 * pair + 2) * w_]], axis=0)
            ma_scr[rows, cols] = (two.T * sg_ref[0, rows, cols].astype(F32)).astype(BF16)

    npiece = MIX // COLS_OUT

    def project(k, j):
        rows = slice(k * ROWS_OUT, (k + 1) * ROWS_OUT)
        cols = slice(j * COLS_OUT, (j + 1) * COLS_OUT)
        o_ref[0, rows, cols] = (
            x_ref[0, rows, cols]
            + jnp.dot(ma_scr[rows, :], wout_ref[0:A_Q, cols], preferred_element_type=F32)
            + jnp.dot(mb_ref[0, rows, :], wout_ref[A_Q:MIX, cols], preferred_element_type=F32))
        if final_norm and j == npiece - 1:
            y = o_ref[0, rows, :]
            ms = jnp.mean(y * y, axis=-1, keepdims=True)
            o_ref[0, rows, :] = y * lax.rsqrt(ms + RMS_EPS) * fg_ref[...]

    chains = [(w, g) for w in range(TQ_SWA // w_) for g in range(SWA_KV_HEADS)]
    chains_per_chunk = (ROWS_OUT // w_) * SWA_KV_HEADS
    ahead = 2
    pending = [scores(*ch) for ch in chains[:ahead]]
    ready = []
    for n, (w, g) in enumerate(chains):
        if n + ahead < len(chains):
            pending.append(scores(*chains[n + ahead]))
        consume(w, g, pending.pop(0))
        if (n + 1) % chains_per_chunk == 0:
            ready += [(n // chains_per_chunk, j) for j in range(npiece)]
        if ready:
            project(*ready.pop(0))
    for unit in ready:
        project(*unit)


def _swa_out(sinks, aqt, ak, avt, sg, x, mixed_b, wout, fg, final_norm):
    b, s, d = x.shape
    tq = TQ_SWA
    wpt = tq // WINDOW
    vr = SWA_KV_HEADS * VROWS_A
    prev_w = lambda i: jnp.maximum(i * wpt - 1, 0)
    tile = lambda bi, i: (bi, i, 0)
    whole = lambda bi, i: (0, 0)
    return pl.pallas_call(
        functools.partial(_swa_out_kernel, final_norm=final_norm),
        grid=(b, s // tq),
        in_specs=[pl.BlockSpec(memory_space=pltpu.SMEM),
                  pl.BlockSpec((1, A_Q, tq), lambda bi, i: (bi, 0, i)),
                  pl.BlockSpec((1, WINDOW, A_KV), lambda bi, i: (bi, prev_w(i), 0)),
                  pl.BlockSpec((1, tq, A_KV), tile),
                  pl.BlockSpec((1, vr, WINDOW), lambda bi, i: (bi, 0, prev_w(i))),
                  pl.BlockSpec((1, vr, tq), lambda bi, i: (bi, 0, i)),
                  pl.BlockSpec((1, tq, A_Q), tile),
                  pl.BlockSpec((1, tq, d), tile),
                  pl.BlockSpec((1, tq, B_W), tile),
                  pl.BlockSpec((MIX, d), whole),
                  pl.BlockSpec((1, d), whole)],
        out_specs=pl.BlockSpec((1, tq, d), tile),
        out_shape=jax.ShapeDtypeStruct((b, s, d), F32),
        scratch_shapes=[pltpu.VMEM((tq + WINDOW, A_KV), BF16),
                        pltpu.VMEM((vr, tq + WINDOW), BF16),
                        pltpu.VMEM((tq, A_Q), BF16)],
        compiler_params=pltpu.CompilerParams(
            dimension_semantics=("arbitrary", "arbitrary"), vmem_limit_bytes=VMEM_LIMIT),
        name="swa_out",
    )(sinks, aqt, ak, ak, avt, avt, sg, x, mixed_b, wout, fg)


def _diff_kernel(slopes_ref, lq1_ref, lk1_ref, lq2_ref, lk2_ref, subg_ref,
                 qt_ref, k_ref, vt_ref, sg_ref, o_ref,
                 qa_ref, s_ref, m_ref, acc_ref, *, lambda_init):
    tg, tk, tq = TG_DIFF, TK_DIFF, TQ_DIFF
    nchain = 2 * NG_DIFF
    ntile = qt_ref.shape[1]
    hd = pl.program_id(1)
    slope = slopes_ref[hd] * LOG2E

    def key_consts(n):
        kk = lax.broadcasted_iota(jnp.int32, (n, LANES), 0).astype(F32)
        lane = lax.broadcasted_iota(jnp.int32, (n, LANES), 1)
        ab = slope * kk
        ab_hi = ab.astype(BF16).astype(F32)
        ab_lo = ab - ab_hi
        zk = jnp.zeros((n, LANES), F32)
        bias1 = jnp.where(lane == HEAD_DIM, ab_hi,
                          jnp.where(lane == HEAD_DIM + 1, ab_lo, zk)).astype(BF16)
        bias2 = jnp.where(lane == 0, ab_hi, jnp.where(lane == 1, ab_lo, zk)).astype(BF16)
        return lane < HEAD_DIM, bias1, bias2

    lam = (jnp.exp(jnp.sum(lq1_ref[...] * lk1_ref[...], axis=-1, keepdims=True))
           - jnp.exp(jnp.sum(lq2_ref[...] * lk2_ref[...], axis=-1, keepdims=True))
           + lambda_init)

    def build_queries(t):
        rowq = lax.broadcasted_iota(jnp.int32, (LANES, tg), 0)
        one = jnp.ones((LANES, tg), F32)
        zq = jnp.zeros((LANES, tg), F32)
        for g in range(NG_DIFF):
            qt = qt_ref[0, t, :, g * tg:(g + 1) * tg].astype(F32)
            qa_ref[2 * g] = jnp.where(rowq < HEAD_DIM, qt,
                                      jnp.where(rowq < HEAD_DIM + 2, one, zq)).astype(BF16)
            qa_ref[2 * g + 1] = jnp.where(rowq >= HEAD_DIM, qt,
                                          jnp.where(rowq < 2, one, zq)).astype(BF16)

    def reset(c):
        m_ref[c] = jnp.full(m_ref.shape[1:], NEG_BIG, F32)
        acc_ref[c] = jnp.zeros(acc_ref.shape[1:], F32)

    def scores(buf, c, ka, n):
        s_ref[buf, c, 0:n, :] = jnp.dot(ka, qa_ref[c], preferred_element_type=F32)

    def tile(t, first_tile):
        kconst = {n: key_consts(n) for n in ((tg,) if first_tile else (tg, tk))}
        krow = lax.broadcasted_iota(jnp.int32, (tg, tg), 0)
        qcol = lax.broadcasted_iota(jnp.int32, (tg, tg), 1)
        causal = krow <= qcol

        def consume(buf, c, vt, cj, n, masked):
            s = s_ref[buf, c, 0:n, :]
            if masked:
                s = jnp.where(causal, s, NEG_BIG)
            m_old = m_ref[c]
            m_new = jnp.maximum(m_old, jnp.max(s, axis=0, keepdims=True) + cj)
            alpha = jnp.exp2(m_old - m_new)
            p = jnp.exp2(s - (m_new - cj)).astype(BF16)
            acc_ref[c] = alpha * acc_ref[c] + jnp.dot(vt, p, preferred_element_type=F32)
            m_ref[c] = m_new

        def keys(start, n):
            k = k_ref[0, pl.ds(pl.multiple_of(start, n), n), :]
            lane_lo, bias1, bias2 = kconst[n]
            return jnp.where(lane_lo, k, bias1), jnp.where(lane_lo, bias2, k)

        def block_bias(start):
            return slope * jnp.asarray(start, jnp.int32).astype(F32)

        def finish(g):
            a1 = acc_ref[2 * g]
            a2 = acc_ref[2 * g + 1]
            ot = (a1[0:LANES] / a1[LANES:LANES + 1]
                  - lam * (a2[0:LANES] / a2[LANES:LANES + 1]))
            ot = ot * lax.rsqrt(jnp.mean(ot * ot, axis=0, keepdims=True) + SUBLN_EPS) * subg_ref[...]
            ot = ot * (1.0 - lambda_init)
            rows = pl.ds(pl.multiple_of(t * tq + g * tg, tg), tg)
            o_ref[0, rows, :] = (ot.T * sg_ref[0, rows, :].astype(F32)).astype(BF16)

        base = t * tq
        build_queries(t)
        ka = keys(base, tg)
        for c in range(nchain):
            reset(c)
            scores(0, c, ka[c % 2], tg)
        ka_first = None if first_tile else keys(jnp.int32(0), tk)
        for jj in range(NG_DIFF):
            cur, nxt = jj % 2, 1 - jj % 2
            start = base + jj * tg
            vt = vt_ref[0, t * (tq // tk) + (jj * tg) // tk, :,
                        (jj * tg) % tk:(jj * tg) % tk + tg]
            cj = block_bias(start)
            ka = keys(start + tg, tg) if jj + 1 < NG_DIFF else None
            for g in range(jj, NG_DIFF):
                for c in (2 * g, 2 * g + 1):
                    if g > jj:
                        scores(nxt, c, ka[c % 2], tg)
                    consume(cur, c, vt, cj, tg, g == jj)
                    if g == jj and not first_tile:
                        scores(0, c, ka_first[c % 2], tk)
                if g == jj and first_tile:
                    finish(g)
        if first_tile:
            return

        def step(j, last):
            cur = j % 2
            vt = vt_ref[0, j]
            cj = block_bias(j * tk)
            kan = None if last else keys(jnp.int32((j + 1) * tk), tk)
            for g in range(NG_DIFF):
                for c in (2 * g, 2 * g + 1):
                    if not last:
                        scores(1 - cur, c, kan[c % 2], tk)
                    consume(cur, c, vt, cj, tk, False)
                if last:
                    finish(g)

        def earlier_keys(nblk):
            for j in range(nblk):
                step(j, j == nblk - 1)

        for n in range(1, ntile):
            pl.when(t == n)(functools.partial(earlier_keys, n * (tq // tk)))

    tile(jnp.int32(0), True)

    def tile_body(t, carry):
        tile(t, False)
        return carry

    lax.fori_loop(1, ntile, tile_body, 0)


def _diff(slopes, lq1, lk1, lq2, lk2, subg, bqt, bk, bvt, sg, lambda_init):
    b, s, _ = bk.shape
    tg, tk, tq = TG_DIFF, TK_DIFF, TQ_DIFF
    nchain = 2 * NG_DIFF
    smem = pl.BlockSpec(memory_space=pltpu.SMEM)
    small = lambda shape: pl.BlockSpec(shape, lambda bi, h: (0, 0))
    return pl.pallas_call(
        functools.partial(_diff_kernel, lambda_init=lambda_init),
        grid=(b, DIFF_HEADS),
        in_specs=[smem,
                  small((1, HEAD_DIM)), small((1, HEAD_DIM)),
                  small((1, HEAD_DIM)), small((1, HEAD_DIM)),
                  small((LANES, 1)),
                  pl.BlockSpec((1, s // tq, LANES, tq), lambda bi, h: (bi, 0, h, 0)),
                  pl.BlockSpec((1, s, LANES), lambda bi, h: (bi, 0, h)),
                  pl.BlockSpec((1, s // tk, VROWS, tk), lambda bi, h: (bi, 0, h, 0)),
                  pl.BlockSpec((1, s, LANES), lambda bi, h: (bi, 0, A_Q // LANES + h))],
        out_specs=pl.BlockSpec((1, s, LANES), lambda bi, h: (bi, 0, h)),
        out_shape=jax.ShapeDtypeStruct((b, s, B_W), BF16),
        scratch_shapes=[pltpu.VMEM((nchain, LANES, tg), BF16),
                        pltpu.VMEM((2, nchain, tk, tg), F32),
                        pltpu.VMEM((nchain, 1, tg), F32),
                        pltpu.VMEM((nchain, VROWS, tg), F32)],
        compiler_params=pltpu.CompilerParams(
            dimension_semantics=("arbitrary", "arbitrary"),
            vmem_limit_bytes=VMEM_LIMIT),
        name="diffattn",
    )(slopes, lq1, lk1, lq2, lk2, subg, bqt, bk, bvt, sg)


def kernel(x, norm_g, w_in, sinks, lambda_q1, lambda_k1, lambda_q2, lambda_k2,
           subln_g, w_out, final_g):
    b, s, d = x.shape
    depth = norm_g.shape[0]
    diff_slopes = jnp.asarray(
        [2.0 ** (-8.0 * (h + 1) / DIFF_HEADS) for h in range(DIFF_HEADS)], F32)
    h3 = x
    for layer in range(depth):
        aqt, ak, avt, bqt, bk, bvt, sg = _inproj(
            h3, norm_g[layer].reshape(1, d), w_in[layer].astype(BF16))
        mixed_b = _diff(diff_slopes,
                        lambda_q1[layer].reshape(1, HEAD_DIM), lambda_k1[layer].reshape(1, HEAD_DIM),
                        lambda_q2[layer].reshape(1, HEAD_DIM), lambda_k2[layer].reshape(1, HEAD_DIM),
                        subln_g[layer].reshape(LANES, 1),
                        bqt, bk, bvt, sg, _lambda_init(layer))
        h3 = _swa_out(sinks[layer], aqt, ak, avt, sg, h3, mixed_b,
                      w_out[layer].astype(BF16), final_g.reshape(1, d),
                      final_norm=(layer == depth - 1))
    return h3
```

```python
import functools
import math

import jax
import jax.numpy as jnp
import numpy as np
from jax import lax
from jax.experimental import pallas as pl
from jax.experimental.pallas import tpu as pltpu

F32 = jnp.float32
BF16 = jnp.bfloat16

HEAD_DIM = 64
LANES = 128
SWA_Q_HEADS = 8
SWA_KV_HEADS = 2
WINDOW = 128
DIFF_HEADS = 4
RMS_EPS = 1e-6
SUBLN_EPS = 1e-5
NEG_BIG = -1e30
LOG2E = math.log2(math.e)

A_Q = SWA_Q_HEADS * HEAD_DIM
A_KV = SWA_KV_HEADS * HEAD_DIM
B_W = DIFF_HEADS * 2 * HEAD_DIM
MIX = A_Q + B_W
OFF_AK = A_Q
OFF_AV = OFF_AK + A_KV
OFF_BQ = OFF_AV + A_KV
OFF_BK = OFF_BQ + B_W
OFF_BV = OFF_BK + B_W
OFF_GATE = OFF_BV + B_W
IN_COLS = OFF_GATE + MIX

TM_PROJ = 1024
ROWS_OUT = 256
COLS_OUT = 256
TQ_SWA = 1024
TG_DIFF = 256
NG_DIFF = 4
TQ_DIFF = NG_DIFF * TG_DIFF
TK_DIFF = 512
VROWS = LANES + 16
VROWS_A = HEAD_DIM + 16
VMEM_LIMIT = 56 * 1024 * 1024


def _lambda_init(layer_idx):
    return 0.8 - 0.6 * math.exp(-0.3 * layer_idx)


def _nt_dot(a, b):
    return lax.dot_general(a, b, (((1,), (1,)), ((), ())), preferred_element_type=F32)


def _inproj_kernel(x_ref, g_ref, w_ref,
                   aqt_ref, ak_ref, avt_ref, bqt_ref, bk_ref, bvt_ref, sg_ref):
    x = x_ref[0]
    ms = jnp.mean(x * x, axis=-1, keepdims=True)
    h = (x * lax.rsqrt(ms + RMS_EPS) * g_ref[...]).astype(BF16)
    qscale = HEAD_DIM ** -0.5 * LOG2E

    def tok(lo, hi):
        return jnp.dot(h, w_ref[:, lo:hi], preferred_element_type=F32)

    def feat(lo, hi):
        return lax.dot_general(w_ref[:, lo:hi], h, (((0,), (1,)), ((), ())),
                               preferred_element_type=F32)

    ak_ref[0] = tok(OFF_AK, OFF_AV).astype(BF16)
    bk_ref[0] = tok(OFF_BK, OFF_BV).astype(BF16)
    gate = tok(OFF_GATE, IN_COLS)
    sg_ref[0] = (gate * (1.0 / (1.0 + jnp.exp(-gate)))).astype(BF16)

    aqt_ref[0] = (feat(0, OFF_AK) * qscale).astype(BF16)
    avt = feat(OFF_AV, OFF_BQ).astype(BF16)
    for g in range(SWA_KV_HEADS):
        avt_ref[0, g * VROWS_A:g * VROWS_A + HEAD_DIM, :] = avt[g * HEAD_DIM:(g + 1) * HEAD_DIM]
        avt_ref[0, g * VROWS_A + HEAD_DIM:(g + 1) * VROWS_A, :] = jnp.ones(
            (VROWS_A - HEAD_DIM, TM_PROJ), BF16)
    bqt = (feat(OFF_BQ, OFF_BK) * qscale).astype(BF16)
    for c in range(TM_PROJ // TQ_DIFF):
        bqt_ref[0, c] = bqt[:, c * TQ_DIFF:(c + 1) * TQ_DIFF]
    bvt = feat(OFF_BV, OFF_GATE).astype(BF16)
    ones = jnp.ones((VROWS - LANES, TK_DIFF), BF16)
    for c in range(TM_PROJ // TK_DIFF):
        for hd in range(DIFF_HEADS):
            bvt_ref[0, c, hd * VROWS:hd * VROWS + LANES, :] = (
                bvt[hd * LANES:(hd + 1) * LANES, c * TK_DIFF:(c + 1) * TK_DIFF])
            bvt_ref[0, c, hd * VROWS + LANES:(hd + 1) * VROWS, :] = ones


def _inproj(x, g, w):
    b, s, d = x.shape
    tm = TM_PROJ
    row = lambda bi, i: (bi, i, 0)
    col = lambda bi, i: (bi, 0, i)
    whole = lambda bi, i: (0, 0)
    bsd = lambda wd: jax.ShapeDtypeStruct((b, s, wd), BF16)
    bds = lambda wd: jax.ShapeDtypeStruct((b, wd, s), BF16)
    return pl.pallas_call(
        _inproj_kernel,
        grid=(b, s // tm),
        in_specs=[pl.BlockSpec((1, tm, d), row),
                  pl.BlockSpec((1, d), whole),
                  pl.BlockSpec(w.shape, whole)],
        out_specs=[pl.BlockSpec((1, A_Q, tm), col),
                   pl.BlockSpec((1, tm, A_KV), row),
                   pl.BlockSpec((1, SWA_KV_HEADS * VROWS_A, tm), col),
                   pl.BlockSpec((1, tm // TQ_DIFF, B_W, TQ_DIFF), lambda bi, i: (bi, i, 0, 0)),
                   pl.BlockSpec((1, tm, B_W), row),
                   pl.BlockSpec((1, tm // TK_DIFF, DIFF_HEADS * VROWS, TK_DIFF), lambda bi, i: (bi, i, 0, 0)),
                   pl.BlockSpec((1, tm, MIX), row)],
        out_shape=[bds(A_Q), bsd(A_KV), bds(SWA_KV_HEADS * VROWS_A),
                   jax.ShapeDtypeStruct((b, s // TQ_DIFF, B_W, TQ_DIFF), BF16), bsd(B_W),
                   jax.ShapeDtypeStruct((b, s // TK_DIFF, DIFF_HEADS * VROWS, TK_DIFF), BF16),
                   bsd(MIX)],
        compiler_params=pltpu.CompilerParams(
            dimension_semantics=("arbitrary", "arbitrary"), vmem_limit_bytes=VMEM_LIMIT),
        name="inproj",
    )(x, g, w)


def _swa_out_kernel(sinks_ref, qt_ref, kp_ref, kc_ref, vtp_ref, vtc_ref, sg_ref,
                    x_ref, mb_ref, wout_ref, fg_ref, o_ref, kcat, vtcat, ma_scr, *, final_norm):
    w_ = WINDOW
    i = pl.program_id(1)
    group = SWA_Q_HEADS // SWA_KV_HEADS
    nq = group * w_
    kcat[0:w_, :] = kp_ref[0]
    kcat[w_:, :] = kc_ref[0]
    vtcat[:, 0:w_] = vtp_ref[0]
    vtcat[:, w_:] = vtc_ref[0]

    kidx = lax.broadcasted_iota(jnp.int32, (2 * w_, nq), 0)
    qidx = lax.broadcasted_iota(jnp.int32, (2 * w_, nq), 1) & (w_ - 1)
    in_cur = kidx >= w_
    band = (in_cur & (kidx - w_ <= qidx)) | ((kidx < w_) & (kidx > qidx))
    first_band = band & ((i > 0) | in_cur)

    lane = lax.broadcasted_iota(jnp.int32, (2 * w_, LANES), 1)
    kpos = lax.broadcasted_iota(jnp.int32, (2 * w_, LANES), 0).astype(F32)
    zk = jnp.zeros((2 * w_, LANES), F32)
    kmask = (lane < HEAD_DIM, lane >= HEAD_DIM)
    kpos_lanes = (jnp.where((lane == HEAD_DIM) | (lane == HEAD_DIM + 1), kpos, zk).astype(BF16),
                  jnp.where(lane < 2, kpos, zk).astype(BF16))

    hrow = lax.broadcasted_iota(jnp.int32, (HEAD_DIM, nq), 0)
    hcol = lax.broadcasted_iota(jnp.int32, (HEAD_DIM, nq), 1)

    def slope_rows(g):
        out = jnp.zeros((HEAD_DIM, nq), F32)
        for u in range(group):
            sl = 2.0 ** (-8.0 * (g * group + u + 1) / SWA_Q_HEADS) * LOG2E
            hi = float(np.asarray(sl, dtype=BF16).astype(np.float32))
            lo = sl - hi
            in_head = (hcol >= u * w_) & (hcol < (u + 1) * w_)
            out = jnp.where(in_head & (hrow == 0), hi, jnp.where(in_head & (hrow == 1), lo, out))
        return out.astype(BF16)

    srows = [slope_rows(g) for g in range(SWA_KV_HEADS)]

    qpos = (lax.broadcasted_iota(jnp.int32, (1, w_), 1) + w_).astype(F32)

    def sink_row(g):
        return jnp.concatenate(
            [(sinks_ref[g * group + u] + 2.0 ** (-8.0 * (g * group + u + 1) / SWA_Q_HEADS) * qpos) * LOG2E
             for u in range(group)], axis=1)

    sinkv = [sink_row(g) for g in range(SWA_KV_HEADS)]

    def scores(w, g):
        keys = kcat[w * w_:(w + 2) * w_, :]
        ka = jnp.where(kmask[g], keys, kpos_lanes[g])
        qh = jnp.concatenate(
            [qt_ref[0, (g * group + u) * HEAD_DIM:(g * group + u + 1) * HEAD_DIM, w * w_:(w + 1) * w_]
             for u in range(group)], axis=1)
        wq = jnp.concatenate([qh, srows[g]] if g == 0 else [srows[g], qh], axis=0)
        return jnp.dot(ka, wq, preferred_element_type=F32)

    def consume(w, g, s):
        s = jnp.where(first_band if w == 0 else band, s, NEG_BIG)
        m = jnp.maximum(jnp.max(s, axis=0, keepdims=True), sinkv[g])
        p = jnp.exp2(s - m).astype(BF16)
        vt = vtcat[g * VROWS_A:(g + 1) * VROWS_A, w * w_:(w + 2) * w_]
        o = jnp.dot(vt, p, preferred_element_type=F32)
        denom = o[HEAD_DIM:HEAD_DIM + 1] + jnp.exp2(sinkv[g] - m)
        ot = o[0:HEAD_DIM] / denom
        rows = slice(w * w_, (w + 1) * w_)
        for pair in range(group // 2):
            cols = slice((g * (group // 2) + pair) * LANES, (g * (group // 2) + pair + 1) * LANES)
            two = jnp.concatenate([ot[:, (2 * pair) * w_:(2 * pair + 1) * w_],
                                   ot[:, (2 * pair + 1) * w_:(2 * pair + 2) * w_]], axis=0)
            ma_scr[rows, cols] = (two.T * sg_ref[0, rows, cols].astype(F32)).astype(BF16)

    npiece = MIX // COLS_OUT

    def project(k, j):
        rows = slice(k * ROWS_OUT, (k + 1) * ROWS_OUT)
        cols = slice(j * COLS_OUT, (j + 1) * COLS_OUT)
        o_ref[0, rows, cols] = (
            x_ref[0, rows, cols]
            + jnp.dot(ma_scr[rows, :], wout_ref[0:A_Q, cols], preferred_element_type=F32)
            + jnp.dot(mb_ref[0, rows, :], wout_ref[A_Q:MIX, cols], preferred_element_type=F32))
        if final_norm and j == npiece - 1:
            y = o_ref[0, rows, :]
            ms = jnp.mean(y * y, axis=-1, keepdims=True)
            o_ref[0, rows, :] = y * lax.rsqrt(ms + RMS_EPS) * fg_ref[...]

    chains = [(w, g) for w in range(TQ_SWA // w_) for g in range(SWA_KV_HEADS)]
    chains_per_chunk = (ROWS_OUT // w_) * SWA_KV_HEADS
    ahead = 2
    pending = [scores(*ch) for ch in chains[:ahead]]
    ready = []
    for n, (w, g) in enumerate(chains):
        if n + ahead < len(chains):
            pending.append(scores(*chains[n + ahead]))
        consume(w, g, pending.pop(0))
        if (n + 1) % chains_per_chunk == 0:
            ready += [(n // chains_per_chunk, j) for j in range(npiece)]
        if ready:
            project(*ready.pop(0))
    for unit in ready:
        project(*unit)


def _swa_out(sinks, aqt, ak, avt, sg, x, mixed_b, wout, fg, final_norm):
    b, s, d = x.shape
    tq = TQ_SWA
    wpt = tq // WINDOW
    vr = SWA_KV_HEADS * VROWS_A
    prev_w = lambda i: jnp.maximum(i * wpt - 1, 0)
    tile = lambda bi, i: (bi, i, 0)
    whole = lambda bi, i: (0, 0)
    return pl.pallas_call(
        functools.partial(_swa_out_kernel, final_norm=final_norm),
        grid=(b, s // tq),
        in_specs=[pl.BlockSpec(memory_space=pltpu.SMEM),
                  pl.BlockSpec((1, A_Q, tq), lambda bi, i: (bi, 0, i)),
                  pl.BlockSpec((1, WINDOW, A_KV), lambda bi, i: (bi, prev_w(i), 0)),
                  pl.BlockSpec((1, tq, A_KV), tile),
                  pl.BlockSpec((1, vr, WINDOW), lambda bi, i: (bi, 0, prev_w(i))),
                  pl.BlockSpec((1, vr, tq), lambda bi, i: (bi, 0, i)),
                  pl.BlockSpec((1, tq, A_Q), tile),
                  pl.BlockSpec((1, tq, d), tile),
                  pl.BlockSpec((1, tq, B_W), tile),
                  pl.BlockSpec((MIX, d), whole),
                  pl.BlockSpec((1, d), whole)],
        out_specs=pl.BlockSpec((1, tq, d), tile),
        out_shape=jax.ShapeDtypeStruct((b, s, d), F32),
        scratch_shapes=[pltpu.VMEM((tq + WINDOW, A_KV), BF16),
                        pltpu.VMEM((vr, tq + WINDOW), BF16),
                        pltpu.VMEM((tq, A_Q), BF16)],
        compiler_params=pltpu.CompilerParams(
            dimension_semantics=("arbitrary", "arbitrary"), vmem_limit_bytes=VMEM_LIMIT),
        name="swa_out",
    )(sinks, aqt, ak, ak, avt, avt, sg, x, mixed_b, wout, fg)


def _diff_kernel(slopes_ref, lq1_ref, lk1_ref, lq2_ref, lk2_ref, subg_ref,
                 qt_ref, k_ref, vt_ref, sg_ref, o_ref,
                 qa_ref, s_ref, m_ref, acc_ref, *, lambda_init):
    tg, tk, tq = TG_DIFF, TK_DIFF, TQ_DIFF
    nchain = 2 * NG_DIFF
    ntile = qt_ref.shape[1]
    hd = pl.program_id(1)
    slope = slopes_ref[hd] * LOG2E

    def key_consts(n):
        kk = lax.broadcasted_iota(jnp.int32, (n, LANES), 0).astype(F32)
        lane = lax.broadcasted_iota(jnp.int32, (n, LANES), 1)
        ab = slope * kk
        ab_hi = ab.astype(BF16).astype(F32)
        ab_lo = ab - ab_hi
        zk = jnp.zeros((n, LANES), F32)
        bias1 = jnp.where(lane == HEAD_DIM, ab_hi,
                          jnp.where(lane == HEAD_DIM + 1, ab_lo, zk)).astype(BF16)
        bias2 = jnp.where(lane == 0, ab_hi, jnp.where(lane == 1, ab_lo, zk)).astype(BF16)
        return lane < HEAD_DIM, bias1, bias2

    lam = (jnp.exp(jnp.sum(lq1_ref[...] * lk1_ref[...], axis=-1, keepdims=True))
           - jnp.exp(jnp.sum(lq2_ref[...] * lk2_ref[...], axis=-1, keepdims=True))
           + lambda_init)

    def build_queries(t):
        rowq = lax.broadcasted_iota(jnp.int32, (LANES, tg), 0)
        one = jnp.ones((LANES, tg), F32)
        zq = jnp.zeros((LANES, tg), F32)
        for g in range(NG_DIFF):
            qt = qt_ref[0, t, :, g * tg:(g + 1) * tg].astype(F32)
            qa_ref[2 * g] = jnp.where(rowq < HEAD_DIM, qt,
                                      jnp.where(rowq < HEAD_DIM + 2, one, zq)).astype(BF16)
            qa_ref[2 * g + 1] = jnp.where(rowq >= HEAD_DIM, qt,
                                          jnp.where(rowq < 2, one, zq)).astype(BF16)

    def reset(c):
        m_ref[c] = jnp.full(m_ref.shape[1:], NEG_BIG, F32)
        acc_ref[c] = jnp.zeros(acc_ref.shape[1:], F32)

    def scores(buf, c, ka, n):
        s_ref[buf, c, 0:n, :] = jnp.dot(ka, qa_ref[c], preferred_element_type=F32)

    def tile(t):
        first_tile = t == 0
        kconst = {n: key_consts(n) for n in ((tg,) if first_tile else (tg, tk))}
        krow = lax.broadcasted_iota(jnp.int32, (tg, tg), 0)
        qcol = lax.broadcasted_iota(jnp.int32, (tg, tg), 1)
        causal = krow <= qcol

        def consume(buf, c, vt, cj, n, masked):
            s = s_ref[buf, c, 0:n, :]
            if masked:
                s = jnp.where(causal, s, NEG_BIG)
            m_old = m_ref[c]
            m_new = jnp.maximum(m_old, jnp.max(s, axis=0, keepdims=True) + cj)
            alpha = jnp.exp2(m_old - m_new)
            p = jnp.exp2(s - (m_new - cj)).astype(BF16)
            acc_ref[c] = alpha * acc_ref[c] + jnp.dot(vt, p, preferred_element_type=F32)
            m_ref[c] = m_new

        def keys(start, n):
            k = k_ref[0, start:start + n, :]
            lane_lo, bias1, bias2 = kconst[n]
            return jnp.where(lane_lo, k, bias1), jnp.where(lane_lo, bias2, k)

        def block_bias(start):
            return slope * float(start)

        def finish(g):
            a1 = acc_ref[2 * g]
            a2 = acc_ref[2 * g + 1]
            ot = (a1[0:LANES] / a1[LANES:LANES + 1]
                  - lam * (a2[0:LANES] / a2[LANES:LANES + 1]))
            ot = ot * lax.rsqrt(jnp.mean(ot * ot, axis=0, keepdims=True) + SUBLN_EPS) * subg_ref[...]
            ot = ot * (1.0 - lambda_init)
            rows = slice(t * tq + g * tg, t * tq + (g + 1) * tg)
            o_ref[0, rows, :] = (ot.T * sg_ref[0, rows, :].astype(F32)).astype(BF16)

        base = t * tq
        build_queries(t)
        ka = keys(base, tg)
        for c in range(nchain):
            reset(c)
            scores(0, c, ka[c % 2], tg)
        ka_first = None if first_tile else keys(0, tk)
        for jj in range(NG_DIFF):
            cur, nxt = jj % 2, 1 - jj % 2
            start = base + jj * tg
            vt = vt_ref[0, t * (tq // tk) + (jj * tg) // tk, :,
                        (jj * tg) % tk:(jj * tg) % tk + tg]
            cj = block_bias(start)
            ka = keys(start + tg, tg) if jj + 1 < NG_DIFF else None
            for g in range(jj, NG_DIFF):
                for c in (2 * g, 2 * g + 1):
                    if g > jj:
                        scores(nxt, c, ka[c % 2], tg)
                    consume(cur, c, vt, cj, tg, g == jj)
                    if g == jj and not first_tile:
                        scores(0, c, ka_first[c % 2], tk)
                if g == jj and first_tile:
                    finish(g)
        nblk = base // tk
        for j in range(nblk):
            cur = j % 2
            last = j == nblk - 1
            vt = vt_ref[0, j]
            cj = block_bias(j * tk)
            kan = None if last else keys((j + 1) * tk, tk)
            for g in range(NG_DIFF):
                for c in (2 * g, 2 * g + 1):
                    if not last:
                        scores(1 - cur, c, kan[c % 2], tk)
                    consume(cur, c, vt, cj, tk, False)
                if last:
                    finish(g)

    def tile_body(t, carry):
        for n in range(ntile):
            pl.when(t == n)(functools.partial(tile, n))
        return carry

    lax.fori_loop(0, ntile, tile_body, 0)


def _diff(slopes, lq1, lk1, lq2, lk2, subg, bqt, bk, bvt, sg, lambda_init):
    b, s, _ = bk.shape
    tg, tk, tq = TG_DIFF, TK_DIFF, TQ_DIFF
    nchain = 2 * NG_DIFF
    smem = pl.BlockSpec(memory_space=pltpu.SMEM)
    small = lambda shape: pl.BlockSpec(shape, lambda bi, h: (0, 0))
    return pl.pallas_call(
        functools.partial(_diff_kernel, lambda_init=lambda_init),
        grid=(b, DIFF_HEADS),
        in_specs=[smem,
                  small((1, HEAD_DIM)), small((1, HEAD_DIM)),
                  small((1, HEAD_DIM)), small((1, HEAD_DIM)),
                  small((LANES, 1)),
                  pl.BlockSpec((1, s // tq, LANES, tq), lambda bi, h: (bi, 0, h, 0)),
                  pl.BlockSpec((1, s, LANES), lambda bi, h: (bi, 0, h)),
                  pl.BlockSpec((1, s // tk, VROWS, tk), lambda bi, h: (bi, 0, h, 0)),
                  pl.BlockSpec((1, s, LANES), lambda bi, h: (bi, 0, A_Q // LANES + h))],
        out_specs=pl.BlockSpec((1, s, LANES), lambda bi, h: (bi, 0, h)),
        out_shape=jax.ShapeDtypeStruct((b, s, B_W), BF16),
        scratch_shapes=[pltpu.VMEM((nchain, LANES, tg), BF16),
                        pltpu.VMEM((2, nchain, tk, tg), F32),
                        pltpu.VMEM((nchain, 1, tg), F32),
                        pltpu.VMEM((nchain, VROWS, tg), F32)],
        compiler_params=pltpu.CompilerParams(
            dimension_semantics=("arbitrary", "arbitrary"),
            vmem_limit_bytes=VMEM_LIMIT),
        name="diffattn",
    )(slopes, lq1, lk1, lq2, lk2, subg, bqt, bk, bvt, sg)


def kernel(x, norm_g, w_in, sinks, lambda_q1, lambda_k1, lambda_q2, lambda_k2,
           subln_g, w_out, final_g):
    b, s, d = x.shape
    depth = norm_g.shape[0]
    diff_slopes = jnp.asarray(
        [2.0 ** (-8.0 * (h + 1) / DIFF_HEADS) for h in range(DIFF_HEADS)], F32)
    h3 = x
    for layer in range(depth):
        aqt, ak, avt, bqt, bk, bvt, sg = _inproj(
            h3, norm_g[layer].reshape(1, d), w_in[layer].astype(BF16))
        mixed_b = _diff(diff_slopes,
                        lambda_q1[layer].reshape(1, HEAD_DIM), lambda_k1[layer].reshape(1, HEAD_DIM),
                        lambda_q2[layer].reshape(1, HEAD_DIM), lambda_k2[layer].reshape(1, HEAD_DIM),
                        subln_g[layer].reshape(LANES, 1),
                        bqt, bk, bvt, sg, _lambda_init(layer))
        h3 = _swa_out(sinks[layer], aqt, ak, avt, sg, h3, mixed_b,
                      w_out[layer].astype(BF16), final_g.reshape(1, d),
                      final_norm=(layer == depth - 1))
    return h3
```

```python
import functools
import math

import jax
import jax.numpy as jnp
import numpy as np
from jax import lax
from jax.experimental import pallas as pl
from jax.experimental.pallas import tpu as pltpu

F32 = jnp.float32
BF16 = jnp.bfloat16

HEAD_DIM = 64
LANES = 128
SWA_Q_HEADS = 8
SWA_KV_HEADS = 2
WINDOW = 128
DIFF_HEADS = 4
RMS_EPS = 1e-6
SUBLN_EPS = 1e-5
NEG_BIG = -1e30
LOG2E = math.log2(math.e)

A_Q = SWA_Q_HEADS * HEAD_DIM
A_KV = SWA_KV_HEADS * HEAD_DIM
B_W = DIFF_HEADS * 2 * HEAD_DIM
MIX = A_Q + B_W
OFF_AK = A_Q
OFF_AV = OFF_AK + A_KV
OFF_BQ = OFF_AV + A_KV
OFF_BK = OFF_BQ + B_W
OFF_BV = OFF_BK + B_W
OFF_GATE = OFF_BV + B_W
IN_COLS = OFF_GATE + MIX

TM_PROJ = 1024
ROWS_IN = 512
ROWS_OUT = 256
COLS_OUT = 256
TQ_SWA = 1024
TG_DIFF = 256
NG_DIFF = 4
TQ_DIFF = NG_DIFF * TG_DIFF
TK_DIFF = 512
VROWS = LANES + 16
VROWS_A = HEAD_DIM + 16
VMEM_LIMIT = 56 * 1024 * 1024


def _lambda_init(layer_idx):
    return 0.8 - 0.6 * math.exp(-0.3 * layer_idx)


def _nt_dot(a, b):
    return lax.dot_general(a, b, (((1,), (1,)), ((), ())), preferred_element_type=F32)


def _inproj_kernel(x_ref, g_ref, w_ref,
                   aqt_ref, ak_ref, avt_ref, bqt_ref, bk_ref, bvt_ref, sg_ref):
    qscale = HEAD_DIM ** -0.5 * LOG2E
    th = ROWS_IN
    for r in range(0, TM_PROJ, th):
        rows = slice(r, r + th)
        x = x_ref[0, rows, :]
        ms = jnp.mean(x * x, axis=-1, keepdims=True)
        h = (x * lax.rsqrt(ms + RMS_EPS) * g_ref[...]).astype(BF16)

        def tok(lo, hi, h=h):
            return jnp.dot(h, w_ref[:, lo:hi], preferred_element_type=F32)

        def feat(lo, hi, h=h):
            return lax.dot_general(w_ref[:, lo:hi], h, (((0,), (1,)), ((), ())),
                                   preferred_element_type=F32)

        ak_ref[0, rows, :] = tok(OFF_AK, OFF_AV).astype(BF16)
        bk_ref[0, rows, :] = tok(OFF_BK, OFF_BV).astype(BF16)
        gate = tok(OFF_GATE, IN_COLS)
        sg_ref[0, rows, :] = (gate * (1.0 / (1.0 + jnp.exp(-gate)))).astype(BF16)

        aqt_ref[0, :, rows] = (feat(0, OFF_AK) * qscale).astype(BF16)
        avt = feat(OFF_AV, OFF_BQ).astype(BF16)
        for g in range(SWA_KV_HEADS):
            avt_ref[0, g * VROWS_A:g * VROWS_A + HEAD_DIM, rows] = avt[g * HEAD_DIM:(g + 1) * HEAD_DIM]
            avt_ref[0, g * VROWS_A + HEAD_DIM:(g + 1) * VROWS_A, rows] = jnp.ones(
                (VROWS_A - HEAD_DIM, th), BF16)
        bqt_ref[0, r // TQ_DIFF, :, r % TQ_DIFF:r % TQ_DIFF + th] = (
            feat(OFF_BQ, OFF_BK) * qscale).astype(BF16)
        bvt = feat(OFF_BV, OFF_GATE).astype(BF16)
        vcols = slice(r % TK_DIFF, r % TK_DIFF + th)
        for hd in range(DIFF_HEADS):
            bvt_ref[0, r // TK_DIFF, hd * VROWS:hd * VROWS + LANES, vcols] = bvt[hd * LANES:(hd + 1) * LANES]
            bvt_ref[0, r // TK_DIFF, hd * VROWS + LANES:(hd + 1) * VROWS, vcols] = jnp.ones(
                (VROWS - LANES, th), BF16)


def _inproj(x, g, w):
    b, s, d = x.shape
    tm = TM_PROJ
    row = lambda bi, i: (bi, i, 0)
    col = lambda bi, i: (bi, 0, i)
    whole = lambda bi, i: (0, 0)
    bsd = lambda wd: jax.ShapeDtypeStruct((b, s, wd), BF16)
    bds = lambda wd: jax.ShapeDtypeStruct((b, wd, s), BF16)
    return pl.pallas_call(
        _inproj_kernel,
        grid=(b, s // tm),
        in_specs=[pl.BlockSpec((1, tm, d), row),
                  pl.BlockSpec((1, d), whole),
                  pl.BlockSpec(w.shape, whole)],
        out_specs=[pl.BlockSpec((1, A_Q, tm), col),
                   pl.BlockSpec((1, tm, A_KV), row),
                   pl.BlockSpec((1, SWA_KV_HEADS * VROWS_A, tm), col),
                   pl.BlockSpec((1, tm // TQ_DIFF, B_W, TQ_DIFF), lambda bi, i: (bi, i, 0, 0)),
                   pl.BlockSpec((1, tm, B_W), row),
                   pl.BlockSpec((1, tm // TK_DIFF, DIFF_HEADS * VROWS, TK_DIFF), lambda bi, i: (bi, i, 0, 0)),
                   pl.BlockSpec((1, tm, MIX), row)],
        out_shape=[bds(A_Q), bsd(A_KV), bds(SWA_KV_HEADS * VROWS_A),
                   jax.ShapeDtypeStruct((b, s // TQ_DIFF, B_W, TQ_DIFF), BF16), bsd(B_W),
                   jax.ShapeDtypeStruct((b, s // TK_DIFF, DIFF_HEADS * VROWS, TK_DIFF), BF16),
                   bsd(MIX)],
        compiler_params=pltpu.CompilerParams(
            dimension_semantics=("arbitrary", "arbitrary"), vmem_limit_bytes=VMEM_LIMIT),
        name="inproj",
    )(x, g, w)


def _swa_out_kernel(sinks_ref, qt_ref, kp_ref, kc_ref, vtp_ref, vtc_ref, sg_ref,
                    x_ref, mb_ref, wout_ref, fg_ref, o_ref, kcat, vtcat, ma_scr, *, final_norm):
    w_ = WINDOW
    i = pl.program_id(1)
    group = SWA_Q_HEADS // SWA_KV_HEADS
    nq = group * w_
    kcat[0:w_, :] = kp_ref[0]
    kcat[w_:, :] = kc_ref[0]
    vtcat[:, 0:w_] = vtp_ref[0]
    vtcat[:, w_:] = vtc_ref[0]

    kidx = lax.broadcasted_iota(jnp.int32, (2 * w_, nq), 0)
    qidx = lax.broadcasted_iota(jnp.int32, (2 * w_, nq), 1) & (w_ - 1)
    in_cur = kidx >= w_
    band = (in_cur & (kidx - w_ <= qidx)) | ((kidx < w_) & (kidx > qidx))
    first_band = band & ((i > 0) | in_cur)

    lane = lax.broadcasted_iota(jnp.int32, (2 * w_, LANES), 1)
    kpos = lax.broadcasted_iota(jnp.int32, (2 * w_, LANES), 0).astype(F32)
    zk = jnp.zeros((2 * w_, LANES), F32)
    kmask = (lane < HEAD_DIM, lane >= HEAD_DIM)
    kpos_lanes = (jnp.where((lane == HEAD_DIM) | (lane == HEAD_DIM + 1), kpos, zk).astype(BF16),
                  jnp.where(lane < 2, kpos, zk).astype(BF16))

    hrow = lax.broadcasted_iota(jnp.int32, (HEAD_DIM, nq), 0)
    hcol = lax.broadcasted_iota(jnp.int32, (HEAD_DIM, nq), 1)

    def slope_rows(g):
        out = jnp.zeros((HEAD_DIM, nq), F32)
        for u in range(group):
            sl = 2.0 ** (-8.0 * (g * group + u + 1) / SWA_Q_HEADS) * LOG2E
            hi = float(np.asarray(sl, dtype=BF16).astype(np.float32))
            lo = sl - hi
            in_head = (hcol >= u * w_) & (hcol < (u + 1) * w_)
            out = jnp.where(in_head & (hrow == 0), hi, jnp.where(in_head & (hrow == 1), lo, out))
        return out.astype(BF16)

    srows = [slope_rows(g) for g in range(SWA_KV_HEADS)]

    qpos = (lax.broadcasted_iota(jnp.int32, (1, w_), 1) + w_).astype(F32)

    def sink_row(g):
        return jnp.concatenate(
            [(sinks_ref[g * group + u] + 2.0 ** (-8.0 * (g * group + u + 1) / SWA_Q_HEADS) * qpos) * LOG2E
             for u in range(group)], axis=1)

    sinkv = [sink_row(g) for g in range(SWA_KV_HEADS)]

    def scores(w, g):
        keys = kcat[w * w_:(w + 2) * w_, :]
        ka = jnp.where(kmask[g], keys, kpos_lanes[g])
        qh = jnp.concatenate(
            [qt_ref[0, (g * group + u) * HEAD_DIM:(g * group + u + 1) * HEAD_DIM, w * w_:(w + 1) * w_]
             for u in range(group)], axis=1)
        wq = jnp.concatenate([qh, srows[g]] if g == 0 else [srows[g], qh], axis=0)
        return jnp.dot(ka, wq, preferred_element_type=F32)

    def consume(w, g, s):
        s = jnp.where(first_band if w == 0 else band, s, NEG_BIG)
        m = jnp.maximum(jnp.max(s, axis=0, keepdims=True), sinkv[g])
        p = jnp.exp2(s - m).astype(BF16)
        vt = vtcat[g * VROWS_A:(g + 1) * VROWS_A, w * w_:(w + 2) * w_]
        o = jnp.dot(vt, p, preferred_element_type=F32)
        denom = o[HEAD_DIM:HEAD_DIM + 1] + jnp.exp2(sinkv[g] - m)
        ot = o[0:HEAD_DIM] / denom
        rows = slice(w * w_, (w + 1) * w_)
        for pair in range(group // 2):
            cols = slice((g * (group // 2) + pair) * LANES, (g * (group // 2) + pair + 1) * LANES)
            two = jnp.concatenate([ot[:, (2 * pair) * w_:(2 * pair + 1) * w_],
                                   ot[:, (2 * pair + 1) * w_:(2 * pair + 2) * w_]], axis=0)
            ma_scr[rows, cols] = (two.T * sg_ref[0, rows, cols].astype(F32)).astype(BF16)

    npiece = MIX // COLS_OUT

    def project(k, j):
        rows = slice(k * ROWS_OUT, (k + 1) * ROWS_OUT)
        cols = slice(j * COLS_OUT, (j + 1) * COLS_OUT)
        o_ref[0, rows, cols] = (
            x_ref[0, rows, cols]
            + jnp.dot(ma_scr[rows, :], wout_ref[0:A_Q, cols], preferred_element_type=F32)
            + jnp.dot(mb_ref[0, rows, :], wout_ref[A_Q:MIX, cols], preferred_element_type=F32))
        if final_norm and j == npiece - 1:
            y = o_ref[0, rows, :]
            ms = jnp.mean(y * y, axis=-1, keepdims=True)
            o_ref[0, rows, :] = y * lax.rsqrt(ms + RMS_EPS) * fg_ref[...]

    chains = [(w, g) for w in range(TQ_SWA // w_) for g in range(SWA_KV_HEADS)]
    chains_per_chunk = (ROWS_OUT // w_) * SWA_KV_HEADS
    ahead = 2
    pending = [scores(*ch) for ch in chains[:ahead]]
    ready = []
    for n, (w, g) in enumerate(chains):
        if n + ahead < len(chains):
            pending.append(scores(*chains[n + ahead]))
        consume(w, g, pending.pop(0))
        if (n + 1) % chains_per_chunk == 0:
            ready += [(n // chains_per_chunk, j) for j in range(npiece)]
        if ready:
            project(*ready.pop(0))
    for unit in ready:
        project(*unit)


def _swa_out(sinks, aqt, ak, avt, sg, x, mixed_b, wout, fg, final_norm):
    b, s, d = x.shape
    tq = TQ_SWA
    wpt = tq // WINDOW
    vr = SWA_KV_HEADS * VROWS_A
    prev_w = lambda i: jnp.maximum(i * wpt - 1, 0)
    tile = lambda bi, i: (bi, i, 0)
    whole = lambda bi, i: (0, 0)
    return pl.pallas_call(
        functools.partial(_swa_out_kernel, final_norm=final_norm),
        grid=(b, s // tq),
        in_specs=[pl.BlockSpec(memory_space=pltpu.SMEM),
                  pl.BlockSpec((1, A_Q, tq), lambda bi, i: (bi, 0, i)),
                  pl.BlockSpec((1, WINDOW, A_KV), lambda bi, i: (bi, prev_w(i), 0)),
                  pl.BlockSpec((1, tq, A_KV), tile),
                  pl.BlockSpec((1, vr, WINDOW), lambda bi, i: (bi, 0, prev_w(i))),
                  pl.BlockSpec((1, vr, tq), lambda bi, i: (bi, 0, i)),
                  pl.BlockSpec((1, tq, A_Q), tile),
                  pl.BlockSpec((1, tq, d), tile),
                  pl.BlockSpec((1, tq, B_W), tile),
                  pl.BlockSpec((MIX, d), whole),
                  pl.BlockSpec((1, d), whole)],
        out_specs=pl.BlockSpec((1, tq, d), tile),
        out_shape=jax.ShapeDtypeStruct((b, s, d), F32),
        scratch_shapes=[pltpu.VMEM((tq + WINDOW, A_KV), BF16),
                        pltpu.VMEM((vr, tq + WINDOW), BF16),
                        pltpu.VMEM((tq, A_Q), BF16)],
        compiler_params=pltpu.CompilerParams(
            dimension_semantics=("arbitrary", "arbitrary"), vmem_limit_bytes=VMEM_LIMIT),
        name="swa_out",
    )(sinks, aqt, ak, ak, avt, avt, sg, x, mixed_b, wout, fg)


def _diff_kernel(slopes_ref, lq1_ref, lk1_ref, lq2_ref, lk2_ref, subg_ref,
                 qt_ref, k_ref, vt_ref, sg_ref, o_ref,
                 qa_ref, s_ref, m_ref, acc_ref, *, lambda_init):
    tg, tk, tq = TG_DIFF, TK_DIFF, TQ_DIFF
    nchain = 2 * NG_DIFF
    ntile = qt_ref.shape[1]
    hd = pl.program_id(1)
    slope = slopes_ref[hd] * LOG2E

    def key_consts(n):
        kk = lax.broadcasted_iota(jnp.int32, (n, LANES), 0).astype(F32)
        lane = lax.broadcasted_iota(jnp.int32, (n, LANES), 1)
        ab = slope * kk
        ab_hi = ab.astype(BF16).astype(F32)
        ab_lo = ab - ab_hi
        zk = jnp.zeros((n, LANES), F32)
        bias1 = jnp.where(lane == HEAD_DIM, ab_hi,
                          jnp.where(lane == HEAD_DIM + 1, ab_lo, zk)).astype(BF16)
        bias2 = jnp.where(lane == 0, ab_hi, jnp.where(lane == 1, ab_lo, zk)).astype(BF16)
        return lane < HEAD_DIM, bias1, bias2

    lam = (jnp.exp(jnp.sum(lq1_ref[...] * lk1_ref[...], axis=-1, keepdims=True))
           - jnp.exp(jnp.sum(lq2_ref[...] * lk2_ref[...], axis=-1, keepdims=True))
           + lambda_init)

    def build_queries(t):
        rowq = lax.broadcasted_iota(jnp.int32, (LANES, tg), 0)
        one = jnp.ones((LANES, tg), F32)
        zq = jnp.zeros((LANES, tg), F32)
        for g in range(NG_DIFF):
            qt = qt_ref[0, t, :, g * tg:(g + 1) * tg].astype(F32)
            qa_ref[t, 2 * g] = jnp.where(
                rowq < HEAD_DIM, qt, jnp.where(rowq < HEAD_DIM + 2, one, zq)).astype(BF16)
            qa_ref[t, 2 * g + 1] = jnp.where(
                rowq >= HEAD_DIM, qt, jnp.where(rowq < 2, one, zq)).astype(BF16)

    def tile(t):
        first_tile = t == 0
        slot = t

        def reset(c):
            m_ref[slot,c] = jnp.full(m_ref.shape[2:], NEG_BIG, F32)
            acc_ref[slot,c] = jnp.zeros(acc_ref.shape[2:], F32)

        def scores(buf, c, ka, n):
            s_ref[buf, c, 0:n, :] = jnp.dot(ka, qa_ref[slot,c], preferred_element_type=F32)

        kconst = {n: key_consts(n) for n in ((tg,) if first_tile else (tg, tk))}
        krow = lax.broadcasted_iota(jnp.int32, (tg, tg), 0)
        qcol = lax.broadcasted_iota(jnp.int32, (tg, tg), 1)
        causal = krow <= qcol

        def consume(buf, c, vt, cj, n, masked):
            s = s_ref[buf, c, 0:n, :]
            if masked:
                s = jnp.where(causal, s, NEG_BIG)
            m_old = m_ref[slot,c]
            m_new = jnp.maximum(m_old, jnp.max(s, axis=0, keepdims=True) + cj)
            alpha = jnp.exp2(m_old - m_new)
            p = jnp.exp2(s - (m_new - cj)).astype(BF16)
            acc_ref[slot,c] = alpha * acc_ref[slot,c] + jnp.dot(vt, p, preferred_element_type=F32)
            m_ref[slot,c] = m_new

        def keys(start, n):
            k = k_ref[0, start:start + n, :]
            lane_lo, bias1, bias2 = kconst[n]
            return jnp.where(lane_lo, k, bias1), jnp.where(lane_lo, bias2, k)

        def block_bias(start):
            return slope * float(start)

        def finish(g):
            a1 = acc_ref[slot,2 * g]
            a2 = acc_ref[slot,2 * g + 1]
            ot = (a1[0:LANES] / a1[LANES:LANES + 1]
                  - lam * (a2[0:LANES] / a2[LANES:LANES + 1]))
            ot = ot * lax.rsqrt(jnp.mean(ot * ot, axis=0, keepdims=True) + SUBLN_EPS) * subg_ref[...]
            ot = ot * (1.0 - lambda_init)
            rows = slice(t * tq + g * tg, t * tq + (g + 1) * tg)
            o_ref[0, rows, :] = (ot.T * sg_ref[0, rows, :].astype(F32)).astype(BF16)

        base = t * tq
        build_queries(t)
        ka = keys(base, tg)
        for c in range(nchain):
            reset(c)
            scores(0, c, ka[c % 2], tg)
        ka_first = None if first_tile else keys(0, tk)
        for jj in range(NG_DIFF):
            cur, nxt = jj % 2, 1 - jj % 2
            start = base + jj * tg
            vt = vt_ref[0, t * (tq // tk) + (jj * tg) // tk, :,
                        (jj * tg) % tk:(jj * tg) % tk + tg]
            cj = block_bias(start)
            ka = keys(start + tg, tg) if jj + 1 < NG_DIFF else None
            for g in range(jj, NG_DIFF):
                for c in (2 * g, 2 * g + 1):
                    if g > jj:
                        scores(nxt, c, ka[c % 2], tg)
                    consume(cur, c, vt, cj, tg, g == jj)
                    if g == jj and not first_tile:
                        scores(0, c, ka_first[c % 2], tk)
                if g == jj and first_tile:
                    finish(g)
        nblk = base // tk
        for j in range(nblk):
            cur = j % 2
            last = j == nblk - 1
            vt = vt_ref[0, j]
            cj = block_bias(j * tk)
            kan = None if last else keys((j + 1) * tk, tk)
            for g in range(NG_DIFF):
                for c in (2 * g, 2 * g + 1):
                    if not last:
                        scores(1 - cur, c, kan[c % 2], tk)
                    consume(cur, c, vt, cj, tk, False)
                if last:
                    finish(g)

    arms = [list(range(min(3, ntile)))] + [[n] for n in range(3, ntile)]

    def run(tiles):
        for n in tiles:
            tile(n)

    def arm_body(a, carry):
        for n, tiles in enumerate(arms):
            pl.when(a == n)(functools.partial(run, tiles))
        return carry

    lax.fori_loop(0, len(arms), arm_body, 0)


def _diff(slopes, lq1, lk1, lq2, lk2, subg, bqt, bk, bvt, sg, lambda_init):
    b, s, _ = bk.shape
    tg, tk, tq = TG_DIFF, TK_DIFF, TQ_DIFF
    nchain = 2 * NG_DIFF
    smem = pl.BlockSpec(memory_space=pltpu.SMEM)
    small = lambda shape: pl.BlockSpec(shape, lambda bi, h: (0, 0))
    return pl.pallas_call(
        functools.partial(_diff_kernel, lambda_init=lambda_init),
        grid=(b, DIFF_HEADS),
        in_specs=[smem,
                  small((1, HEAD_DIM)), small((1, HEAD_DIM)),
                  small((1, HEAD_DIM)), small((1, HEAD_DIM)),
                  small((LANES, 1)),
                  pl.BlockSpec((1, s // tq, LANES, tq), lambda bi, h: (bi, 0, h, 0)),
                  pl.BlockSpec((1, s, LANES), lambda bi, h: (bi, 0, h)),
                  pl.BlockSpec((1, s // tk, VROWS, tk), lambda bi, h: (bi, 0, h, 0)),
                  pl.BlockSpec((1, s, LANES), lambda bi, h: (bi, 0, A_Q // LANES + h))],
        out_specs=pl.BlockSpec((1, s, LANES), lambda bi, h: (bi, 0, h)),
        out_shape=jax.ShapeDtypeStruct((b, s, B_W), BF16),
        scratch_shapes=[pltpu.VMEM((s // tq, nchain, LANES, tg), BF16),
                        pltpu.VMEM((2, nchain, tk, tg), F32),
                        pltpu.VMEM((s // tq, nchain, 1, tg), F32),
                        pltpu.VMEM((s // tq, nchain, VROWS, tg), F32)],
        compiler_params=pltpu.CompilerParams(
            dimension_semantics=("arbitrary", "arbitrary"),
            vmem_limit_bytes=VMEM_LIMIT),
        name="diffattn",
    )(slopes, lq1, lk1, lq2, lk2, subg, bqt, bk, bvt, sg)


def kernel(x, norm_g, w_in, sinks, lambda_q1, lambda_k1, lambda_q2, lambda_k2,
           subln_g, w_out, final_g):
    b, s, d = x.shape
    depth = norm_g.shape[0]
    diff_slopes = jnp.asarray(
        [2.0 ** (-8.0 * (h + 1) / DIFF_HEADS) for h in range(DIFF_HEADS)], F32)
    h3 = x
    for layer in range(depth):
        aqt, ak, avt, bqt, bk, bvt, sg = _inproj(
            h3, norm_g[layer].reshape(1, d), w_in[layer].astype(BF16))
        mixed_b = _diff(diff_slopes,
                        lambda_q1[layer].reshape(1, HEAD_DIM), lambda_k1[layer].reshape(1, HEAD_DIM),
                        lambda_q2[layer].reshape(1, HEAD_DIM), lambda_k2[layer].reshape(1, HEAD_DIM),
                        subln_g[layer].reshape(LANES, 1),
                        bqt, bk, bvt, sg, _lambda_init(layer))
        h3 = _swa_out(sinks[layer], aqt, ak, avt, sg, h3, mixed_b,
                      w_out[layer].astype(BF16), final_g.reshape(1, d),
                      final_norm=(layer == depth - 1))
    return h3
```

```python
import functools
import math

import jax
import jax.numpy as jnp
import numpy as np
from jax import lax
from jax.experimental import pallas as pl
from jax.experimental.pallas import tpu as pltpu

F32 = jnp.float32
BF16 = jnp.bfloat16

HEAD_DIM = 64
LANES = 128
SWA_Q_HEADS = 8
SWA_KV_HEADS = 2
WINDOW = 128
DIFF_HEADS = 4
RMS_EPS = 1e-6
SUBLN_EPS = 1e-5
NEG_BIG = -1e30
LOG2E = math.log2(math.e)

A_Q = SWA_Q_HEADS * HEAD_DIM
A_KV = SWA_KV_HEADS * HEAD_DIM
B_W = DIFF_HEADS * 2 * HEAD_DIM
MIX = A_Q + B_W
OFF_AK = A_Q
OFF_AV = OFF_AK + A_KV
OFF_BQ = OFF_AV + A_KV
OFF_BK = OFF_BQ + B_W
OFF_BV = OFF_BK + B_W
OFF_GATE = OFF_BV + B_W
IN_COLS = OFF_GATE + MIX

TM_PROJ = 1024
ROWS_IN = 256
ROWS_OUT = 256
COLS_OUT = 256
TQ_SWA = 1024
TG_DIFF = 256
NG_DIFF = 4
TQ_DIFF = NG_DIFF * TG_DIFF
TK_DIFF = 512
VROWS = LANES + 16
VROWS_A = HEAD_DIM + 16
VMEM_LIMIT = 56 * 1024 * 1024


def _lambda_init(layer_idx):
    return 0.8 - 0.6 * math.exp(-0.3 * layer_idx)


def _nt_dot(a, b):
    return lax.dot_general(a, b, (((1,), (1,)), ((), ())), preferred_element_type=F32)


def _inproj_kernel(x_ref, g_ref, w_ref,
                   aqt_ref, ak_ref, avt_ref, bqt_ref, bk_ref, bvt_ref, sg_ref):
    qscale = HEAD_DIM ** -0.5 * LOG2E
    th = ROWS_IN
    for r in range(0, TM_PROJ, th):
        rows = slice(r, r + th)
        x = x_ref[0, rows, :]
        ms = jnp.mean(x * x, axis=-1, keepdims=True)
        h = (x * lax.rsqrt(ms + RMS_EPS) * g_ref[...]).astype(BF16)

        def tok(lo, hi, h=h):
            return jnp.dot(h, w_ref[:, lo:hi], preferred_element_type=F32)

        def feat(lo, hi, h=h):
            return lax.dot_general(w_ref[:, lo:hi], h, (((0,), (1,)), ((), ())),
                                   preferred_element_type=F32)

        ak_ref[0, rows, :] = tok(OFF_AK, OFF_AV).astype(BF16)
        bk_ref[0, rows, :] = tok(OFF_BK, OFF_BV).astype(BF16)
        gate = tok(OFF_GATE, IN_COLS)
        sg_ref[0, rows, :] = (gate * (1.0 / (1.0 + jnp.exp(-gate)))).astype(BF16)

        aqt_ref[0, :, rows] = (feat(0, OFF_AK) * qscale).astype(BF16)
        avt = feat(OFF_AV, OFF_BQ).astype(BF16)
        for g in range(SWA_KV_HEADS):
            avt_ref[0, g * VROWS_A:g * VROWS_A + HEAD_DIM, rows] = avt[g * HEAD_DIM:(g + 1) * HEAD_DIM]
            avt_ref[0, g * VROWS_A + HEAD_DIM:(g + 1) * VROWS_A, rows] = jnp.ones(
                (VROWS_A - HEAD_DIM, th), BF16)
        bqt_ref[0, r // TQ_DIFF, :, r % TQ_DIFF:r % TQ_DIFF + th] = (
            feat(OFF_BQ, OFF_BK) * qscale).astype(BF16)
        bvt = feat(OFF_BV, OFF_GATE).astype(BF16)
        vcols = slice(r % TK_DIFF, r % TK_DIFF + th)
        for hd in range(DIFF_HEADS):
            bvt_ref[0, r // TK_DIFF, hd * VROWS:hd * VROWS + LANES, vcols] = bvt[hd * LANES:(hd + 1) * LANES]
            bvt_ref[0, r // TK_DIFF, hd * VROWS + LANES:(hd + 1) * VROWS, vcols] = jnp.ones(
                (VROWS - LANES, th), BF16)


def _inproj(x, g, w):
    b, s, d = x.shape
    tm = TM_PROJ
    row = lambda bi, i: (bi, i, 0)
    col = lambda bi, i: (bi, 0, i)
    whole = lambda bi, i: (0, 0)
    bsd = lambda wd: jax.ShapeDtypeStruct((b, s, wd), BF16)
    bds = lambda wd: jax.ShapeDtypeStruct((b, wd, s), BF16)
    return pl.pallas_call(
        _inproj_kernel,
        grid=(b, s // tm),
        in_specs=[pl.BlockSpec((1, tm, d), row),
                  pl.BlockSpec((1, d), whole),
                  pl.BlockSpec(w.shape, whole)],
        out_specs=[pl.BlockSpec((1, A_Q, tm), col),
                   pl.BlockSpec((1, tm, A_KV), row),
                   pl.BlockSpec((1, SWA_KV_HEADS * VROWS_A, tm), col),
                   pl.BlockSpec((1, tm // TQ_DIFF, B_W, TQ_DIFF), lambda bi, i: (bi, i, 0, 0)),
                   pl.BlockSpec((1, tm, B_W), row),
                   pl.BlockSpec((1, tm // TK_DIFF, DIFF_HEADS * VROWS, TK_DIFF), lambda bi, i: (bi, i, 0, 0)),
                   pl.BlockSpec((1, tm, MIX), row)],
        out_shape=[bds(A_Q), bsd(A_KV), bds(SWA_KV_HEADS * VROWS_A),
                   jax.ShapeDtypeStruct((b, s // TQ_DIFF, B_W, TQ_DIFF), BF16), bsd(B_W),
                   jax.ShapeDtypeStruct((b, s // TK_DIFF, DIFF_HEADS * VROWS, TK_DIFF), BF16),
                   bsd(MIX)],
        compiler_params=pltpu.CompilerParams(
            dimension_semantics=("arbitrary", "arbitrary"), vmem_limit_bytes=VMEM_LIMIT),
        name="inproj",
    )(x, g, w)


def _swa_out_kernel(sinks_ref, qt_ref, kp_ref, kc_ref, vtp_ref, vtc_ref, sg_ref,
                    x_ref, mb_ref, wout_ref, fg_ref, o_ref, kcat, vtcat, ma_scr, *, final_norm):
    w_ = WINDOW
    i = pl.program_id(1)
    group = SWA_Q_HEADS // SWA_KV_HEADS
    nq = group * w_
    kcat[0:w_, :] = kp_ref[0]
    kcat[w_:, :] = kc_ref[0]
    vtcat[:, 0:w_] = vtp_ref[0]
    vtcat[:, w_:] = vtc_ref[0]

    kidx = lax.broadcasted_iota(jnp.int32, (2 * w_, nq), 0)
    qidx = lax.broadcasted_iota(jnp.int32, (2 * w_, nq), 1) & (w_ - 1)
    in_cur = kidx >= w_
    band = (in_cur & (kidx - w_ <= qidx)) | ((kidx < w_) & (kidx > qidx))
    first_band = band & ((i > 0) | in_cur)

    lane = lax.broadcasted_iota(jnp.int32, (2 * w_, LANES), 1)
    kpos = lax.broadcasted_iota(jnp.int32, (2 * w_, LANES), 0).astype(F32)
    zk = jnp.zeros((2 * w_, LANES), F32)
    kmask = (lane < HEAD_DIM, lane >= HEAD_DIM)
    kpos_lanes = (jnp.where((lane == HEAD_DIM) | (lane == HEAD_DIM + 1), kpos, zk).astype(BF16),
                  jnp.where(lane < 2, kpos, zk).astype(BF16))

    hrow = lax.broadcasted_iota(jnp.int32, (HEAD_DIM, nq), 0)
    hcol = lax.broadcasted_iota(jnp.int32, (HEAD_DIM, nq), 1)

    def slope_rows(g):
        out = jnp.zeros((HEAD_DIM, nq), F32)
        for u in range(group):
            sl = 2.0 ** (-8.0 * (g * group + u + 1) / SWA_Q_HEADS) * LOG2E
            hi = float(np.asarray(sl, dtype=BF16).astype(np.float32))
            lo = sl - hi
            in_head = (hcol >= u * w_) & (hcol < (u + 1) * w_)
            out = jnp.where(in_head & (hrow == 0), hi, jnp.where(in_head & (hrow == 1), lo, out))
        return out.astype(BF16)

    srows = [slope_rows(g) for g in range(SWA_KV_HEADS)]

    qpos = (lax.broadcasted_iota(jnp.int32, (1, w_), 1) + w_).astype(F32)

    def sink_row(g):
        return jnp.concatenate(
            [(sinks_ref[g * group + u] + 2.0 ** (-8.0 * (g * group + u + 1) / SWA_Q_HEADS) * qpos) * LOG2E
             for u in range(group)], axis=1)

    sinkv = [sink_row(g) for g in range(SWA_KV_HEADS)]

    def scores(w, g):
        keys = kcat[w * w_:(w + 2) * w_, :]
        ka = jnp.where(kmask[g], keys, kpos_lanes[g])
        qh = jnp.concatenate(
            [qt_ref[0, (g * group + u) * HEAD_DIM:(g * group + u + 1) * HEAD_DIM, w * w_:(w + 1) * w_]
             for u in range(group)], axis=1)
        wq = jnp.concatenate([qh, srows[g]] if g == 0 else [srows[g], qh], axis=0)
        return jnp.dot(ka, wq, preferred_element_type=F32)

    def consume(w, g, s):
        s = jnp.where(first_band if w == 0 else band, s, NEG_BIG)
        m = jnp.maximum(jnp.max(s, axis=0, keepdims=True), sinkv[g])
        p = jnp.exp2(s - m).astype(BF16)
        vt = vtcat[g * VROWS_A:(g + 1) * VROWS_A, w * w_:(w + 2) * w_]
        o = jnp.dot(vt, p, preferred_element_type=F32)
        denom = o[HEAD_DIM:HEAD_DIM + 1] + jnp.exp2(sinkv[g] - m)
        ot = o[0:HEAD_DIM] / denom
        rows = slice(w * w_, (w + 1) * w_)
        for pair in range(group // 2):
            cols = slice((g * (group // 2) + pair) * LANES, (g * (group // 2) + pair + 1) * LANES)
            two = jnp.concatenate([ot[:, (2 * pair) * w_:(2 * pair + 1) * w_],
                                   ot[:, (2 * pair + 1) * w_:(2 * pair + 2) * w_]], axis=0)
            ma_scr[rows, cols] = (two.T * sg_ref[0, rows, cols].astype(F32)).astype(BF16)

    npiece = MIX // COLS_OUT

    def project(k, j):
        rows = slice(k * ROWS_OUT, (k + 1) * ROWS_OUT)
        cols = slice(j * COLS_OUT, (j + 1) * COLS_OUT)
        o_ref[0, rows, cols] = (
            x_ref[0, rows, cols]
            + jnp.dot(ma_scr[rows, :], wout_ref[0:A_Q, cols], preferred_element_type=F32)
            + jnp.dot(mb_ref[0, rows, :], wout_ref[A_Q:MIX, cols], preferred_element_type=F32))
        if final_norm and j == npiece - 1:
            y = o_ref[0, rows, :]
            ms = jnp.mean(y * y, axis=-1, keepdims=True)
            o_ref[0, rows, :] = y * lax.rsqrt(ms + RMS_EPS) * fg_ref[...]

    chains = [(w, g) for w in range(TQ_SWA // w_) for g in range(SWA_KV_HEADS)]
    chains_per_chunk = (ROWS_OUT // w_) * SWA_KV_HEADS
    ahead = 2
    pending = [scores(*ch) for ch in chains[:ahead]]
    ready = []
    for n, (w, g) in enumerate(chains):
        if n + ahead < len(chains):
            pending.append(scores(*chains[n + ahead]))
        consume(w, g, pending.pop(0))
        if (n + 1) % chains_per_chunk == 0:
            ready += [(n // chains_per_chunk, j) for j in range(npiece)]
        if ready:
            project(*ready.pop(0))
    for unit in ready:
        project(*unit)


def _swa_out(sinks, aqt, ak, avt, sg, x, mixed_b, wout, fg, final_norm):
    b, s, d = x.shape
    tq = TQ_SWA
    wpt = tq // WINDOW
    vr = SWA_KV_HEADS * VROWS_A
    prev_w = lambda i: jnp.maximum(i * wpt - 1, 0)
    tile = lambda bi, i: (bi, i, 0)
    whole = lambda bi, i: (0, 0)
    return pl.pallas_call(
        functools.partial(_swa_out_kernel, final_norm=final_norm),
        grid=(b, s // tq),
        in_specs=[pl.BlockSpec(memory_space=pltpu.SMEM),
                  pl.BlockSpec((1, A_Q, tq), lambda bi, i: (bi, 0, i)),
                  pl.BlockSpec((1, WINDOW, A_KV), lambda bi, i: (bi, prev_w(i), 0)),
                  pl.BlockSpec((1, tq, A_KV), tile),
                  pl.BlockSpec((1, vr, WINDOW), lambda bi, i: (bi, 0, prev_w(i))),
                  pl.BlockSpec((1, vr, tq), lambda bi, i: (bi, 0, i)),
                  pl.BlockSpec((1, tq, A_Q), tile),
                  pl.BlockSpec((1, tq, d), tile),
                  pl.BlockSpec((1, tq, B_W), tile),
                  pl.BlockSpec((MIX, d), whole),
                  pl.BlockSpec((1, d), whole)],
        out_specs=pl.BlockSpec((1, tq, d), tile),
        out_shape=jax.ShapeDtypeStruct((b, s, d), F32),
        scratch_shapes=[pltpu.VMEM((tq + WINDOW, A_KV), BF16),
                        pltpu.VMEM((vr, tq + WINDOW), BF16),
                        pltpu.VMEM((tq, A_Q), BF16)],
        compiler_params=pltpu.CompilerParams(
            dimension_semantics=("arbitrary", "arbitrary"), vmem_limit_bytes=VMEM_LIMIT),
        name="swa_out",
    )(sinks, aqt, ak, ak, avt, avt, sg, x, mixed_b, wout, fg)


def _diff_kernel(slopes_ref, lq1_ref, lk1_ref, lq2_ref, lk2_ref, subg_ref,
                 qt_ref, k_ref, vt_ref, sg_ref, o_ref,
                 qa_ref, s_ref, m_ref, acc_ref, *, lambda_init):
    tg, tk, tq = TG_DIFF, TK_DIFF, TQ_DIFF
    nchain = 2 * NG_DIFF
    ntile = qt_ref.shape[1]
    hd = pl.program_id(1)
    slope = slopes_ref[hd] * LOG2E

    def key_consts(n):
        kk = lax.broadcasted_iota(jnp.int32, (n, LANES), 0).astype(F32)
        lane = lax.broadcasted_iota(jnp.int32, (n, LANES), 1)
        ab = slope * kk
        ab_hi = ab.astype(BF16).astype(F32)
        ab_lo = ab - ab_hi
        zk = jnp.zeros((n, LANES), F32)
        bias1 = jnp.where(lane == HEAD_DIM, ab_hi,
                          jnp.where(lane == HEAD_DIM + 1, ab_lo, zk)).astype(BF16)
        bias2 = jnp.where(lane == 0, ab_hi, jnp.where(lane == 1, ab_lo, zk)).astype(BF16)
        return lane < HEAD_DIM, bias1, bias2

    lam = (jnp.exp(jnp.sum(lq1_ref[...] * lk1_ref[...], axis=-1, keepdims=True))
           - jnp.exp(jnp.sum(lq2_ref[...] * lk2_ref[...], axis=-1, keepdims=True))
           + lambda_init)

    def build_queries(t):
        rowq = lax.broadcasted_iota(jnp.int32, (LANES, tg), 0)
        one = jnp.ones((LANES, tg), F32)
        zq = jnp.zeros((LANES, tg), F32)
        for g in range(NG_DIFF):
            qt = qt_ref[0, t, :, g * tg:(g + 1) * tg].astype(F32)
            qa_ref[t, 2 * g] = jnp.where(
                rowq < HEAD_DIM, qt, jnp.where(rowq < HEAD_DIM + 2, one, zq)).astype(BF16)
            qa_ref[t, 2 * g + 1] = jnp.where(
                rowq >= HEAD_DIM, qt, jnp.where(rowq < 2, one, zq)).astype(BF16)

    def tile(t):
        first_tile = t == 0
        slot = t

        def reset(c):
            m_ref[slot, c] = jnp.full(m_ref.shape[2:], NEG_BIG, F32)
            acc_ref[slot, c] = jnp.zeros(acc_ref.shape[2:], F32)

        def scores(buf, c, ka, n):
            s_ref[buf, c, 0:n, :] = jnp.dot(ka, qa_ref[slot, c], preferred_element_type=F32)

        kconst = {n: key_consts(n) for n in ((tg,) if first_tile else (tg, tk))}
        krow = lax.broadcasted_iota(jnp.int32, (tg, tg), 0)
        qcol = lax.broadcasted_iota(jnp.int32, (tg, tg), 1)
        causal = krow <= qcol

        def consume(buf, c, vt, cj, n, masked):
            s = s_ref[buf, c, 0:n, :]
            if masked:
                s = jnp.where(causal, s, NEG_BIG)
            m_old = m_ref[slot, c]
            m_new = jnp.maximum(m_old, jnp.max(s, axis=0, keepdims=True) + cj)
            alpha = jnp.exp2(m_old - m_new)
            p = jnp.exp2(s - (m_new - cj)).astype(BF16)
            acc_ref[slot, c] = alpha * acc_ref[slot, c] + jnp.dot(vt, p, preferred_element_type=F32)
            m_ref[slot, c] = m_new

        def keys(start, n):
            k = k_ref[0, start:start + n, :]
            lane_lo, bias1, bias2 = kconst[n]
            return jnp.where(lane_lo, k, bias1), jnp.where(lane_lo, bias2, k)

        def block_bias(start):
            return slope * float(start)

        def finish(g):
            a1 = acc_ref[slot, 2 * g]
            a2 = acc_ref[slot, 2 * g + 1]
            ot = (a1[0:LANES] / a1[LANES:LANES + 1]
                  - lam * (a2[0:LANES] / a2[LANES:LANES + 1]))
            ot = ot * lax.rsqrt(jnp.mean(ot * ot, axis=0, keepdims=True) + SUBLN_EPS) * subg_ref[...]
            ot = ot * (1.0 - lambda_init)
            rows = slice(t * tq + g * tg, t * tq + (g + 1) * tg)
            o_ref[0, rows, :] = (ot.T * sg_ref[0, rows, :].astype(F32)).astype(BF16)

        base = t * tq
        build_queries(t)
        ka = keys(base, tg)
        for c in range(nchain):
            reset(c)
            scores(0, c, ka[c % 2], tg)
        ka_first = None if first_tile else keys(0, tk)
        for jj in range(NG_DIFF):
            cur, nxt = jj % 2, 1 - jj % 2
            start = base + jj * tg
            vt = vt_ref[0, t * (tq // tk) + (jj * tg) // tk, :,
                        (jj * tg) % tk:(jj * tg) % tk + tg]
            cj = block_bias(start)
            ka = keys(start + tg, tg) if jj + 1 < NG_DIFF else None
            for g in range(jj, NG_DIFF):
                for c in (2 * g, 2 * g + 1):
                    if g > jj:
                        scores(nxt, c, ka[c % 2], tg)
                    consume(cur, c, vt, cj, tg, g == jj)
                    if g == jj and not first_tile:
                        scores(0, c, ka_first[c % 2], tk)
                if g == jj and first_tile:
                    finish(g)
        nblk = base // tk
        for j in range(nblk):
            cur = j % 2
            last = j == nblk - 1
            vt = vt_ref[0, j]
            cj = block_bias(j * tk)
            kan = None if last else keys((j + 1) * tk, tk)
            for g in range(NG_DIFF):
                for c in (2 * g, 2 * g + 1):
                    if not last:
                        scores(1 - cur, c, kan[c % 2], tk)
                    consume(cur, c, vt, cj, tk, False)
                if last:
                    finish(g)

    arms = [list(range(min(3, ntile)))] + [[n] for n in range(3, ntile)]

    def run(tiles):
        for n in tiles:
            tile(n)

    def arm_body(a, carry):
        for n, tiles in enumerate(arms):
            pl.when(a == n)(functools.partial(run, tiles))
        return carry

    lax.fori_loop(0, len(arms), arm_body, 0)


def _diff(slopes, lq1, lk1, lq2, lk2, subg, bqt, bk, bvt, sg, lambda_init):
    b, s, _ = bk.shape
    tg, tk, tq = TG_DIFF, TK_DIFF, TQ_DIFF
    nchain = 2 * NG_DIFF
    smem = pl.BlockSpec(memory_space=pltpu.SMEM)
    small = lambda shape: pl.BlockSpec(shape, lambda bi, h: (0, 0))
    return pl.pallas_call(
        functools.partial(_diff_kernel, lambda_init=lambda_init),
        grid=(b, DIFF_HEADS),
        in_specs=[smem,
                  small((1, HEAD_DIM)), small((1, HEAD_DIM)),
                  small((1, HEAD_DIM)), small((1, HEAD_DIM)),
                  small((LANES, 1)),
                  pl.BlockSpec((1, s // tq, LANES, tq), lambda bi, h: (bi, 0, h, 0)),
                  pl.BlockSpec((1, s, LANES), lambda bi, h: (bi, 0, h)),
                  pl.BlockSpec((1, s // tk, VROWS, tk), lambda bi, h: (bi, 0, h, 0)),
                  pl.BlockSpec((1, s, LANES), lambda bi, h: (bi, 0, A_Q // LANES + h))],
        out_specs=pl.BlockSpec((1, s, LANES), lambda bi, h: (bi, 0, h)),
        out_shape=jax.ShapeDtypeStruct((b, s, B_W), BF16),
        scratch_shapes=[pltpu.VMEM((s // tq, nchain, LANES, tg), BF16),
                        pltpu.VMEM((2, nchain, tk, tg), F32),
                        pltpu.VMEM((s // tq, nchain, 1, tg), F32),
                        pltpu.VMEM((s // tq, nchain, VROWS, tg), F32)],
        compiler_params=pltpu.CompilerParams(
            dimension_semantics=("arbitrary", "arbitrary"),
            vmem_limit_bytes=VMEM_LIMIT),
        name="diffattn",
    )(slopes, lq1, lk1, lq2, lk2, subg, bqt, bk, bvt, sg)


def kernel(x, norm_g, w_in, sinks, lambda_q1, lambda_k1, lambda_q2, lambda_k2,
           subln_g, w_out, final_g):
    b, s, d = x.shape
    depth = norm_g.shape[0]
    diff_slopes = jnp.asarray(
        [2.0 ** (-8.0 * (h + 1) / DIFF_HEADS) for h in range(DIFF_HEADS)], F32)
    h3 = x
    for layer in range(depth):
        aqt, ak, avt, bqt, bk, bvt, sg = _inproj(
            h3, norm_g[layer].reshape(1, d), w_in[layer].astype(BF16))
        mixed_b = _diff(diff_slopes,
                        lambda_q1[layer].reshape(1, HEAD_DIM), lambda_k1[layer].reshape(1, HEAD_DIM),
                        lambda_q2[layer].reshape(1, HEAD_DIM), lambda_k2[layer].reshape(1, HEAD_DIM),
                        subln_g[layer].reshape(LANES, 1),
                        bqt, bk, bvt, sg, _lambda_init(layer))
        h3 = _swa_out(sinks[layer], aqt, ak, avt, sg, h3, mixed_b,
                      w_out[layer].astype(BF16), final_g.reshape(1, d),
                      final_norm=(layer == depth - 1))
    return h3
```

```python
import functools
import math

import jax
import jax.numpy as jnp
import numpy as np
from jax import lax
from jax.experimental import pallas as pl
from jax.experimental.pallas import tpu as pltpu

F32 = jnp.float32
BF16 = jnp.bfloat16

HEAD_DIM = 64
LANES = 128
SWA_Q_HEADS = 8
SWA_KV_HEADS = 2
WINDOW = 128
DIFF_HEADS = 4
RMS_EPS = 1e-6
SUBLN_EPS = 1e-5
NEG_BIG = -1e30
LOG2E = math.log2(math.e)

A_Q = SWA_Q_HEADS * HEAD_DIM
A_KV = SWA_KV_HEADS * HEAD_DIM
B_W = DIFF_HEADS * 2 * HEAD_DIM
MIX = A_Q + B_W
OFF_AK = A_Q
OFF_AV = OFF_AK + A_KV
OFF_BQ = OFF_AV + A_KV
OFF_BK = OFF_BQ + B_W
OFF_BV = OFF_BK + B_W
OFF_GATE = OFF_BV + B_W
IN_COLS = OFF_GATE + MIX

TM_PROJ = 1024
ROWS_IN = 256
CAST_COLS = 256
ROWS_OUT = 256
COLS_OUT = 256
TQ_SWA = 1024
TG_DIFF = 256
NG_DIFF = 4
TQ_DIFF = NG_DIFF * TG_DIFF
TK_DIFF = 512
VROWS = LANES + 16
VROWS_A = HEAD_DIM + 16
VMEM_LIMIT = 56 * 1024 * 1024


def _lambda_init(layer_idx):
    return 0.8 - 0.6 * math.exp(-0.3 * layer_idx)


def _nt_dot(a, b):
    return lax.dot_general(a, b, (((1,), (1,)), ((), ())), preferred_element_type=F32)


def _cast_weight_once(src_ref, dst_ref):
    @pl.when((pl.program_id(0) == 0) & (pl.program_id(1) == 0))
    def _():
        for lo in range(0, src_ref.shape[1], CAST_COLS):
            dst_ref[:, lo:lo + CAST_COLS] = src_ref[:, lo:lo + CAST_COLS].astype(BF16)


def _inproj_kernel(x_ref, g_ref, wf_ref,
                   aqt_ref, ak_ref, avt_ref, bqt_ref, bk_ref, bvt_ref, sg_ref, w_ref):
    _cast_weight_once(wf_ref, w_ref)
    qscale = HEAD_DIM ** -0.5 * LOG2E
    th = ROWS_IN
    for r in range(0, TM_PROJ, th):
        rows = slice(r, r + th)
        x = x_ref[0, rows, :]
        ms = jnp.mean(x * x, axis=-1, keepdims=True)
        h = (x * lax.rsqrt(ms + RMS_EPS) * g_ref[...]).astype(BF16)

        def tok(lo, hi, h=h):
            return jnp.dot(h, w_ref[:, lo:hi], preferred_element_type=F32)

        def feat(lo, hi, h=h):
            return lax.dot_general(w_ref[:, lo:hi], h, (((0,), (1,)), ((), ())),
                                   preferred_element_type=F32)

        ak_ref[0, rows, :] = tok(OFF_AK, OFF_AV).astype(BF16)
        bk_ref[0, rows, :] = tok(OFF_BK, OFF_BV).astype(BF16)
        gate = tok(OFF_GATE, IN_COLS)
        sg_ref[0, rows, :] = (gate * (1.0 / (1.0 + jnp.exp(-gate)))).astype(BF16)

        aqt_ref[0, :, rows] = (feat(0, OFF_AK) * qscale).astype(BF16)
        avt = feat(OFF_AV, OFF_BQ).astype(BF16)
        for g in range(SWA_KV_HEADS):
            avt_ref[0, g * VROWS_A:g * VROWS_A + HEAD_DIM, rows] = avt[g * HEAD_DIM:(g + 1) * HEAD_DIM]
            avt_ref[0, g * VROWS_A + HEAD_DIM:(g + 1) * VROWS_A, rows] = jnp.ones(
                (VROWS_A - HEAD_DIM, th), BF16)
        bqt_ref[0, r // TQ_DIFF, :, r % TQ_DIFF:r % TQ_DIFF + th] = (
            feat(OFF_BQ, OFF_BK) * qscale).astype(BF16)
        bvt = feat(OFF_BV, OFF_GATE).astype(BF16)
        vcols = slice(r % TK_DIFF, r % TK_DIFF + th)
        for hd in range(DIFF_HEADS):
            bvt_ref[0, r // TK_DIFF, hd * VROWS:hd * VROWS + LANES, vcols] = bvt[hd * LANES:(hd + 1) * LANES]
            bvt_ref[0, r // TK_DIFF, hd * VROWS + LANES:(hd + 1) * VROWS, vcols] = jnp.ones(
                (VROWS - LANES, th), BF16)


def _inproj(x, g, w):
    b, s, d = x.shape
    tm = TM_PROJ
    row = lambda bi, i: (bi, i, 0)
    col = lambda bi, i: (bi, 0, i)
    whole = lambda bi, i: (0, 0)
    bsd = lambda wd: jax.ShapeDtypeStruct((b, s, wd), BF16)
    bds = lambda wd: jax.ShapeDtypeStruct((b, wd, s), BF16)
    return pl.pallas_call(
        _inproj_kernel,
        grid=(b, s // tm),
        in_specs=[pl.BlockSpec((1, tm, d), row),
                  pl.BlockSpec((1, d), whole),
                  pl.BlockSpec(w.shape, whole)],
        out_specs=[pl.BlockSpec((1, A_Q, tm), col),
                   pl.BlockSpec((1, tm, A_KV), row),
                   pl.BlockSpec((1, SWA_KV_HEADS * VROWS_A, tm), col),
                   pl.BlockSpec((1, tm // TQ_DIFF, B_W, TQ_DIFF), lambda bi, i: (bi, i, 0, 0)),
                   pl.BlockSpec((1, tm, B_W), row),
                   pl.BlockSpec((1, tm // TK_DIFF, DIFF_HEADS * VROWS, TK_DIFF), lambda bi, i: (bi, i, 0, 0)),
                   pl.BlockSpec((1, tm, MIX), row)],
        out_shape=[bds(A_Q), bsd(A_KV), bds(SWA_KV_HEADS * VROWS_A),
                   jax.ShapeDtypeStruct((b, s // TQ_DIFF, B_W, TQ_DIFF), BF16), bsd(B_W),
                   jax.ShapeDtypeStruct((b, s // TK_DIFF, DIFF_HEADS * VROWS, TK_DIFF), BF16),
                   bsd(MIX)],
        scratch_shapes=[pltpu.VMEM(w.shape, BF16)],
        compiler_params=pltpu.CompilerParams(
            dimension_semantics=("arbitrary", "arbitrary"), vmem_limit_bytes=VMEM_LIMIT),
        name="inproj",
    )(x, g, w)


def _swa_out_kernel(sinks_ref, qt_ref, kp_ref, kc_ref, vtp_ref, vtc_ref, sg_ref,
                    x_ref, mb_ref, woutf_ref, fg_ref, o_ref, kcat, vtcat, ma_scr, wout_ref,
                    *, final_norm):
    w_ = WINDOW
    i = pl.program_id(1)
    group = SWA_Q_HEADS // SWA_KV_HEADS
    nq = group * w_
    _cast_weight_once(woutf_ref, wout_ref)
    kcat[0:w_, :] = kp_ref[0]
    kcat[w_:, :] = kc_ref[0]
    vtcat[:, 0:w_] = vtp_ref[0]
    vtcat[:, w_:] = vtc_ref[0]

    kidx = lax.broadcasted_iota(jnp.int32, (2 * w_, nq), 0)
    qidx = lax.broadcasted_iota(jnp.int32, (2 * w_, nq), 1) & (w_ - 1)
    in_cur = kidx >= w_
    band = (in_cur & (kidx - w_ <= qidx)) | ((kidx < w_) & (kidx > qidx))
    first_band = band & ((i > 0) | in_cur)

    lane = lax.broadcasted_iota(jnp.int32, (2 * w_, LANES), 1)
    kpos = lax.broadcasted_iota(jnp.int32, (2 * w_, LANES), 0).astype(F32)
    zk = jnp.zeros((2 * w_, LANES), F32)
    kmask = (lane < HEAD_DIM, lane >= HEAD_DIM)
    kpos_lanes = (jnp.where((lane == HEAD_DIM) | (lane == HEAD_DIM + 1), kpos, zk).astype(BF16),
                  jnp.where(lane < 2, kpos, zk).astype(BF16))

    hrow = lax.broadcasted_iota(jnp.int32, (HEAD_DIM, nq), 0)
    hcol = lax.broadcasted_iota(jnp.int32, (HEAD_DIM, nq), 1)

    def slope_rows(g):
        out = jnp.zeros((HEAD_DIM, nq), F32)
        for u in range(group):
            sl = 2.0 ** (-8.0 * (g * group + u + 1) / SWA_Q_HEADS) * LOG2E
            hi = float(np.asarray(sl, dtype=BF16).astype(np.float32))
            lo = sl - hi
            in_head = (hcol >= u * w_) & (hcol < (u + 1) * w_)
            out = jnp.where(in_head & (hrow == 0), hi, jnp.where(in_head & (hrow == 1), lo, out))
        return out.astype(BF16)

    srows = [slope_rows(g) for g in range(SWA_KV_HEADS)]

    qpos = (lax.broadcasted_iota(jnp.int32, (1, w_), 1) + w_).astype(F32)

    def sink_row(g):
        return jnp.concatenate(
            [(sinks_ref[g * group + u] + 2.0 ** (-8.0 * (g * group + u + 1) / SWA_Q_HEADS) * qpos) * LOG2E
             for u in range(group)], axis=1)

    sinkv = [sink_row(g) for g in range(SWA_KV_HEADS)]

    def scores(w, g):
        keys = kcat[w * w_:(w + 2) * w_, :]
        ka = jnp.where(kmask[g], keys, kpos_lanes[g])
        qh = jnp.concatenate(
            [qt_ref[0, (g * group + u) * HEAD_DIM:(g * group + u + 1) * HEAD_DIM, w * w_:(w + 1) * w_]
             for u in range(group)], axis=1)
        wq = jnp.concatenate([qh, srows[g]] if g == 0 else [srows[g], qh], axis=0)
        return jnp.dot(ka, wq, preferred_element_type=F32)

    def consume(w, g, s):
        s = jnp.where(first_band if w == 0 else band, s, NEG_BIG)
        m = jnp.maximum(jnp.max(s, axis=0, keepdims=True), sinkv[g])
        p = jnp.exp2(s - m).astype(BF16)
        vt = vtcat[g * VROWS_A:(g + 1) * VROWS_A, w * w_:(w + 2) * w_]
        o = jnp.dot(vt, p, preferred_element_type=F32)
        denom = o[HEAD_DIM:HEAD_DIM + 1] + jnp.exp2(sinkv[g] - m)
        ot = o[0:HEAD_DIM] / denom
        rows = slice(w * w_, (w + 1) * w_)
        for pair in range(group // 2):
            cols = slice((g * (group // 2) + pair) * LANES, (g * (group // 2) + pair + 1) * LANES)
            two = jnp.concatenate([ot[:, (2 * pair) * w_:(2 * pair + 1) * w_],
                                   ot[:, (2 * pair + 1) * w_:(2 * pair + 2) * w_]], axis=0)
            ma_scr[rows, cols] = (two.T * sg_ref[0, rows, cols].astype(F32)).astype(BF16)

    npiece = MIX // COLS_OUT

    def project(k, j):
        rows = slice(k * ROWS_OUT, (k + 1) * ROWS_OUT)
        cols = slice(j * COLS_OUT, (j + 1) * COLS_OUT)
        o_ref[0, rows, cols] = (
            x_ref[0, rows, cols]
            + jnp.dot(ma_scr[rows, :], wout_ref[0:A_Q, cols], preferred_element_type=F32)
            + jnp.dot(mb_ref[0, rows, :], wout_ref[A_Q:MIX, cols], preferred_element_type=F32))
        if final_norm and j == npiece - 1:
            y = o_ref[0, rows, :]
            ms = jnp.mean(y * y, axis=-1, keepdims=True)
            o_ref[0, rows, :] = y * lax.rsqrt(ms + RMS_EPS) * fg_ref[...]

    chains = [(w, g) for w in range(TQ_SWA // w_) for g in range(SWA_KV_HEADS)]
    chains_per_chunk = (ROWS_OUT // w_) * SWA_KV_HEADS
    ahead = 2
    pending = [scores(*ch) for ch in chains[:ahead]]
    ready = []
    for n, (w, g) in enumerate(chains):
        if n + ahead < len(chains):
            pending.append(scores(*chains[n + ahead]))
        consume(w, g, pending.pop(0))
        if (n + 1) % chains_per_chunk == 0:
            ready += [(n // chains_per_chunk, j) for j in range(npiece)]
        if ready:
            project(*ready.pop(0))
    for unit in ready:
        project(*unit)


def _swa_out(sinks, aqt, ak, avt, sg, x, mixed_b, wout, fg, final_norm):
    b, s, d = x.shape
    tq = TQ_SWA
    wpt = tq // WINDOW
    vr = SWA_KV_HEADS * VROWS_A
    prev_w = lambda i: jnp.maximum(i * wpt - 1, 0)
    tile = lambda bi, i: (bi, i, 0)
    whole = lambda bi, i: (0, 0)
    return pl.pallas_call(
        functools.partial(_swa_out_kernel, final_norm=final_norm),
        grid=(b, s // tq),
        in_specs=[pl.BlockSpec(memory_space=pltpu.SMEM),
                  pl.BlockSpec((1, A_Q, tq), lambda bi, i: (bi, 0, i)),
                  pl.BlockSpec((1, WINDOW, A_KV), lambda bi, i: (bi, prev_w(i), 0)),
                  pl.BlockSpec((1, tq, A_KV), tile),
                  pl.BlockSpec((1, vr, WINDOW), lambda bi, i: (bi, 0, prev_w(i))),
                  pl.BlockSpec((1, vr, tq), lambda bi, i: (bi, 0, i)),
                  pl.BlockSpec((1, tq, A_Q), tile),
                  pl.BlockSpec((1, tq, d), tile),
                  pl.BlockSpec((1, tq, B_W), tile),
                  pl.BlockSpec((MIX, d), whole),
                  pl.BlockSpec((1, d), whole)],
        out_specs=pl.BlockSpec((1, tq, d), tile),
        out_shape=jax.ShapeDtypeStruct((b, s, d), F32),
        scratch_shapes=[pltpu.VMEM((tq + WINDOW, A_KV), BF16),
                        pltpu.VMEM((vr, tq + WINDOW), BF16),
                        pltpu.VMEM((tq, A_Q), BF16),
                        pltpu.VMEM((MIX, d), BF16)],
        compiler_params=pltpu.CompilerParams(
            dimension_semantics=("arbitrary", "arbitrary"), vmem_limit_bytes=VMEM_LIMIT),
        name="swa_out",
    )(sinks, aqt, ak, ak, avt, avt, sg, x, mixed_b, wout, fg)


def _diff_kernel(slopes_ref, lq1_ref, lk1_ref, lq2_ref, lk2_ref, subg_ref,
                 qt_ref, k_ref, vt_ref, sg_ref, o_ref,
                 qa_ref, s_ref, m_ref, acc_ref, *, lambda_init):
    tg, tk, tq = TG_DIFF, TK_DIFF, TQ_DIFF
    nchain = 2 * NG_DIFF
    ntile = qt_ref.shape[1]
    hd = pl.program_id(1)
    slope = slopes_ref[hd] * LOG2E

    def key_consts(n):
        kk = lax.broadcasted_iota(jnp.int32, (n, LANES), 0).astype(F32)
        lane = lax.broadcasted_iota(jnp.int32, (n, LANES), 1)
        ab = slope * kk
        ab_hi = ab.astype(BF16).astype(F32)
        ab_lo = ab - ab_hi
        zk = jnp.zeros((n, LANES), F32)
        bias1 = jnp.where(lane == HEAD_DIM, ab_hi,
                          jnp.where(lane == HEAD_DIM + 1, ab_lo, zk)).astype(BF16)
        bias2 = jnp.where(lane == 0, ab_hi, jnp.where(lane == 1, ab_lo, zk)).astype(BF16)
        return lane < HEAD_DIM, bias1, bias2

    lam = (jnp.exp(jnp.sum(lq1_ref[...] * lk1_ref[...], axis=-1, keepdims=True))
           - jnp.exp(jnp.sum(lq2_ref[...] * lk2_ref[...], axis=-1, keepdims=True))
           + lambda_init)

    def build_queries(t):
        rowq = lax.broadcasted_iota(jnp.int32, (LANES, tg), 0)
        one = jnp.ones((LANES, tg), F32)
        zq = jnp.zeros((LANES, tg), F32)
        for g in range(NG_DIFF):
            qt = qt_ref[0, t, :, g * tg:(g + 1) * tg].astype(F32)
            qa_ref[t, 2 * g] = jnp.where(
                rowq < HEAD_DIM, qt, jnp.where(rowq < HEAD_DIM + 2, one, zq)).astype(BF16)
            qa_ref[t, 2 * g + 1] = jnp.where(
                rowq >= HEAD_DIM, qt, jnp.where(rowq < 2, one, zq)).astype(BF16)

    def tile(t):
        first_tile = t == 0
        slot = t

        def reset(c):
            m_ref[slot, c] = jnp.full(m_ref.shape[2:], NEG_BIG, F32)
            acc_ref[slot, c] = jnp.zeros(acc_ref.shape[2:], F32)

        def scores(buf, c, ka, n):
            s_ref[buf, c, 0:n, :] = jnp.dot(ka, qa_ref[slot, c], preferred_element_type=F32)

        kconst = {n: key_consts(n) for n in ((tg,) if first_tile else (tg, tk))}
        krow = lax.broadcasted_iota(jnp.int32, (tg, tg), 0)
        qcol = lax.broadcasted_iota(jnp.int32, (tg, tg), 1)
        causal = krow <= qcol

        def consume(buf, c, vt, cj, n, masked):
            s = s_ref[buf, c, 0:n, :]
            if masked:
                s = jnp.where(causal, s, NEG_BIG)
            m_old = m_ref[slot, c]
            m_new = jnp.maximum(m_old, jnp.max(s, axis=0, keepdims=True) + cj)
            alpha = jnp.exp2(m_old - m_new)
            p = jnp.exp2(s - (m_new - cj)).astype(BF16)
            acc_ref[slot, c] = alpha * acc_ref[slot, c] + jnp.dot(vt, p, preferred_element_type=F32)
            m_ref[slot, c] = m_new

        def keys(start, n):
            k = k_ref[0, start:start + n, :]
            lane_lo, bias1, bias2 = kconst[n]
            return jnp.where(lane_lo, k, bias1), jnp.where(lane_lo, bias2, k)

        def block_bias(start):
            return slope * float(start)

        def finish(g):
            a1 = acc_ref[slot, 2 * g]
            a2 = acc_ref[slot, 2 * g + 1]
            ot = (a1[0:LANES] / a1[LANES:LANES + 1]
                  - lam * (a2[0:LANES] / a2[LANES:LANES + 1]))
            ot = ot * lax.rsqrt(jnp.mean(ot * ot, axis=0, keepdims=True) + SUBLN_EPS) * subg_ref[...]
            ot = ot * (1.0 - lambda_init)
            rows = slice(t * tq + g * tg, t * tq + (g + 1) * tg)
            o_ref[0, rows, :] = (ot.T * sg_ref[0, rows, :].astype(F32)).astype(BF16)

        base = t * tq
        build_queries(t)
        ka = keys(base, tg)
        for c in range(nchain):
            reset(c)
            scores(0, c, ka[c % 2], tg)
        ka_first = None if first_tile else keys(0, tk)
        for jj in range(NG_DIFF):
            cur, nxt = jj % 2, 1 - jj % 2
            start = base + jj * tg
            vt = vt_ref[0, t * (tq // tk) + (jj * tg) // tk, :,
                        (jj * tg) % tk:(jj * tg) % tk + tg]
            cj = block_bias(start)
            ka = keys(start + tg, tg) if jj + 1 < NG_DIFF else None
            for g in range(jj, NG_DIFF):
                for c in (2 * g, 2 * g + 1):
                    if g > jj:
                        scores(nxt, c, ka[c % 2], tg)
                    consume(cur, c, vt, cj, tg, g == jj)
                    if g == jj and not first_tile:
                        scores(0, c, ka_first[c % 2], tk)
                if g == jj and first_tile:
                    finish(g)
        nblk = base // tk
        for j in range(nblk):
            cur = j % 2
            last = j == nblk - 1
            vt = vt_ref[0, j]
            cj = block_bias(j * tk)
            kan = None if last else keys((j + 1) * tk, tk)
            for g in range(NG_DIFF):
                for c in (2 * g, 2 * g + 1):
                    if not last:
                        scores(1 - cur, c, kan[c % 2], tk)
                    consume(cur, c, vt, cj, tk, False)
                if last:
                    finish(g)

    arms = [list(range(min(3, ntile)))] + [[n] for n in range(3, ntile)]

    def run(tiles):
        for n in tiles:
            tile(n)

    def arm_body(a, carry):
        for n, tiles in enumerate(arms):
            pl.when(a == n)(functools.partial(run, tiles))
        return carry

    lax.fori_loop(0, len(arms), arm_body, 0)


def _diff(slopes, lq1, lk1, lq2, lk2, subg, bqt, bk, bvt, sg, lambda_init):
    b, s, _ = bk.shape
    tg, tk, tq = TG_DIFF, TK_DIFF, TQ_DIFF
    nchain = 2 * NG_DIFF
    smem = pl.BlockSpec(memory_space=pltpu.SMEM)
    small = lambda shape: pl.BlockSpec(shape, lambda bi, h: (0, 0))
    return pl.pallas_call(
        functools.partial(_diff_kernel, lambda_init=lambda_init),
        grid=(b, DIFF_HEADS),
        in_specs=[smem,
                  small((1, HEAD_DIM)), small((1, HEAD_DIM)),
                  small((1, HEAD_DIM)), small((1, HEAD_DIM)),
                  small((LANES, 1)),
                  pl.BlockSpec((1, s // tq, LANES, tq), lambda bi, h: (bi, 0, h, 0)),
                  pl.BlockSpec((1, s, LANES), lambda bi, h: (bi, 0, h)),
                  pl.BlockSpec((1, s // tk, VROWS, tk), lambda bi, h: (bi, 0, h, 0)),
                  pl.BlockSpec((1, s, LANES), lambda bi, h: (bi, 0, A_Q // LANES + h))],
        out_specs=pl.BlockSpec((1, s, LANES), lambda bi, h: (bi, 0, h)),
        out_shape=jax.ShapeDtypeStruct((b, s, B_W), BF16),
        scratch_shapes=[pltpu.VMEM((s // tq, nchain, LANES, tg), BF16),
                        pltpu.VMEM((2, nchain, tk, tg), F32),
                        pltpu.VMEM((s // tq, nchain, 1, tg), F32),
                        pltpu.VMEM((s // tq, nchain, VROWS, tg), F32)],
        compiler_params=pltpu.CompilerParams(
            dimension_semantics=("arbitrary", "arbitrary"),
            vmem_limit_bytes=VMEM_LIMIT),
        name="diffattn",
    )(slopes, lq1, lk1, lq2, lk2, subg, bqt, bk, bvt, sg)


def kernel(x, norm_g, w_in, sinks, lambda_q1, lambda_k1, lambda_q2, lambda_k2,
           subln_g, w_out, final_g):
    b, s, d = x.shape
    depth = norm_g.shape[0]
    diff_slopes = jnp.asarray(
        [2.0 ** (-8.0 * (h + 1) / DIFF_HEADS) for h in range(DIFF_HEADS)], F32)
    h3 = x
    for layer in range(depth):
        aqt, ak, avt, bqt, bk, bvt, sg = _inproj(
            h3, norm_g[layer].reshape(1, d), w_in[layer])
        mixed_b = _diff(diff_slopes,
                        lambda_q1[layer].reshape(1, HEAD_DIM), lambda_k1[layer].reshape(1, HEAD_DIM),
                        lambda_q2[layer].reshape(1, HEAD_DIM), lambda_k2[layer].reshape(1, HEAD_DIM),
                        subln_g[layer].reshape(LANES, 1),
                        bqt, bk, bvt, sg, _lambda_init(layer))
        h3 = _swa_out(sinks[layer], aqt, ak, avt, sg, h3, mixed_b,
                      w_out[layer], final_g.reshape(1, d),
                      final_norm=(layer == depth - 1))
    return h3
```

```python
import functools
import math

import jax
import jax.numpy as jnp
import numpy as np
from jax import lax
from jax.experimental import pallas as pl
from jax.experimental.pallas import tpu as pltpu

F32 = jnp.float32
BF16 = jnp.bfloat16

HEAD_DIM = 64
LANES = 128
SWA_Q_HEADS = 8
SWA_KV_HEADS = 2
WINDOW = 128
DIFF_HEADS = 4
RMS_EPS = 1e-6
SUBLN_EPS = 1e-5
NEG_BIG = -1e30
LOG2E = math.log2(math.e)

A_Q = SWA_Q_HEADS * HEAD_DIM
A_KV = SWA_KV_HEADS * HEAD_DIM
B_W = DIFF_HEADS * 2 * HEAD_DIM
MIX = A_Q + B_W
OFF_AK = A_Q
OFF_AV = OFF_AK + A_KV
OFF_BQ = OFF_AV + A_KV
OFF_BK = OFF_BQ + B_W
OFF_BV = OFF_BK + B_W
OFF_GATE = OFF_BV + B_W
IN_COLS = OFF_GATE + MIX

TM_PROJ = 1024
ROWS_IN = 256
CAST_COLS = 256
ROWS_OUT = 256
COLS_OUT = 256
TQ_SWA = 1024
TG_DIFF = 256
NG_DIFF = 4
TQ_DIFF = NG_DIFF * TG_DIFF
TK_DIFF = 512
BF16_ROWS = 16
VROWS = LANES + BF16_ROWS
VROWS_A = HEAD_DIM + BF16_ROWS
VMEM_LIMIT = 56 * 1024 * 1024


def _lambda_init(layer_idx):
    return 0.8 - 0.6 * math.exp(-0.3 * layer_idx)


def _cast_weight_once(src_ref, dst_ref):
    @pl.when((pl.program_id(0) == 0) & (pl.program_id(1) == 0))
    def _():
        for lo in range(0, src_ref.shape[1], CAST_COLS):
            dst_ref[:, lo:lo + CAST_COLS] = src_ref[:, lo:lo + CAST_COLS].astype(BF16)


def _inproj_kernel(x_ref, g_ref, wf_ref,
                   aqt_ref, ak_ref, avt_ref, bqt_ref, bk_ref, bvt_ref, sg_ref, w_ref):
    _cast_weight_once(wf_ref, w_ref)
    qscale = HEAD_DIM ** -0.5 * LOG2E
    th = ROWS_IN
    for r in range(0, TM_PROJ, th):
        rows = slice(r, r + th)
        x = x_ref[0, rows, :]
        ms = jnp.mean(x * x, axis=-1, keepdims=True)
        h = (x * lax.rsqrt(ms + RMS_EPS) * g_ref[...]).astype(BF16)

        def tok(lo, hi, h=h):
            return jnp.dot(h, w_ref[:, lo:hi], preferred_element_type=F32)

        def feat(lo, hi, h=h):
            return lax.dot_general(w_ref[:, lo:hi], h, (((0,), (1,)), ((), ())),
                                   preferred_element_type=F32)

        ak_ref[0, rows, :] = tok(OFF_AK, OFF_AV).astype(BF16)
        bk_ref[0, rows, :] = tok(OFF_BK, OFF_BV).astype(BF16)
        gate = tok(OFF_GATE, IN_COLS)
        sg_ref[0, rows, :] = (gate * (1.0 / (1.0 + jnp.exp(-gate)))).astype(BF16)

        aqt_ref[0, :, rows] = (feat(0, OFF_AK) * qscale).astype(BF16)
        avt = feat(OFF_AV, OFF_BQ).astype(BF16)
        for g in range(SWA_KV_HEADS):
            avt_ref[0, g * VROWS_A:g * VROWS_A + HEAD_DIM, rows] = avt[g * HEAD_DIM:(g + 1) * HEAD_DIM]
            avt_ref[0, g * VROWS_A + HEAD_DIM:(g + 1) * VROWS_A, rows] = jnp.ones(
                (VROWS_A - HEAD_DIM, th), BF16)
        bqt_ref[0, r // TQ_DIFF, :, r % TQ_DIFF:r % TQ_DIFF + th] = (
            feat(OFF_BQ, OFF_BK) * qscale).astype(BF16)
        bvt = feat(OFF_BV, OFF_GATE).astype(BF16)
        vcols = slice(r % TK_DIFF, r % TK_DIFF + th)
        for hd in range(DIFF_HEADS):
            bvt_ref[0, r // TK_DIFF, hd * VROWS:hd * VROWS + LANES, vcols] = bvt[hd * LANES:(hd + 1) * LANES]
            bvt_ref[0, r // TK_DIFF, hd * VROWS + LANES:(hd + 1) * VROWS, vcols] = jnp.ones(
                (VROWS - LANES, th), BF16)


def _inproj(x, g, w):
    b, s, d = x.shape
    tm = TM_PROJ
    row = lambda bi, i: (bi, i, 0)
    col = lambda bi, i: (bi, 0, i)
    whole = lambda bi, i: (0, 0)
    bsd = lambda wd: jax.ShapeDtypeStruct((b, s, wd), BF16)
    bds = lambda wd: jax.ShapeDtypeStruct((b, wd, s), BF16)
    return pl.pallas_call(
        _inproj_kernel,
        grid=(b, s // tm),
        in_specs=[pl.BlockSpec((1, tm, d), row),
                  pl.BlockSpec((1, d), whole),
                  pl.BlockSpec(w.shape, whole)],
        out_specs=[pl.BlockSpec((1, A_Q, tm), col),
                   pl.BlockSpec((1, tm, A_KV), row),
                   pl.BlockSpec((1, SWA_KV_HEADS * VROWS_A, tm), col),
                   pl.BlockSpec((1, tm // TQ_DIFF, B_W, TQ_DIFF), lambda bi, i: (bi, i, 0, 0)),
                   pl.BlockSpec((1, tm, B_W), row),
                   pl.BlockSpec((1, tm // TK_DIFF, DIFF_HEADS * VROWS, TK_DIFF), lambda bi, i: (bi, i, 0, 0)),
                   pl.BlockSpec((1, tm, MIX), row)],
        out_shape=[bds(A_Q), bsd(A_KV), bds(SWA_KV_HEADS * VROWS_A),
                   jax.ShapeDtypeStruct((b, s // TQ_DIFF, B_W, TQ_DIFF), BF16), bsd(B_W),
                   jax.ShapeDtypeStruct((b, s // TK_DIFF, DIFF_HEADS * VROWS, TK_DIFF), BF16),
                   bsd(MIX)],
        scratch_shapes=[pltpu.VMEM(w.shape, BF16)],
        compiler_params=pltpu.CompilerParams(
            dimension_semantics=("arbitrary", "arbitrary"), vmem_limit_bytes=VMEM_LIMIT),
        name="inproj",
    )(x, g, w)


def _swa_out_kernel(sinks_ref, qt_ref, kp_ref, kc_ref, vtp_ref, vtc_ref, sg_ref,
                    x_ref, mb_ref, woutf_ref, fg_ref, o_ref, kcat, vtcat, ma_scr, wout_ref,
                    *, final_norm):
    w_ = WINDOW
    i = pl.program_id(1)
    group = SWA_Q_HEADS // SWA_KV_HEADS
    nq = group * w_
    _cast_weight_once(woutf_ref, wout_ref)
    kcat[0:w_, :] = kp_ref[0]
    kcat[w_:, :] = kc_ref[0]
    vtcat[:, 0:w_] = vtp_ref[0]
    vtcat[:, w_:] = vtc_ref[0]

    kidx = lax.broadcasted_iota(jnp.int32, (2 * w_, nq), 0)
    qidx = lax.broadcasted_iota(jnp.int32, (2 * w_, nq), 1) & (w_ - 1)
    in_cur = kidx >= w_
    band = (in_cur & (kidx - w_ <= qidx)) | ((kidx < w_) & (kidx > qidx))
    first_band = band & ((i > 0) | in_cur)

    lane = lax.broadcasted_iota(jnp.int32, (2 * w_, LANES), 1)
    kpos = lax.broadcasted_iota(jnp.int32, (2 * w_, LANES), 0).astype(F32)
    zk = jnp.zeros((2 * w_, LANES), F32)
    kmask = (lane < HEAD_DIM, lane >= HEAD_DIM)
    kpos_lanes = (jnp.where((lane == HEAD_DIM) | (lane == HEAD_DIM + 1), kpos, zk).astype(BF16),
                  jnp.where(lane < 2, kpos, zk).astype(BF16))

    hrow = lax.broadcasted_iota(jnp.int32, (HEAD_DIM, nq), 0)
    hcol = lax.broadcasted_iota(jnp.int32, (HEAD_DIM, nq), 1)

    def slope_rows(g):
        out = jnp.zeros((HEAD_DIM, nq), F32)
        for u in range(group):
            sl = 2.0 ** (-8.0 * (g * group + u + 1) / SWA_Q_HEADS) * LOG2E
            hi = float(np.asarray(sl, dtype=BF16).astype(np.float32))
            lo = sl - hi
            in_head = (hcol >= u * w_) & (hcol < (u + 1) * w_)
            out = jnp.where(in_head & (hrow == 0), hi, jnp.where(in_head & (hrow == 1), lo, out))
        return out.astype(BF16)

    srows = [slope_rows(g) for g in range(SWA_KV_HEADS)]

    qpos = (lax.broadcasted_iota(jnp.int32, (1, w_), 1) + w_).astype(F32)

    def sink_row(g):
        return jnp.concatenate(
            [(sinks_ref[g * group + u] + 2.0 ** (-8.0 * (g * group + u + 1) / SWA_Q_HEADS) * qpos) * LOG2E
             for u in range(group)], axis=1)

    sinkv = [sink_row(g) for g in range(SWA_KV_HEADS)]

    def scores(w, g):
        keys = kcat[w * w_:(w + 2) * w_, :]
        ka = jnp.where(kmask[g], keys, kpos_lanes[g])
        qh = jnp.concatenate(
            [qt_ref[0, (g * group + u) * HEAD_DIM:(g * group + u + 1) * HEAD_DIM, w * w_:(w + 1) * w_]
             for u in range(group)], axis=1)
        wq = jnp.concatenate([qh, srows[g]] if g == 0 else [srows[g], qh], axis=0)
        return jnp.dot(ka, wq, preferred_element_type=F32)

    def consume(w, g, s):
        s = jnp.where(first_band if w == 0 else band, s, NEG_BIG)
        m = jnp.maximum(jnp.max(s, axis=0, keepdims=True), sinkv[g])
        p = jnp.exp2(s - m).astype(BF16)
        vt = vtcat[g * VROWS_A:(g + 1) * VROWS_A, w * w_:(w + 2) * w_]
        o = jnp.dot(vt, p, preferred_element_type=F32)
        denom = o[HEAD_DIM:HEAD_DIM + 1] + jnp.exp2(sinkv[g] - m)
        ot = o[0:HEAD_DIM] / denom
        rows = slice(w * w_, (w + 1) * w_)
        for pair in range(group // 2):
            cols = slice((g * (group // 2) + pair) * LANES, (g * (group // 2) + pair + 1) * LANES)
            two = jnp.concatenate([ot[:, (2 * pair) * w_:(2 * pair + 1) * w_],
                                   ot[:, (2 * pair + 1) * w_:(2 * pair + 2) * w_]], axis=0)
            ma_scr[rows, cols] = (two.T * sg_ref[0, rows, cols].astype(F32)).astype(BF16)

    npiece = MIX // COLS_OUT

    def project(k, j):
        rows = slice(k * ROWS_OUT, (k + 1) * ROWS_OUT)
        cols = slice(j * COLS_OUT, (j + 1) * COLS_OUT)
        o_ref[0, rows, cols] = (
            x_ref[0, rows, cols]
            + jnp.dot(ma_scr[rows, :], wout_ref[0:A_Q, cols], preferred_element_type=F32)
            + jnp.dot(mb_ref[0, rows, :], wout_ref[A_Q:MIX, cols], preferred_element_type=F32))
        if final_norm and j == npiece - 1:
            y = o_ref[0, rows, :]
            ms = jnp.mean(y * y, axis=-1, keepdims=True)
            o_ref[0, rows, :] = y * lax.rsqrt(ms + RMS_EPS) * fg_ref[...]

    chains = [(w, g) for w in range(TQ_SWA // w_) for g in range(SWA_KV_HEADS)]
    chains_per_chunk = (ROWS_OUT // w_) * SWA_KV_HEADS
    ahead = 2
    pending = [scores(*ch) for ch in chains[:ahead]]
    ready = []
    for n, (w, g) in enumerate(chains):
        if n + ahead < len(chains):
            pending.append(scores(*chains[n + ahead]))
        consume(w, g, pending.pop(0))
        if (n + 1) % chains_per_chunk == 0:
            ready += [(n // chains_per_chunk, j) for j in range(npiece)]
        if ready:
            project(*ready.pop(0))
    for unit in ready:
        project(*unit)


def _swa_out(sinks, aqt, ak, avt, sg, x, mixed_b, wout, fg, final_norm):
    b, s, d = x.shape
    tq = TQ_SWA
    wpt = tq // WINDOW
    vr = SWA_KV_HEADS * VROWS_A
    prev_w = lambda i: jnp.maximum(i * wpt - 1, 0)
    tile = lambda bi, i: (bi, i, 0)
    whole = lambda bi, i: (0, 0)
    return pl.pallas_call(
        functools.partial(_swa_out_kernel, final_norm=final_norm),
        grid=(b, s // tq),
        in_specs=[pl.BlockSpec(memory_space=pltpu.SMEM),
                  pl.BlockSpec((1, A_Q, tq), lambda bi, i: (bi, 0, i)),
                  pl.BlockSpec((1, WINDOW, A_KV), lambda bi, i: (bi, prev_w(i), 0)),
                  pl.BlockSpec((1, tq, A_KV), tile),
                  pl.BlockSpec((1, vr, WINDOW), lambda bi, i: (bi, 0, prev_w(i))),
                  pl.BlockSpec((1, vr, tq), lambda bi, i: (bi, 0, i)),
                  pl.BlockSpec((1, tq, A_Q), tile),
                  pl.BlockSpec((1, tq, d), tile),
                  pl.BlockSpec((1, tq, B_W), tile),
                  pl.BlockSpec((MIX, d), whole),
                  pl.BlockSpec((1, d), whole)],
        out_specs=pl.BlockSpec((1, tq, d), tile),
        out_shape=jax.ShapeDtypeStruct((b, s, d), F32),
        scratch_shapes=[pltpu.VMEM((tq + WINDOW, A_KV), BF16),
                        pltpu.VMEM((vr, tq + WINDOW), BF16),
                        pltpu.VMEM((tq, A_Q), BF16),
                        pltpu.VMEM((MIX, d), BF16)],
        compiler_params=pltpu.CompilerParams(
            dimension_semantics=("arbitrary", "arbitrary"), vmem_limit_bytes=VMEM_LIMIT),
        name="swa_out",
    )(sinks, aqt, ak, ak, avt, avt, sg, x, mixed_b, wout, fg)


def _diff_kernel(slopes_ref, lq1_ref, lk1_ref, lq2_ref, lk2_ref, subg_ref,
                 qt_ref, k_ref, vt_ref, sg_ref, o_ref,
                 qa_ref, s_ref, m_ref, acc_ref, *, lambda_init):
    tg, tk, tq = TG_DIFF, TK_DIFF, TQ_DIFF
    nchain = 2 * NG_DIFF
    ntile = qt_ref.shape[1]
    hd = pl.program_id(1)
    slope = slopes_ref[hd] * LOG2E

    def key_consts(n):
        kk = lax.broadcasted_iota(jnp.int32, (n, LANES), 0).astype(F32)
        lane = lax.broadcasted_iota(jnp.int32, (n, LANES), 1)
        ab = slope * kk
        ab_hi = ab.astype(BF16).astype(F32)
        ab_lo = ab - ab_hi
        zk = jnp.zeros((n, LANES), F32)
        bias1 = jnp.where(lane == HEAD_DIM, ab_hi,
                          jnp.where(lane == HEAD_DIM + 1, ab_lo, zk)).astype(BF16)
        bias2 = jnp.where(lane == 0, ab_hi, jnp.where(lane == 1, ab_lo, zk)).astype(BF16)
        return lane < HEAD_DIM, bias1, bias2

    lam = (jnp.exp(jnp.sum(lq1_ref[...] * lk1_ref[...], axis=-1, keepdims=True))
           - jnp.exp(jnp.sum(lq2_ref[...] * lk2_ref[...], axis=-1, keepdims=True))
           + lambda_init)

    def build_queries(t):
        rowq = lax.broadcasted_iota(jnp.int32, (LANES, tg), 0)
        one = jnp.ones((LANES, tg), F32)
        zq = jnp.zeros((LANES, tg), F32)
        for g in range(NG_DIFF):
            qt = qt_ref[0, t, :, g * tg:(g + 1) * tg].astype(F32)
            qa_ref[t, 2 * g] = jnp.where(
                rowq < HEAD_DIM, qt, jnp.where(rowq < HEAD_DIM + 2, one, zq)).astype(BF16)
            qa_ref[t, 2 * g + 1] = jnp.where(
                rowq >= HEAD_DIM, qt, jnp.where(rowq < 2, one, zq)).astype(BF16)

    def tile(t):
        first_tile = t == 0
        slot = t

        def reset(c):
            m_ref[slot, c] = jnp.full(m_ref.shape[2:], NEG_BIG, F32)
            acc_ref[slot, c] = jnp.zeros(acc_ref.shape[2:], F32)

        def scores(buf, c, ka, n):
            s_ref[buf, c, 0:n, :] = jnp.dot(ka, qa_ref[slot, c], preferred_element_type=F32)

        kconst = {n: key_consts(n) for n in ((tg,) if first_tile else (tg, tk))}
        krow = lax.broadcasted_iota(jnp.int32, (tg, tg), 0)
        qcol = lax.broadcasted_iota(jnp.int32, (tg, tg), 1)
        causal = krow <= qcol

        def consume(buf, c, vt, cj, n, masked):
            s = s_ref[buf, c, 0:n, :]
            if masked:
                s = jnp.where(causal, s, NEG_BIG)
            m_old = m_ref[slot, c]
            m_new = jnp.maximum(m_old, jnp.max(s, axis=0, keepdims=True) + cj)
            alpha = jnp.exp2(m_old - m_new)
            p = jnp.exp2(s - (m_new - cj)).astype(BF16)
            acc_ref[slot, c] = alpha * acc_ref[slot, c] + jnp.dot(vt, p, preferred_element_type=F32)
            m_ref[slot, c] = m_new

        def keys(start, n):
            k = k_ref[0, start:start + n, :]
            lane_lo, bias1, bias2 = kconst[n]
            return jnp.where(lane_lo, k, bias1), jnp.where(lane_lo, bias2, k)

        def block_bias(start):
            return slope * float(start)

        def finish(g):
            a1 = acc_ref[slot, 2 * g]
            a2 = acc_ref[slot, 2 * g + 1]
            ot = (a1[0:LANES] / a1[LANES:LANES + 1]
                  - lam * (a2[0:LANES] / a2[LANES:LANES + 1]))
            ot = ot * lax.rsqrt(jnp.mean(ot * ot, axis=0, keepdims=True) + SUBLN_EPS) * subg_ref[...]
            ot = ot * (1.0 - lambda_init)
            rows = slice(t * tq + g * tg, t * tq + (g + 1) * tg)
            o_ref[0, rows, :] = (ot.T * sg_ref[0, rows, :].astype(F32)).astype(BF16)

        base = t * tq
        build_queries(t)
        ka = keys(base, tg)
        for c in range(nchain):
            reset(c)
            scores(0, c, ka[c % 2], tg)
        ka_first = None if first_tile else keys(0, tk)
        for jj in range(NG_DIFF):
            cur, nxt = jj % 2, 1 - jj % 2
            start = base + jj * tg
            vt = vt_ref[0, t * (tq // tk) + (jj * tg) // tk, :,
                        (jj * tg) % tk:(jj * tg) % tk + tg]
            cj = block_bias(start)
            ka = keys(start + tg, tg) if jj + 1 < NG_DIFF else None
            for g in range(jj, NG_DIFF):
                for c in (2 * g, 2 * g + 1):
                    if g > jj:
                        scores(nxt, c, ka[c % 2], tg)
                    consume(cur, c, vt, cj, tg, g == jj)
                    if g == jj and not first_tile:
                        scores(0, c, ka_first[c % 2], tk)
                if g == jj and first_tile:
                    finish(g)
        nblk = base // tk
        for j in range(nblk):
            cur = j % 2
            last = j == nblk - 1
            vt = vt_ref[0, j]
            cj = block_bias(j * tk)
            kan = None if last else keys((j + 1) * tk, tk)
            for g in range(NG_DIFF):
                for c in (2 * g, 2 * g + 1):
                    if not last:
                        scores(1 - cur, c, kan[c % 2], tk)
                    consume(cur, c, vt, cj, tk, False)
                if last:
                    finish(g)

    arms = [list(range(min(3, ntile)))] + [[n] for n in range(3, ntile)]

    def run(tiles):
        for n in tiles:
            tile(n)

    def arm_body(a, carry):
        for n, tiles in enumerate(arms):
            pl.when(a == n)(functools.partial(run, tiles))
        return carry

    lax.fori_loop(0, len(arms), arm_body, 0)


def _diff(slopes, lq1, lk1, lq2, lk2, subg, bqt, bk, bvt, sg, lambda_init):
    b, s, _ = bk.shape
    tg, tk, tq = TG_DIFF, TK_DIFF, TQ_DIFF
    nchain = 2 * NG_DIFF
    smem = pl.BlockSpec(memory_space=pltpu.SMEM)
    small = lambda shape: pl.BlockSpec(shape, lambda bi, h: (0, 0))
    return pl.pallas_call(
        functools.partial(_diff_kernel, lambda_init=lambda_init),
        grid=(b, DIFF_HEADS),
        in_specs=[smem,
                  small((1, HEAD_DIM)), small((1, HEAD_DIM)),
                  small((1, HEAD_DIM)), small((1, HEAD_DIM)),
                  small((LANES, 1)),
                  pl.BlockSpec((1, s // tq, LANES, tq), lambda bi, h: (bi, 0, h, 0)),
                  pl.BlockSpec((1, s, LANES), lambda bi, h: (bi, 0, h)),
                  pl.BlockSpec((1, s // tk, VROWS, tk), lambda bi, h: (bi, 0, h, 0)),
                  pl.BlockSpec((1, s, LANES), lambda bi, h: (bi, 0, A_Q // LANES + h))],
        out_specs=pl.BlockSpec((1, s, LANES), lambda bi, h: (bi, 0, h)),
        out_shape=jax.ShapeDtypeStruct((b, s, B_W), BF16),
        scratch_shapes=[pltpu.VMEM((s // tq, nchain, LANES, tg), BF16),
                        pltpu.VMEM((2, nchain, tk, tg), F32),
                        pltpu.VMEM((s // tq, nchain, 1, tg), F32),
                        pltpu.VMEM((s // tq, nchain, VROWS, tg), F32)],
        compiler_params=pltpu.CompilerParams(
            dimension_semantics=("arbitrary", "arbitrary"),
            vmem_limit_bytes=VMEM_LIMIT),
        name="diffattn",
    )(slopes, lq1, lk1, lq2, lk2, subg, bqt, bk, bvt, sg)


def kernel(x, norm_g, w_in, sinks, lambda_q1, lambda_k1, lambda_q2, lambda_k2,
           subln_g, w_out, final_g):
    b, s, d = x.shape
    depth = norm_g.shape[0]
    diff_slopes = jnp.asarray(
        [2.0 ** (-8.0 * (h + 1) / DIFF_HEADS) for h in range(DIFF_HEADS)], F32)
    h3 = x
    for layer in range(depth):
        aqt, ak, avt, bqt, bk, bvt, sg = _inproj(
            h3, norm_g[layer].reshape(1, d), w_in[layer])
        mixed_b = _diff(diff_slopes,
                        lambda_q1[layer].reshape(1, HEAD_DIM), lambda_k1[layer].reshape(1, HEAD_DIM),
                        lambda_q2[layer].reshape(1, HEAD_DIM), lambda_k2[layer].reshape(1, HEAD_DIM),
                        subln_g[layer].reshape(LANES, 1),
                        bqt, bk, bvt, sg, _lambda_init(layer))
        h3 = _swa_out(sinks[layer], aqt, ak, avt, sg, h3, mixed_b,
                      w_out[layer], final_g.reshape(1, d),
                      final_norm=(layer == depth - 1))
    return h3
```

```python
import functools
import math

import jax
import jax.numpy as jnp
import numpy as np
from jax import lax
from jax.experimental import pallas as pl
from jax.experimental.pallas import tpu as pltpu

F32 = jnp.float32
BF16 = jnp.bfloat16

HEAD_DIM = 64
LANES = 128
SWA_Q_HEADS = 8
SWA_KV_HEADS = 2
WINDOW = 128
DIFF_HEADS = 4
RMS_EPS = 1e-6
SUBLN_EPS = 1e-5
NEG_BIG = -1e30
LOG2E = math.log2(math.e)

A_Q = SWA_Q_HEADS * HEAD_DIM
A_KV = SWA_KV_HEADS * HEAD_DIM
B_W = DIFF_HEADS * 2 * HEAD_DIM
MIX = A_Q + B_W
OFF_AK = A_Q
OFF_AV = OFF_AK + A_KV
OFF_BQ = OFF_AV + A_KV
OFF_BK = OFF_BQ + B_W
OFF_BV = OFF_BK + B_W
OFF_GATE = OFF_BV + B_W
IN_COLS = OFF_GATE + MIX

TM_PROJ = 1024
ROWS_IN = 256
CAST_COLS = 256
ROWS_OUT = 256
COLS_OUT = 256
TQ_SWA = 1024
TG_DIFF = 256
NG_DIFF = 4
TQ_DIFF = NG_DIFF * TG_DIFF
TK_DIFF = 512
BF16_ROWS = 16
VROWS = LANES + BF16_ROWS
VROWS_A = HEAD_DIM + BF16_ROWS
VMEM_LIMIT = 56 * 1024 * 1024


def _lambda_init(layer_idx):
    return 0.8 - 0.6 * math.exp(-0.3 * layer_idx)


def _cast_weight_once(src_ref, dst_ref):
    @pl.when((pl.program_id(0) == 0) & (pl.program_id(1) == 0))
    def _():
        for lo in range(0, src_ref.shape[1], CAST_COLS):
            dst_ref[:, lo:lo + CAST_COLS] = src_ref[:, lo:lo + CAST_COLS].astype(BF16)


def _inproj_kernel(x_ref, g_ref, wf_ref,
                   aqt_ref, ak_ref, avt_ref, bqt_ref, bk_ref, bvt_ref, sg_ref, w_ref):
    _cast_weight_once(wf_ref, w_ref)
    qscale = HEAD_DIM ** -0.5 * LOG2E
    th = ROWS_IN
    for r in range(0, TM_PROJ, th):
        rows = slice(r, r + th)
        x = x_ref[0, rows, :]
        ms = jnp.mean(x * x, axis=-1, keepdims=True)
        h = (x * lax.rsqrt(ms + RMS_EPS) * g_ref[...]).astype(BF16)

        def tok(lo, hi, h=h):
            return jnp.dot(h, w_ref[:, lo:hi], preferred_element_type=F32)

        def feat(lo, hi, h=h):
            return lax.dot_general(w_ref[:, lo:hi], h, (((0,), (1,)), ((), ())),
                                   preferred_element_type=F32)

        ak_ref[0, rows, :] = tok(OFF_AK, OFF_AV).astype(BF16)
        bk_ref[0, rows, :] = tok(OFF_BK, OFF_BV).astype(BF16)
        gate = tok(OFF_GATE, IN_COLS)
        sg_ref[0, rows, :] = (gate * (1.0 / (1.0 + jnp.exp(-gate)))).astype(BF16)

        aqt_ref[0, :, rows] = (feat(0, OFF_AK) * qscale).astype(BF16)
        avt = feat(OFF_AV, OFF_BQ).astype(BF16)
        for g in range(SWA_KV_HEADS):
            avt_ref[0, g * VROWS_A:g * VROWS_A + HEAD_DIM, rows] = avt[g * HEAD_DIM:(g + 1) * HEAD_DIM]
            avt_ref[0, g * VROWS_A + HEAD_DIM:(g + 1) * VROWS_A, rows] = jnp.ones(
                (VROWS_A - HEAD_DIM, th), BF16)
        bqt_ref[0, r // TQ_DIFF, :, r % TQ_DIFF:r % TQ_DIFF + th] = (
            feat(OFF_BQ, OFF_BK) * qscale).astype(BF16)
        bvt = feat(OFF_BV, OFF_GATE).astype(BF16)
        vcols = slice(r % TK_DIFF, r % TK_DIFF + th)
        for hd in range(DIFF_HEADS):
            bvt_ref[0, r // TK_DIFF, hd * VROWS:hd * VROWS + LANES, vcols] = bvt[hd * LANES:(hd + 1) * LANES]
            bvt_ref[0, r // TK_DIFF, hd * VROWS + LANES:(hd + 1) * VROWS, vcols] = jnp.ones(
                (VROWS - LANES, th), BF16)


def _inproj(x, g, w):
    b, s, d = x.shape
    tm = TM_PROJ
    row = lambda bi, i: (bi, i, 0)
    col = lambda bi, i: (bi, 0, i)
    whole = lambda bi, i: (0, 0)
    bsd = lambda wd: jax.ShapeDtypeStruct((b, s, wd), BF16)
    bds = lambda wd: jax.ShapeDtypeStruct((b, wd, s), BF16)
    return pl.pallas_call(
        _inproj_kernel,
        grid=(b, s // tm),
        in_specs=[pl.BlockSpec((1, tm, d), row),
                  pl.BlockSpec((1, d), whole),
                  pl.BlockSpec(w.shape, whole)],
        out_specs=[pl.BlockSpec((1, A_Q, tm), col),
                   pl.BlockSpec((1, tm, A_KV), row),
                   pl.BlockSpec((1, SWA_KV_HEADS * VROWS_A, tm), col),
                   pl.BlockSpec((1, tm // TQ_DIFF, B_W, TQ_DIFF), lambda bi, i: (bi, i, 0, 0)),
                   pl.BlockSpec((1, tm, B_W), row),
                   pl.BlockSpec((1, tm // TK_DIFF, DIFF_HEADS * VROWS, TK_DIFF), lambda bi, i: (bi, i, 0, 0)),
                   pl.BlockSpec((1, tm, MIX), row)],
        out_shape=[bds(A_Q), bsd(A_KV), bds(SWA_KV_HEADS * VROWS_A),
                   jax.ShapeDtypeStruct((b, s // TQ_DIFF, B_W, TQ_DIFF), BF16), bsd(B_W),
                   jax.ShapeDtypeStruct((b, s // TK_DIFF, DIFF_HEADS * VROWS, TK_DIFF), BF16),
                   bsd(MIX)],
        scratch_shapes=[pltpu.VMEM(w.shape, BF16)],
        compiler_params=pltpu.CompilerParams(
            dimension_semantics=("arbitrary", "arbitrary"), vmem_limit_bytes=VMEM_LIMIT),
        name="inproj",
    )(x, g, w)


def _swa_out_kernel(sinks_ref, qt_ref, kp_ref, kc_ref, vtp_ref, vtc_ref, sg_ref,
                    x_ref, mb_ref, woutf_ref, fg_ref, o_ref, kcat, vtcat, ma_scr, wout_ref,
                    *, final_norm):
    w_ = WINDOW
    i = pl.program_id(1)
    group = SWA_Q_HEADS // SWA_KV_HEADS
    nq = group * w_
    _cast_weight_once(woutf_ref, wout_ref)
    kcat[0:w_, :] = kp_ref[0]
    kcat[w_:, :] = kc_ref[0]
    vtcat[:, 0:w_] = vtp_ref[0]
    vtcat[:, w_:] = vtc_ref[0]

    kidx = lax.broadcasted_iota(jnp.int32, (2 * w_, nq), 0)
    qidx = lax.broadcasted_iota(jnp.int32, (2 * w_, nq), 1) & (w_ - 1)
    in_cur = kidx >= w_
    band = (in_cur & (kidx - w_ <= qidx)) | ((kidx < w_) & (kidx > qidx))
    first_band = band & ((i > 0) | in_cur)

    lane = lax.broadcasted_iota(jnp.int32, (2 * w_, LANES), 1)
    kpos = lax.broadcasted_iota(jnp.int32, (2 * w_, LANES), 0).astype(F32)
    zk = jnp.zeros((2 * w_, LANES), F32)
    kmask = (lane < HEAD_DIM, lane >= HEAD_DIM)
    kpos_lanes = (jnp.where((lane == HEAD_DIM) | (lane == HEAD_DIM + 1), kpos, zk).astype(BF16),
                  jnp.where(lane < 2, kpos, zk).astype(BF16))

    hrow = lax.broadcasted_iota(jnp.int32, (HEAD_DIM, nq), 0)
    hcol = lax.broadcasted_iota(jnp.int32, (HEAD_DIM, nq), 1)

    def slope_rows(g):
        out = jnp.zeros((HEAD_DIM, nq), F32)
        for u in range(group):
            sl = 2.0 ** (-8.0 * (g * group + u + 1) / SWA_Q_HEADS) * LOG2E
            hi = float(np.asarray(sl, dtype=BF16).astype(np.float32))
            lo = sl - hi
            in_head = (hcol >= u * w_) & (hcol < (u + 1) * w_)
            out = jnp.where(in_head & (hrow == 0), hi, jnp.where(in_head & (hrow == 1), lo, out))
        return out.astype(BF16)

    srows = [slope_rows(g) for g in range(SWA_KV_HEADS)]

    qpos = (lax.broadcasted_iota(jnp.int32, (1, w_), 1) + w_).astype(F32)

    def sink_row(g):
        return jnp.concatenate(
            [(sinks_ref[g * group + u] + 2.0 ** (-8.0 * (g * group + u + 1) / SWA_Q_HEADS) * qpos) * LOG2E
             for u in range(group)], axis=1)

    sinkv = [sink_row(g) for g in range(SWA_KV_HEADS)]

    def scores(w, g):
        keys = kcat[w * w_:(w + 2) * w_, :]
        ka = jnp.where(kmask[g], keys, kpos_lanes[g])
        qh = jnp.concatenate(
            [qt_ref[0, (g * group + u) * HEAD_DIM:(g * group + u + 1) * HEAD_DIM, w * w_:(w + 1) * w_]
             for u in range(group)], axis=1)
        wq = jnp.concatenate([qh, srows[g]] if g == 0 else [srows[g], qh], axis=0)
        return jnp.dot(ka, wq, preferred_element_type=F32)

    def consume(w, g, s):
        s = jnp.where(first_band if w == 0 else band, s, NEG_BIG)
        m = jnp.maximum(jnp.max(s, axis=0, keepdims=True), sinkv[g])
        p = jnp.exp2(s - m).astype(BF16)
        vt = vtcat[g * VROWS_A:(g + 1) * VROWS_A, w * w_:(w + 2) * w_]
        o = jnp.dot(vt, p, preferred_element_type=F32)
        denom = o[HEAD_DIM:HEAD_DIM + 1] + jnp.exp2(sinkv[g] - m)
        ot = o[0:HEAD_DIM] / denom
        rows = slice(w * w_, (w + 1) * w_)
        for pair in range(group // 2):
            cols = slice((g * (group // 2) + pair) * LANES, (g * (group // 2) + pair + 1) * LANES)
            two = jnp.concatenate([ot[:, (2 * pair) * w_:(2 * pair + 1) * w_],
                                   ot[:, (2 * pair + 1) * w_:(2 * pair + 2) * w_]], axis=0)
            ma_scr[rows, cols] = (two.T * sg_ref[0, rows, cols].astype(F32)).astype(BF16)

    npiece = MIX // COLS_OUT

    def project(k, j):
        rows = slice(k * ROWS_OUT, (k + 1) * ROWS_OUT)
        cols = slice(j * COLS_OUT, (j + 1) * COLS_OUT)
        o_ref[0, rows, cols] = (
            x_ref[0, rows, cols]
            + jnp.dot(ma_scr[rows, :], wout_ref[0:A_Q, cols], preferred_element_type=F32)
            + jnp.dot(mb_ref[0, rows, :], wout_ref[A_Q:MIX, cols], preferred_element_type=F32))
        if final_norm and j == npiece - 1:
            y = o_ref[0, rows, :]
            ms = jnp.mean(y * y, axis=-1, keepdims=True)
            o_ref[0, rows, :] = y * lax.rsqrt(ms + RMS_EPS) * fg_ref[...]

    chains = [(w, g) for w in range(TQ_SWA // w_) for g in range(SWA_KV_HEADS)]
    chains_per_chunk = (ROWS_OUT // w_) * SWA_KV_HEADS
    ahead = 2
    pending = [scores(*ch) for ch in chains[:ahead]]
    ready = []
    for n, (w, g) in enumerate(chains):
        if n + ahead < len(chains):
            pending.append(scores(*chains[n + ahead]))
        consume(w, g, pending.pop(0))
        if (n + 1) % chains_per_chunk == 0:
            ready += [(n // chains_per_chunk, j) for j in range(npiece)]
        if ready:
            project(*ready.pop(0))
    for unit in ready:
        project(*unit)


def _swa_out(sinks, aqt, ak, avt, sg, x, mixed_b, wout, fg, final_norm):
    b, s, d = x.shape
    tq = TQ_SWA
    wpt = tq // WINDOW
    vr = SWA_KV_HEADS * VROWS_A
    prev_w = lambda i: jnp.maximum(i * wpt - 1, 0)
    tile = lambda bi, i: (bi, i, 0)
    whole = lambda bi, i: (0, 0)
    return pl.pallas_call(
        functools.partial(_swa_out_kernel, final_norm=final_norm),
        grid=(b, s // tq),
        in_specs=[pl.BlockSpec(memory_space=pltpu.SMEM),
                  pl.BlockSpec((1, A_Q, tq), lambda bi, i: (bi, 0, i)),
                  pl.BlockSpec((1, WINDOW, A_KV), lambda bi, i: (bi, prev_w(i), 0)),
                  pl.BlockSpec((1, tq, A_KV), tile),
                  pl.BlockSpec((1, vr, WINDOW), lambda bi, i: (bi, 0, prev_w(i))),
                  pl.BlockSpec((1, vr, tq), lambda bi, i: (bi, 0, i)),
                  pl.BlockSpec((1, tq, A_Q), tile),
                  pl.BlockSpec((1, tq, d), tile),
                  pl.BlockSpec((1, tq, B_W), tile),
                  pl.BlockSpec((MIX, d), whole),
                  pl.BlockSpec((1, d), whole)],
        out_specs=pl.BlockSpec((1, tq, d), tile),
        out_shape=jax.ShapeDtypeStruct((b, s, d), F32),
        scratch_shapes=[pltpu.VMEM((tq + WINDOW, A_KV), BF16),
                        pltpu.VMEM((vr, tq + WINDOW), BF16),
                        pltpu.VMEM((tq, A_Q), BF16),
                        pltpu.VMEM((MIX, d), BF16)],
        compiler_params=pltpu.CompilerParams(
            dimension_semantics=("arbitrary", "arbitrary"), vmem_limit_bytes=VMEM_LIMIT),
        name="swa_out",
    )(sinks, aqt, ak, ak, avt, avt, sg, x, mixed_b, wout, fg)


def _diff_kernel(slopes_ref, lq1_ref, lk1_ref, lq2_ref, lk2_ref, subg_ref,
                 qt_ref, k_ref, vt_ref, sg_ref, o_ref,
                 qa_ref, s_ref, m_ref, acc_ref, *, lambda_init):
    tg, tk, tq = TG_DIFF, TK_DIFF, TQ_DIFF
    nchain = 2 * NG_DIFF
    ntile = qt_ref.shape[1]
    hd = pl.program_id(1)
    slope = slopes_ref[hd] * LOG2E

    def key_consts(n):
        kk = lax.broadcasted_iota(jnp.int32, (n, LANES), 0).astype(F32)
        lane = lax.broadcasted_iota(jnp.int32, (n, LANES), 1)
        ab = slope * kk
        ab_hi = ab.astype(BF16).astype(F32)
        ab_lo = ab - ab_hi
        zk = jnp.zeros((n, LANES), F32)
        bias1 = jnp.where(lane == HEAD_DIM, ab_hi,
                          jnp.where(lane == HEAD_DIM + 1, ab_lo, zk)).astype(BF16)
        bias2 = jnp.where(lane == 0, ab_hi, jnp.where(lane == 1, ab_lo, zk)).astype(BF16)
        return lane < HEAD_DIM, bias1, bias2

    lam = (jnp.exp(jnp.sum(lq1_ref[...] * lk1_ref[...], axis=-1, keepdims=True))
           - jnp.exp(jnp.sum(lq2_ref[...] * lk2_ref[...], axis=-1, keepdims=True))
           + lambda_init)

    def build_queries(t):
        rowq = lax.broadcasted_iota(jnp.int32, (LANES, tg), 0)
        one = jnp.ones((LANES, tg), F32)
        zq = jnp.zeros((LANES, tg), F32)
        for g in range(NG_DIFF):
            qt = qt_ref[0, t, :, g * tg:(g + 1) * tg].astype(F32)
            qa_ref[t, 2 * g] = jnp.where(
                rowq < HEAD_DIM, qt, jnp.where(rowq < HEAD_DIM + 2, one, zq)).astype(BF16)
            qa_ref[t, 2 * g + 1] = jnp.where(
                rowq >= HEAD_DIM, qt, jnp.where(rowq < 2, one, zq)).astype(BF16)

    def tile(t):
        first_tile = t == 0
        slot = t

        def reset(c):
            m_ref[slot, c] = jnp.full(m_ref.shape[2:], NEG_BIG, F32)
            acc_ref[slot, c] = jnp.zeros(acc_ref.shape[2:], F32)

        def scores(buf, c, ka, n):
            s_ref[buf, c, 0:n, :] = jnp.dot(ka, qa_ref[slot, c], preferred_element_type=F32)

        kconst = {n: key_consts(n) for n in ((tg,) if first_tile else (tg, tk))}
        krow = lax.broadcasted_iota(jnp.int32, (tg, tg), 0)
        qcol = lax.broadcasted_iota(jnp.int32, (tg, tg), 1)
        causal = krow <= qcol

        def consume(buf, c, vt, cj, n, masked):
            s = s_ref[buf, c, 0:n, :]
            if masked:
                s = jnp.where(causal, s, NEG_BIG)
            m_old = m_ref[slot, c]
            m_new = jnp.maximum(m_old, jnp.max(s, axis=0, keepdims=True) + cj)
            alpha = jnp.exp2(m_old - m_new)
            p = jnp.exp2(s - (m_new - cj)).astype(BF16)
            acc_ref[slot, c] = alpha * acc_ref[slot, c] + jnp.dot(vt, p, preferred_element_type=F32)
            m_ref[slot, c] = m_new

        def keys(start, n):
            k = k_ref[0, start:start + n, :]
            lane_lo, bias1, bias2 = kconst[n]
            return jnp.where(lane_lo, k, bias1), jnp.where(lane_lo, bias2, k)

        def block_bias(start):
            return slope * float(start)

        def finish(g):
            a1 = acc_ref[slot, 2 * g]
            a2 = acc_ref[slot, 2 * g + 1]
            ot = (a1[0:LANES] / a1[LANES:LANES + 1]
                  - lam * (a2[0:LANES] / a2[LANES:LANES + 1]))
            ot = ot * lax.rsqrt(jnp.mean(ot * ot, axis=0, keepdims=True) + SUBLN_EPS) * subg_ref[...]
            ot = ot * (1.0 - lambda_init)
            rows = slice(t * tq + g * tg, t * tq + (g + 1) * tg)
            o_ref[0, rows, :] = (ot.T * sg_ref[0, rows, :].astype(F32)).astype(BF16)

        base = t * tq
        build_queries(t)
        ka = keys(base, tg)
        for c in range(nchain):
            reset(c)
            scores(0, c, ka[c % 2], tg)
        ka_first = None if first_tile else keys(0, tk)
        for jj in range(NG_DIFF):
            cur, nxt = jj % 2, 1 - jj % 2
            start = base + jj * tg
            vt = vt_ref[0, t * (tq // tk) + (jj * tg) // tk, :,
                        (jj * tg) % tk:(jj * tg) % tk + tg]
            cj = block_bias(start)
            ka = keys(start + tg, tg) if jj + 1 < NG_DIFF else None
            for g in range(jj, NG_DIFF):
                for c in (2 * g, 2 * g + 1):
                    if g > jj:
                        scores(nxt, c, ka[c % 2], tg)
                    consume(cur, c, vt, cj, tg, g == jj)
                    if g == jj and not first_tile:
                        scores(0, c, ka_first[c % 2], tk)
                if g == jj and first_tile:
                    finish(g)
        nblk = base // tk
        for j in range(nblk):
            cur = j % 2
            last = j == nblk - 1
            vt = vt_ref[0, j]
            cj = block_bias(j * tk)
            kan = None if last else keys((j + 1) * tk, tk)
            for g in range(NG_DIFF):
                for c in (2 * g, 2 * g + 1):
                    if not last:
                        scores(1 - cur, c, kan[c % 2], tk)
                    consume(cur, c, vt, cj, tk, False)
                if last:
                    finish(g)

    arms = [list(range(n, min(n + 2, ntile))) for n in range(0, ntile, 2)]

    def run(tiles):
        for n in tiles:
            tile(n)

    def arm_body(a, carry):
        for n, tiles in enumerate(arms):
            pl.when(a == n)(functools.partial(run, tiles))
        return carry

    lax.fori_loop(0, len(arms), arm_body, 0)


def _diff(slopes, lq1, lk1, lq2, lk2, subg, bqt, bk, bvt, sg, lambda_init):
    b, s, _ = bk.shape
    tg, tk, tq = TG_DIFF, TK_DIFF, TQ_DIFF
    nchain = 2 * NG_DIFF
    smem = pl.BlockSpec(memory_space=pltpu.SMEM)
    small = lambda shape: pl.BlockSpec(shape, lambda bi, h: (0, 0))
    return pl.pallas_call(
        functools.partial(_diff_kernel, lambda_init=lambda_init),
        grid=(b, DIFF_HEADS),
        in_specs=[smem,
                  small((1, HEAD_DIM)), small((1, HEAD_DIM)),
                  small((1, HEAD_DIM)), small((1, HEAD_DIM)),
                  small((LANES, 1)),
                  pl.BlockSpec((1, s // tq, LANES, tq), lambda bi, h: (bi, 0, h, 0)),
                  pl.BlockSpec((1, s, LANES), lambda bi, h: (bi, 0, h)),
                  pl.BlockSpec((1, s // tk, VROWS, tk), lambda bi, h: (bi, 0, h, 0)),
                  pl.BlockSpec((1, s, LANES), lambda bi, h: (bi, 0, A_Q // LANES + h))],
        out_specs=pl.BlockSpec((1, s, LANES), lambda bi, h: (bi, 0, h)),
        out_shape=jax.ShapeDtypeStruct((b, s, B_W), BF16),
        scratch_shapes=[pltpu.VMEM((s // tq, nchain, LANES, tg), BF16),
                        pltpu.VMEM((2, nchain, tk, tg), F32),
                        pltpu.VMEM((s // tq, nchain, 1, tg), F32),
                        pltpu.VMEM((s // tq, nchain, VROWS, tg), F32)],
        compiler_params=pltpu.CompilerParams(
            dimension_semantics=("arbitrary", "arbitrary"),
            vmem_limit_bytes=VMEM_LIMIT),
        name="diffattn",
    )(slopes, lq1, lk1, lq2, lk2, subg, bqt, bk, bvt, sg)


def kernel(x, norm_g, w_in, sinks, lambda_q1, lambda_k1, lambda_q2, lambda_k2,
           subln_g, w_out, final_g):
    b, s, d = x.shape
    depth = norm_g.shape[0]
    diff_slopes = jnp.asarray(
        [2.0 ** (-8.0 * (h + 1) / DIFF_HEADS) for h in range(DIFF_HEADS)], F32)
    h3 = x
    for layer in range(depth):
        aqt, ak, avt, bqt, bk, bvt, sg = _inproj(
            h3, norm_g[layer].reshape(1, d), w_in[layer])
        mixed_b = _diff(diff_slopes,
                        lambda_q1[layer].reshape(1, HEAD_DIM), lambda_k1[layer].reshape(1, HEAD_DIM),
                        lambda_q2[layer].reshape(1, HEAD_DIM), lambda_k2[layer].reshape(1, HEAD_DIM),
                        subln_g[layer].reshape(LANES, 1),
                        bqt, bk, bvt, sg, _lambda_init(layer))
        h3 = _swa_out(sinks[layer], aqt, ak, avt, sg, h3, mixed_b,
                      w_out[layer], final_g.reshape(1, d),
                      final_norm=(layer == depth - 1))
    return h3
```

```python
import functools
import math

import jax
import jax.numpy as jnp
import numpy as np
from jax import lax
from jax.experimental import pallas as pl
from jax.experimental.pallas import tpu as pltpu

F32 = jnp.float32
BF16 = jnp.bfloat16

HEAD_DIM = 64
LANES = 128
SWA_Q_HEADS = 8
SWA_KV_HEADS = 2
WINDOW = 128
DIFF_HEADS = 4
RMS_EPS = 1e-6
SUBLN_EPS = 1e-5
NEG_BIG = -1e30
LOG2E = math.log2(math.e)

A_Q = SWA_Q_HEADS * HEAD_DIM
A_KV = SWA_KV_HEADS * HEAD_DIM
B_W = DIFF_HEADS * 2 * HEAD_DIM
MIX = A_Q + B_W
OFF_AK = A_Q
OFF_AV = OFF_AK + A_KV
OFF_BQ = OFF_AV + A_KV
OFF_BK = OFF_BQ + B_W
OFF_BV = OFF_BK + B_W
OFF_GATE = OFF_BV + B_W
IN_COLS = OFF_GATE + MIX

TM_PROJ = 1024
ROWS_IN = 256
CAST_COLS = 256
ROWS_OUT = 256
COLS_OUT = 256
HEADS_PER_CHAIN = 2
TQ_SWA = 1024
TG_DIFF = 256
NG_DIFF = 4
TQ_DIFF = NG_DIFF * TG_DIFF
TK_DIFF = 512
BF16_ROWS = 16
VROWS = LANES + BF16_ROWS
VROWS_A = HEAD_DIM + BF16_ROWS
VMEM_LIMIT = 56 * 1024 * 1024


def _lambda_init(layer_idx):
    return 0.8 - 0.6 * math.exp(-0.3 * layer_idx)


def _cast_weight_once(src_ref, dst_ref):
    @pl.when((pl.program_id(0) == 0) & (pl.program_id(1) == 0))
    def _():
        for lo in range(0, src_ref.shape[1], CAST_COLS):
            dst_ref[:, lo:lo + CAST_COLS] = src_ref[:, lo:lo + CAST_COLS].astype(BF16)


def _inproj_kernel(x_ref, g_ref, wf_ref,
                   aqt_ref, ak_ref, avt_ref, bqt_ref, bk_ref, bvt_ref, sg_ref, w_ref):
    _cast_weight_once(wf_ref, w_ref)
    qscale = HEAD_DIM ** -0.5 * LOG2E
    th = ROWS_IN
    for r in range(0, TM_PROJ, th):
        rows = slice(r, r + th)
        x = x_ref[0, rows, :]
        ms = jnp.mean(x * x, axis=-1, keepdims=True)
        h = (x * lax.rsqrt(ms + RMS_EPS) * g_ref[...]).astype(BF16)

        def tok(lo, hi, h=h):
            return jnp.dot(h, w_ref[:, lo:hi], preferred_element_type=F32)

        def feat(lo, hi, h=h):
            return lax.dot_general(w_ref[:, lo:hi], h, (((0,), (1,)), ((), ())),
                                   preferred_element_type=F32)

        ak_ref[0, rows, :] = tok(OFF_AK, OFF_AV).astype(BF16)
        bk_ref[0, rows, :] = tok(OFF_BK, OFF_BV).astype(BF16)
        gate = tok(OFF_GATE, IN_COLS)
        sg_ref[0, rows, :] = (gate * (1.0 / (1.0 + jnp.exp(-gate)))).astype(BF16)

        aqt_ref[0, :, rows] = (feat(0, OFF_AK) * qscale).astype(BF16)
        avt = feat(OFF_AV, OFF_BQ).astype(BF16)
        for g in range(SWA_KV_HEADS):
            avt_ref[0, g * VROWS_A:g * VROWS_A + HEAD_DIM, rows] = avt[g * HEAD_DIM:(g + 1) * HEAD_DIM]
            avt_ref[0, g * VROWS_A + HEAD_DIM:(g + 1) * VROWS_A, rows] = jnp.ones(
                (VROWS_A - HEAD_DIM, th), BF16)
        bqt_ref[0, r // TQ_DIFF, :, r % TQ_DIFF:r % TQ_DIFF + th] = (
            feat(OFF_BQ, OFF_BK) * qscale).astype(BF16)
        bvt = feat(OFF_BV, OFF_GATE).astype(BF16)
        vcols = slice(r % TK_DIFF, r % TK_DIFF + th)
        for hd in range(DIFF_HEADS):
            bvt_ref[0, r // TK_DIFF, hd * VROWS:hd * VROWS + LANES, vcols] = bvt[hd * LANES:(hd + 1) * LANES]
            bvt_ref[0, r // TK_DIFF, hd * VROWS + LANES:(hd + 1) * VROWS, vcols] = jnp.ones(
                (VROWS - LANES, th), BF16)


def _inproj(x, g, w):
    b, s, d = x.shape
    tm = TM_PROJ
    row = lambda bi, i: (bi, i, 0)
    col = lambda bi, i: (bi, 0, i)
    whole = lambda bi, i: (0, 0)
    bsd = lambda wd: jax.ShapeDtypeStruct((b, s, wd), BF16)
    bds = lambda wd: jax.ShapeDtypeStruct((b, wd, s), BF16)
    return pl.pallas_call(
        _inproj_kernel,
        grid=(b, s // tm),
        in_specs=[pl.BlockSpec((1, tm, d), row),
                  pl.BlockSpec((1, d), whole),
                  pl.BlockSpec(w.shape, whole)],
        out_specs=[pl.BlockSpec((1, A_Q, tm), col),
                   pl.BlockSpec((1, tm, A_KV), row),
                   pl.BlockSpec((1, SWA_KV_HEADS * VROWS_A, tm), col),
                   pl.BlockSpec((1, tm // TQ_DIFF, B_W, TQ_DIFF), lambda bi, i: (bi, i, 0, 0)),
                   pl.BlockSpec((1, tm, B_W), row),
                   pl.BlockSpec((1, tm // TK_DIFF, DIFF_HEADS * VROWS, TK_DIFF), lambda bi, i: (bi, i, 0, 0)),
                   pl.BlockSpec((1, tm, MIX), row)],
        out_shape=[bds(A_Q), bsd(A_KV), bds(SWA_KV_HEADS * VROWS_A),
                   jax.ShapeDtypeStruct((b, s // TQ_DIFF, B_W, TQ_DIFF), BF16), bsd(B_W),
                   jax.ShapeDtypeStruct((b, s // TK_DIFF, DIFF_HEADS * VROWS, TK_DIFF), BF16),
                   bsd(MIX)],
        scratch_shapes=[pltpu.VMEM(w.shape, BF16)],
        compiler_params=pltpu.CompilerParams(
            dimension_semantics=("arbitrary", "arbitrary"), vmem_limit_bytes=VMEM_LIMIT),
        name="inproj",
    )(x, g, w)


def _swa_out_kernel(sinks_ref, qt_ref, kp_ref, kc_ref, vtp_ref, vtc_ref, sg_ref,
                    x_ref, mb_ref, woutf_ref, fg_ref, o_ref, kcat, vtcat, ma_scr, wout_ref,
                    *, final_norm):
    w_ = WINDOW
    i = pl.program_id(1)
    group = SWA_Q_HEADS // SWA_KV_HEADS
    hpc = HEADS_PER_CHAIN
    nq = hpc * w_
    _cast_weight_once(woutf_ref, wout_ref)
    kcat[0:w_, :] = kp_ref[0]
    kcat[w_:, :] = kc_ref[0]
    vtcat[:, 0:w_] = vtp_ref[0]
    vtcat[:, w_:] = vtc_ref[0]

    kidx = lax.broadcasted_iota(jnp.int32, (2 * w_, nq), 0)
    qidx = lax.broadcasted_iota(jnp.int32, (2 * w_, nq), 1) & (w_ - 1)
    in_cur = kidx >= w_
    band = (in_cur & (kidx - w_ <= qidx)) | ((kidx < w_) & (kidx > qidx))
    first_band = band & ((i > 0) | in_cur)

    lane = lax.broadcasted_iota(jnp.int32, (2 * w_, LANES), 1)
    kpos = lax.broadcasted_iota(jnp.int32, (2 * w_, LANES), 0).astype(F32)
    zk = jnp.zeros((2 * w_, LANES), F32)
    kmask = (lane < HEAD_DIM, lane >= HEAD_DIM)
    kpos_lanes = (jnp.where((lane == HEAD_DIM) | (lane == HEAD_DIM + 1), kpos, zk).astype(BF16),
                  jnp.where(lane < 2, kpos, zk).astype(BF16))

    hrow = lax.broadcasted_iota(jnp.int32, (HEAD_DIM, nq), 0)
    hcol = lax.broadcasted_iota(jnp.int32, (HEAD_DIM, nq), 1)

    def slope_rows(h0):
        out = jnp.zeros((HEAD_DIM, nq), F32)
        for u in range(hpc):
            sl = 2.0 ** (-8.0 * (h0 + u + 1) / SWA_Q_HEADS) * LOG2E
            hi = float(np.asarray(sl, dtype=BF16).astype(np.float32))
            lo = sl - hi
            in_head = (hcol >= u * w_) & (hcol < (u + 1) * w_)
            out = jnp.where(in_head & (hrow == 0), hi, jnp.where(in_head & (hrow == 1), lo, out))
        return out.astype(BF16)

    srows = {h0: slope_rows(h0) for h0 in range(0, SWA_Q_HEADS, hpc)}

    qpos = (lax.broadcasted_iota(jnp.int32, (1, w_), 1) + w_).astype(F32)

    def sink_row(h0):
        return jnp.concatenate(
            [(sinks_ref[h0 + u] + 2.0 ** (-8.0 * (h0 + u + 1) / SWA_Q_HEADS) * qpos) * LOG2E
             for u in range(hpc)], axis=1)

    sinkv = {h0: sink_row(h0) for h0 in range(0, SWA_Q_HEADS, hpc)}

    def scores(w, h0):
        g = h0 // group
        keys = kcat[w * w_:(w + 2) * w_, :]
        ka = jnp.where(kmask[g], keys, kpos_lanes[g])
        qh = jnp.concatenate(
            [qt_ref[0, (h0 + u) * HEAD_DIM:(h0 + u + 1) * HEAD_DIM, w * w_:(w + 1) * w_]
             for u in range(hpc)], axis=1)
        wq = jnp.concatenate([qh, srows[h0]] if g == 0 else [srows[h0], qh], axis=0)
        return jnp.dot(ka, wq, preferred_element_type=F32)

    def consume(w, h0, s):
        g = h0 // group
        s = jnp.where(first_band if w == 0 else band, s, NEG_BIG)
        m = jnp.maximum(jnp.max(s, axis=0, keepdims=True), sinkv[h0])
        p = jnp.exp2(s - m).astype(BF16)
        vt = vtcat[g * VROWS_A:(g + 1) * VROWS_A, w * w_:(w + 2) * w_]
        o = jnp.dot(vt, p, preferred_element_type=F32)
        denom = o[HEAD_DIM:HEAD_DIM + 1] + jnp.exp2(sinkv[h0] - m)
        ot = o[0:HEAD_DIM] / denom
        rows = slice(w * w_, (w + 1) * w_)
        for pair in range(hpc // 2):
            cols = slice((h0 // 2 + pair) * LANES, (h0 // 2 + pair + 1) * LANES)
            two = jnp.concatenate([ot[:, (2 * pair) * w_:(2 * pair + 1) * w_],
                                   ot[:, (2 * pair + 1) * w_:(2 * pair + 2) * w_]], axis=0)
            ma_scr[rows, cols] = (two.T * sg_ref[0, rows, cols].astype(F32)).astype(BF16)

    npiece = MIX // COLS_OUT

    def project(k, j):
        rows = slice(k * ROWS_OUT, (k + 1) * ROWS_OUT)
        cols = slice(j * COLS_OUT, (j + 1) * COLS_OUT)
        o_ref[0, rows, cols] = (
            x_ref[0, rows, cols]
            + jnp.dot(ma_scr[rows, :], wout_ref[0:A_Q, cols], preferred_element_type=F32)
            + jnp.dot(mb_ref[0, rows, :], wout_ref[A_Q:MIX, cols], preferred_element_type=F32))
        if final_norm and j == npiece - 1:
            y = o_ref[0, rows, :]
            ms = jnp.mean(y * y, axis=-1, keepdims=True)
            o_ref[0, rows, :] = y * lax.rsqrt(ms + RMS_EPS) * fg_ref[...]

    chains = [(w, h0) for w in range(TQ_SWA // w_) for h0 in range(0, SWA_Q_HEADS, hpc)]
    chains_per_chunk = (ROWS_OUT // w_) * (SWA_Q_HEADS // hpc)
    ahead = 4
    pending = [scores(*ch) for ch in chains[:ahead]]
    ready = []
    for n, (w, h0) in enumerate(chains):
        if n + ahead < len(chains):
            pending.append(scores(*chains[n + ahead]))
        consume(w, h0, pending.pop(0))
        if (n + 1) % chains_per_chunk == 0:
            ready += [(n // chains_per_chunk, j) for j in range(npiece)]
        if ready:
            project(*ready.pop(0))
    for unit in ready:
        project(*unit)


def _swa_out(sinks, aqt, ak, avt, sg, x, mixed_b, wout, fg, final_norm):
    b, s, d = x.shape
    tq = TQ_SWA
    wpt = tq // WINDOW
    vr = SWA_KV_HEADS * VROWS_A
    prev_w = lambda i: jnp.maximum(i * wpt - 1, 0)
    tile = lambda bi, i: (bi, i, 0)
    whole = lambda bi, i: (0, 0)
    return pl.pallas_call(
        functools.partial(_swa_out_kernel, final_norm=final_norm),
        grid=(b, s // tq),
        in_specs=[pl.BlockSpec(memory_space=pltpu.SMEM),
                  pl.BlockSpec((1, A_Q, tq), lambda bi, i: (bi, 0, i)),
                  pl.BlockSpec((1, WINDOW, A_KV), lambda bi, i: (bi, prev_w(i), 0)),
                  pl.BlockSpec((1, tq, A_KV), tile),
                  pl.BlockSpec((1, vr, WINDOW), lambda bi, i: (bi, 0, prev_w(i))),
                  pl.BlockSpec((1, vr, tq), lambda bi, i: (bi, 0, i)),
                  pl.BlockSpec((1, tq, A_Q), tile),
                  pl.BlockSpec((1, tq, d), tile),
                  pl.BlockSpec((1, tq, B_W), tile),
                  pl.BlockSpec((MIX, d), whole),
                  pl.BlockSpec((1, d), whole)],
        out_specs=pl.BlockSpec((1, tq, d), tile),
        out_shape=jax.ShapeDtypeStruct((b, s, d), F32),
        scratch_shapes=[pltpu.VMEM((tq + WINDOW, A_KV), BF16),
                        pltpu.VMEM((vr, tq + WINDOW), BF16),
                        pltpu.VMEM((tq, A_Q), BF16),
                        pltpu.VMEM((MIX, d), BF16)],
        compiler_params=pltpu.CompilerParams(
            dimension_semantics=("arbitrary", "arbitrary"), vmem_limit_bytes=VMEM_LIMIT),
        name="swa_out",
    )(sinks, aqt, ak, ak, avt, avt, sg, x, mixed_b, wout, fg)


def _diff_kernel(slopes_ref, lq1_ref, lk1_ref, lq2_ref, lk2_ref, subg_ref,
                 qt_ref, k_ref, vt_ref, sg_ref, o_ref,
                 qa_ref, s_ref, m_ref, acc_ref, *, lambda_init):
    tg, tk, tq = TG_DIFF, TK_DIFF, TQ_DIFF
    nchain = 2 * NG_DIFF
    ntile = qt_ref.shape[1]
    hd = pl.program_id(1)
    slope = slopes_ref[hd] * LOG2E

    def key_consts(n):
        kk = lax.broadcasted_iota(jnp.int32, (n, LANES), 0).astype(F32)
        lane = lax.broadcasted_iota(jnp.int32, (n, LANES), 1)
        ab = slope * kk
        ab_hi = ab.astype(BF16).astype(F32)
        ab_lo = ab - ab_hi
        zk = jnp.zeros((n, LANES), F32)
        bias1 = jnp.where(lane == HEAD_DIM, ab_hi,
                          jnp.where(lane == HEAD_DIM + 1, ab_lo, zk)).astype(BF16)
        bias2 = jnp.where(lane == 0, ab_hi, jnp.where(lane == 1, ab_lo, zk)).astype(BF16)
        return lane < HEAD_DIM, bias1, bias2

    lam = (jnp.exp(jnp.sum(lq1_ref[...] * lk1_ref[...], axis=-1, keepdims=True))
           - jnp.exp(jnp.sum(lq2_ref[...] * lk2_ref[...], axis=-1, keepdims=True))
           + lambda_init)

    def build_queries(t):
        rowq = lax.broadcasted_iota(jnp.int32, (LANES, tg), 0)
        one = jnp.ones((LANES, tg), F32)
        zq = jnp.zeros((LANES, tg), F32)
        for g in range(NG_DIFF):
            qt = qt_ref[0, t, :, g * tg:(g + 1) * tg].astype(F32)
            qa_ref[t, 2 * g] = jnp.where(
                rowq < HEAD_DIM, qt, jnp.where(rowq < HEAD_DIM + 2, one, zq)).astype(BF16)
            qa_ref[t, 2 * g + 1] = jnp.where(
                rowq >= HEAD_DIM, qt, jnp.where(rowq < 2, one, zq)).astype(BF16)

    def tile(t):
        first_tile = t == 0
        slot = t

        def reset(c):
            m_ref[slot, c] = jnp.full(m_ref.shape[2:], NEG_BIG, F32)
            acc_ref[slot, c] = jnp.zeros(acc_ref.shape[2:], F32)

        def scores(buf, c, ka, n):
            s_ref[buf, c, 0:n, :] = jnp.dot(ka, qa_ref[slot, c], preferred_element_type=F32)

        kconst = {n: key_consts(n) for n in ((tg,) if first_tile else (tg, tk))}
        krow = lax.broadcasted_iota(jnp.int32, (tg, tg), 0)
        qcol = lax.broadcasted_iota(jnp.int32, (tg, tg), 1)
        causal = krow <= qcol

        def consume(buf, c, vt, cj, n, masked):
            s = s_ref[buf, c, 0:n, :]
            if masked:
                s = jnp.where(causal, s, NEG_BIG)
            m_old = m_ref[slot, c]
            m_new = jnp.maximum(m_old, jnp.max(s, axis=0, keepdims=True) + cj)
            alpha = jnp.exp2(m_old - m_new)
            p = jnp.exp2(s - (m_new - cj)).astype(BF16)
            acc_ref[slot, c] = alpha * acc_ref[slot, c] + jnp.dot(vt, p, preferred_element_type=F32)
            m_ref[slot, c] = m_new

        def keys(start, n):
            k = k_ref[0, start:start + n, :]
            lane_lo, bias1, bias2 = kconst[n]
            return jnp.where(lane_lo, k, bias1), jnp.where(lane_lo, bias2, k)

        def block_bias(start):
            return slope * float(start)

        def finish(g):
            a1 = acc_ref[slot, 2 * g]
            a2 = acc_ref[slot, 2 * g + 1]
            ot = (a1[0:LANES] / a1[LANES:LANES + 1]
                  - lam * (a2[0:LANES] / a2[LANES:LANES + 1]))
            ot = ot * lax.rsqrt(jnp.mean(ot * ot, axis=0, keepdims=True) + SUBLN_EPS) * subg_ref[...]
            ot = ot * (1.0 - lambda_init)
            rows = slice(t * tq + g * tg, t * tq + (g + 1) * tg)
            o_ref[0, rows, :] = (ot.T * sg_ref[0, rows, :].astype(F32)).astype(BF16)

        base = t * tq
        build_queries(t)
        ka = keys(base, tg)
        for c in range(nchain):
            reset(c)
            scores(0, c, ka[c % 2], tg)
        ka_first = None if first_tile else keys(0, tk)
        for jj in range(NG_DIFF):
            cur, nxt = jj % 2, 1 - jj % 2
            start = base + jj * tg
            vt = vt_ref[0, t * (tq // tk) + (jj * tg) // tk, :,
                        (jj * tg) % tk:(jj * tg) % tk + tg]
            cj = block_bias(start)
            ka = keys(start + tg, tg) if jj + 1 < NG_DIFF else None
            for g in range(jj, NG_DIFF):
                for c in (2 * g, 2 * g + 1):
                    if g > jj:
                        scores(nxt, c, ka[c % 2], tg)
                    consume(cur, c, vt, cj, tg, g == jj)
                    if g == jj and not first_tile:
                        scores(0, c, ka_first[c % 2], tk)
                if g == jj and first_tile:
                    finish(g)
        nblk = base // tk
        for j in range(nblk):
            cur = j % 2
            last = j == nblk - 1
            vt = vt_ref[0, j]
            cj = block_bias(j * tk)
            kan = None if last else keys((j + 1) * tk, tk)
            for g in range(NG_DIFF):
                for c in (2 * g, 2 * g + 1):
                    if not last:
                        scores(1 - cur, c, kan[c % 2], tk)
                    consume(cur, c, vt, cj, tk, False)
                if last:
                    finish(g)

    arms = [list(range(min(3, ntile)))] + [[n] for n in range(3, ntile)]

    def run(tiles):
        for n in tiles:
            tile(n)

    def arm_body(a, carry):
        for n, tiles in enumerate(arms):
            pl.when(a == n)(functools.partial(run, tiles))
        return carry

    lax.fori_loop(0, len(arms), arm_body, 0)


def _diff(slopes, lq1, lk1, lq2, lk2, subg, bqt, bk, bvt, sg, lambda_init):
    b, s, _ = bk.shape
    tg, tk, tq = TG_DIFF, TK_DIFF, TQ_DIFF
    nchain = 2 * NG_DIFF
    smem = pl.BlockSpec(memory_space=pltpu.SMEM)
    small = lambda shape: pl.BlockSpec(shape, lambda bi, h: (0, 0))
    return pl.pallas_call(
        functools.partial(_diff_kernel, lambda_init=lambda_init),
        grid=(b, DIFF_HEADS),
        in_specs=[smem,
                  small((1, HEAD_DIM)), small((1, HEAD_DIM)),
                  small((1, HEAD_DIM)), small((1, HEAD_DIM)),
                  small((LANES, 1)),
                  pl.BlockSpec((1, s // tq, LANES, tq), lambda bi, h: (bi, 0, h, 0)),
                  pl.BlockSpec((1, s, LANES), lambda bi, h: (bi, 0, h)),
                  pl.BlockSpec((1, s // tk, VROWS, tk), lambda bi, h: (bi, 0, h, 0)),
                  pl.BlockSpec((1, s, LANES), lambda bi, h: (bi, 0, A_Q // LANES + h))],
        out_specs=pl.BlockSpec((1, s, LANES), lambda bi, h: (bi, 0, h)),
        out_shape=jax.ShapeDtypeStruct((b, s, B_W), BF16),
        scratch_shapes=[pltpu.VMEM((s // tq, nchain, LANES, tg), BF16),
                        pltpu.VMEM((2, nchain, tk, tg), F32),
                        pltpu.VMEM((s // tq, nchain, 1, tg), F32),
                        pltpu.VMEM((s // tq, nchain, VROWS, tg), F32)],
        compiler_params=pltpu.CompilerParams(
            dimension_semantics=("arbitrary", "arbitrary"),
            vmem_limit_bytes=VMEM_LIMIT),
        name="diffattn",
    )(slopes, lq1, lk1, lq2, lk2, subg, bqt, bk, bvt, sg)


def kernel(x, norm_g, w_in, sinks, lambda_q1, lambda_k1, lambda_q2, lambda_k2,
           subln_g, w_out, final_g):
    b, s, d = x.shape
    depth = norm_g.shape[0]
    diff_slopes = jnp.asarray(
        [2.0 ** (-8.0 * (h + 1) / DIFF_HEADS) for h in range(DIFF_HEADS)], F32)
    h3 = x
    for layer in range(depth):
        aqt, ak, avt, bqt, bk, bvt, sg = _inproj(
            h3, norm_g[layer].reshape(1, d), w_in[layer])
        mixed_b = _diff(diff_slopes,
                        lambda_q1[layer].reshape(1, HEAD_DIM), lambda_k1[layer].reshape(1, HEAD_DIM),
                        lambda_q2[layer].reshape(1, HEAD_DIM), lambda_k2[layer].reshape(1, HEAD_DIM),
                        subln_g[layer].reshape(LANES, 1),
                        bqt, bk, bvt, sg, _lambda_init(layer))
        h3 = _swa_out(sinks[layer], aqt, ak, avt, sg, h3, mixed_b,
                      w_out[layer], final_g.reshape(1, d),
                      final_norm=(layer == depth - 1))
    return h3
```
